```python
import jax, jax.numpy as jnp
from jax import lax
import numpy as np

D_MODEL = 1024
BATCH = 4
SEQ = 4096
DEPTH = 1

HEAD_DIM = 128
DSA_HEADS = 4
MOBA_HEADS = 4
IDX_HEADS = 8
IDX_DIM = 64
DSA_TOPK_MAX = 256
MOBA_BLOCK = 256
MOBA_TOPK = 3
D_FF = -(-8 * D_MODEL // 768) * 256
ROPE_THETA = 10000.0
EPS = 1e-6
NEG = -1e30
DSA_Q_BLOCK = 128
MOBA_Q_BLOCK = 32
DSA_W = DSA_HEADS * HEAD_DIM
MOBA_W = MOBA_HEADS * HEAD_DIM
IN_SIZES = (DSA_W, DSA_W, DSA_W, IDX_HEADS * IDX_DIM, IDX_DIM, IDX_HEADS,
            MOBA_W, MOBA_W, MOBA_W, D_MODEL, D_MODEL)
IN_W = sum(IN_SIZES)
N_MOD = 6

kernel_name = "hybrid_dsa_moba_gated_block"


def rms_norm(x, gain):
    xf = x.astype(jnp.float32)
    y = xf * lax.rsqrt(jnp.mean(xf * xf, axis=-1, keepdims=True) + EPS)
    return (y * gain.astype(jnp.float32)).astype(x.dtype)


def rope_tables(seq, dim):
    inv = ROPE_THETA ** (-jnp.arange(0, dim, 2, dtype=jnp.float32) / dim)
    ang = jnp.arange(seq, dtype=jnp.float32)[:, None] * inv[None, :]
    return jnp.cos(ang)[:, None, :], jnp.sin(ang)[:, None, :]


def apply_rope(x, cos, sin):
    x1, x2 = jnp.split(x.astype(jnp.float32), 2, axis=-1)
    return jnp.concatenate([x1 * cos - x2 * sin, x2 * cos + x1 * sin], axis=-1).astype(x.dtype)


def dsa_attention(q, k, v, qi, ki, wi):
    B, S, H, Dh = q.shape
    topk = min(DSA_TOPK_MAX, S // 4)
    nq = S // DSA_Q_BLOCK
    scale = Dh ** -0.5
    key_pos = jnp.arange(S)
    bi = jnp.arange(B)[:, None, None]

    def chunks(a):
        return jnp.moveaxis(a.reshape(a.shape[0], nq, DSA_Q_BLOCK, *a.shape[2:]), 1, 0)

    def block(args):
        qb, qib, wib, t = args
        logits = jnp.einsum('bqhd,bsd->bqhs', qib, ki).astype(jnp.float32)
        score = jnp.einsum('bqh,bqhs->bqs', wib.astype(jnp.float32), jax.nn.relu(logits))
        score = jnp.where((key_pos[None, :] <= t[:, None])[None], score, NEG)
        _, sel = lax.top_k(score, topk)
        ks = k[bi, sel]
        vs = v[bi, sel]
        s = jnp.einsum('bqhd,bqkhd->bhqk', qb, ks).astype(jnp.float32) * scale
        valid = sel <= t[None, :, None]
        s = jnp.where(valid[:, None], s, NEG)
        p = jax.nn.softmax(s, axis=-1).astype(v.dtype)
        return jnp.einsum('bhqk,bqkhd->bqhd', p, vs)

    t_c = jnp.arange(S).reshape(nq, DSA_Q_BLOCK)
    out = lax.map(block, (chunks(q), chunks(qi), chunks(wi), t_c))
    return jnp.moveaxis(out, 0, 1).reshape(B, S, H * Dh)


def moba_attention(q, k, v):
    B, S, H, Dh = q.shape
    nb = -(-S // MOBA_BLOCK)
    pad = nb * MOBA_BLOCK - S
    scale = Dh ** -0.5
    qh = q.transpose(0, 2, 1, 3)
    padw = ((0, 0), (0, 0), (0, pad), (0, 0))
    kh = jnp.pad(k.transpose(0, 2, 1, 3), padw).reshape(B, H, nb, MOBA_BLOCK, Dh)
    vh = jnp.pad(v.transpose(0, 2, 1, 3), padw).reshape(B, H, nb, MOBA_BLOCK, Dh)
    own = (jnp.arange(S) // MOBA_BLOCK).astype(jnp.int32)
    own_b = jnp.broadcast_to(own[:, None], (B, H, S, 1))
    n_sel = min(MOBA_TOPK, nb - 1)
    if n_sel > 0:
        kmean = jnp.mean(kh.astype(jnp.float32), axis=3)
        gate = jnp.einsum('bhsd,bhnd->bhsn', qh.astype(jnp.float32), kmean)
        past = jnp.arange(nb)[None, :] < own[:, None]
        gate = jnp.where(past, gate, NEG)
        _, sel = lax.top_k(gate, n_sel)
        blocks = jnp.concatenate([sel.astype(jnp.int32), own_b], axis=-1)
        bvalid = jnp.concatenate([sel < own[:, None], jnp.ones((B, H, S, 1), bool)], axis=-1)
    else:
        blocks = own_b
        bvalid = jnp.ones((B, H, S, 1), bool)
    nq = S // MOBA_Q_BLOCK
    bi = jnp.arange(B)[:, None, None, None]
    hi = jnp.arange(H)[None, :, None, None]
    offs = jnp.arange(MOBA_BLOCK)

    def chunks(a):
        return jnp.moveaxis(a.reshape(B, H, nq, MOBA_Q_BLOCK, *a.shape[3:]), 2, 0)

    def block(args):
        qb, bl, bv, t = args
        kg = kh[bi, hi, bl]
        vg = vh[bi, hi, bl]
        s = jnp.einsum('bhqd,bhqnkd->bhqnk', qb, kg).astype(jnp.float32) * scale
        kpos = bl[..., None] * MOBA_BLOCK + offs
        mask = bv[..., None] & (kpos <= t[:, None, None])
        s = jnp.where(mask, s, NEG)
        shp = s.shape
        p = jax.nn.softmax(s.reshape(shp[0], shp[1], shp[2], -1), axis=-1).reshape(shp).astype(v.dtype)
        return jnp.einsum('bhqnk,bhqnkd->bhqd', p, vg)

    t_c = jnp.arange(S).reshape(nq, MOBA_Q_BLOCK)
    out = lax.map(block, (chunks(qh), chunks(blocks), chunks(bvalid), t_c))
    out = jnp.moveaxis(out, 0, 2).reshape(B, H, S, Dh)
    return out.transpose(0, 2, 1, 3).reshape(B, S, H * Dh)


def setup_inputs(seed: int = 0) -> dict:
    key = jax.random.key(seed)
    ks = jax.random.split(key, 16)
    f32 = jnp.float32

    def w(k, shape, fan_in):
        return jax.random.normal(k, shape, f32) * fan_in ** -0.5

    def gain(k, shape):
        return 1.0 + 0.02 * jax.random.normal(k, shape, f32)

    return {
        "x": jax.random.normal(ks[0], (BATCH, SEQ, D_MODEL), f32),
        "c": jax.random.normal(ks[1], (BATCH, D_MODEL), f32),
        "w_mod": w(ks[2], (DEPTH, D_MODEL, N_MOD * D_MODEL), D_MODEL),
        "b_mod": 0.02 * jax.random.normal(ks[3], (DEPTH, N_MOD * D_MODEL), f32),
        "g_mix_norm": gain(ks[4], (DEPTH, D_MODEL)),
        "g_ffn_norm": gain(ks[5], (DEPTH, D_MODEL)),
        "w_in": w(ks[6], (DEPTH, D_MODEL, IN_W), D_MODEL),
        "g_q_dsa": gain(ks[7], (DEPTH, HEAD_DIM)),
        "g_k_dsa": gain(ks[8], (DEPTH, HEAD_DIM)),
        "g_q_moba": gain(ks[9], (DEPTH, HEAD_DIM)),
        "g_k_moba": gain(ks[10], (DEPTH, HEAD_DIM)),
        "w_br_dsa": w(ks[11], (DEPTH, DSA_W, D_MODEL), DSA_W),
        "w_br_moba": w(ks[12], (DEPTH, MOBA_W, D_MODEL), MOBA_W),
        "w_out": w(ks[13], (DEPTH, D_MODEL, D_MODEL), D_MODEL),
        "w_gate_up": w(ks[14], (DEPTH, D_MODEL, 2 * D_FF), D_MODEL),
        "w_down": w(ks[15], (DEPTH, D_FF, D_MODEL), D_FF),
    }


def reference(x, c, w_mod, b_mod, g_mix_norm, g_ffn_norm, w_in, g_q_dsa, g_k_dsa,
              g_q_moba, g_k_moba, w_br_dsa, w_br_moba, w_out, w_gate_up, w_down):
    B, S, _ = x.shape
    cos_h, sin_h = rope_tables(S, HEAD_DIM)
    cos_i, sin_i = rope_tables(S, IDX_DIM)
    split_at = np.cumsum(IN_SIZES)[:-1].tolist()
    idx_w_scale = (IDX_HEADS ** -0.5) * (IDX_DIM ** -0.5)
    for l in range(DEPTH):
        mod = jax.nn.silu(c) @ w_mod[l] + b_mod[l]
        sh_m, sc_m, gt_m, sh_f, sc_f, gt_f = [m[:, None, :] for m in jnp.split(mod, N_MOD, axis=-1)]

        h = rms_norm(x, g_mix_norm[l]) * (1.0 + sc_m) + sh_m
        proj = h @ w_in[l]
        (q_a, k_a, v_a, q_i, k_i, w_i, q_b, k_b, v_b, gate_a, gate_b) = jnp.split(proj, split_at, axis=-1)
        q_a = apply_rope(rms_norm(q_a.reshape(B, S, DSA_HEADS, HEAD_DIM), g_q_dsa[l]), cos_h, sin_h)
        k_a = apply_rope(rms_norm(k_a.reshape(B, S, DSA_HEADS, HEAD_DIM), g_k_dsa[l]), cos_h, sin_h)
        v_a = v_a.reshape(B, S, DSA_HEADS, HEAD_DIM)
        q_i = apply_rope(q_i.reshape(B, S, IDX_HEADS, IDX_DIM), cos_i, sin_i)
        k_i = apply_rope(k_i[:, :, None, :], cos_i, sin_i)[:, :, 0, :]
        w_i = w_i * idx_w_scale
        o_a = dsa_attention(q_a, k_a, v_a, q_i, k_i, w_i)

        q_b = apply_rope(rms_norm(q_b.reshape(B, S, MOBA_HEADS, HEAD_DIM), g_q_moba[l]), cos_h, sin_h)
        k_b = apply_rope(rms_norm(k_b.reshape(B, S, MOBA_HEADS, HEAD_DIM), g_k_moba[l]), cos_h, sin_h)
        v_b = v_b.reshape(B, S, MOBA_HEADS, HEAD_DIM)
        o_b = moba_attention(q_b, k_b, v_b)

        merged = (jax.nn.sigmoid(gate_a) * (o_a @ w_br_dsa[l])
                  + jax.nn.sigmoid(gate_b) * (o_b @ w_br_moba[l]))
        x = x + gt_m * (merged @ w_out[l])

        h = rms_norm(x, g_ffn_norm[l]) * (1.0 + sc_f) + sh_f
        g_ff, u_ff = jnp.split(h @ w_gate_up[l], 2, axis=-1)
        x = x + gt_f * ((jax.nn.silu(g_ff) * u_ff) @ w_down[l])
    return x
```

```python
import functools

import jax
import jax.numpy as jnp
from jax import lax
from jax.experimental import pallas as pl
from jax.experimental.pallas import tpu as pltpu

F32 = jnp.float32
BF16 = jnp.bfloat16
I32 = jnp.int32

D_MODEL = 1024
HEAD_DIM = 128
N_HEADS = 4
BR_W = N_HEADS * HEAD_DIM
IDX_HEADS = 8
IDX_DIM = 64
DSA_TOPK_MAX = 256
MOBA_BLOCK = 256
MOBA_TOPK = 3
D_FF = 2816
ROPE_THETA = 10000.0
EPS = 1e-6
NEG = -1e30
M_INIT = -1e29
N_MOD = 6
IDX_OUT_W = 640
IDX_K = 256
INT_MIN = -(2 ** 31)

LANES = 128
TQ = 256
TM_PROJ = 512
TM_FFN = 512
FF_CHUNK = 256
SEL_ROWS = 64
VMEM_LIMIT = 56 * 1024 * 1024

_NT = (((1,), (1,)), ((), ()))


def _split_bf16(x):
    hi = x.astype(BF16)
    lo = (x - hi.astype(F32)).astype(BF16)
    return hi, lo


def _dot(a, b):
    return jnp.dot(a, b, preferred_element_type=F32)


def _dot_nt(a, b):
    return lax.dot_general(a, b, _NT, preferred_element_type=F32)


def _mod_kernel(c_ref, w_ref, b_ref, o_ref):
    c = c_ref[...]
    a_hi, a_lo = _split_bf16(c * jax.nn.sigmoid(c))
    w_hi, w_lo = _split_bf16(w_ref[...])
    o_ref[...] = _dot(a_hi, w_hi) + _dot(a_lo, w_hi) + _dot(a_hi, w_lo) + b_ref[...]


def _mod_call(c, w_mod, b_mod):
    bsz, d = c.shape
    n = w_mod.shape[1]
    tn = 1024
    return pl.pallas_call(
        _mod_kernel,
        grid=(n // tn,),
        in_specs=[
            pl.BlockSpec((bsz, d), lambda i: (0, 0)),
            pl.BlockSpec((d, tn), lambda i: (0, i)),
            pl.BlockSpec((1, tn), lambda i: (0, i)),
        ],
        out_specs=pl.BlockSpec((bsz, tn), lambda i: (0, i)),
        out_shape=jax.ShapeDtypeStruct((bsz, n), F32),
        compiler_params=pltpu.CompilerParams(vmem_limit_bytes=VMEM_LIMIT),
        name="mod",
    )(c, w_mod, b_mod.reshape(1, n))


def _rope_partner_64(x):
    lane = lax.broadcasted_iota(I32, x.shape, 1)
    return jnp.where((lane % IDX_DIM) < IDX_DIM // 2,
                     pltpu.roll(x, LANES - IDX_DIM // 2, 1), pltpu.roll(x, IDX_DIM // 2, 1))


def _proj_kernel(x_ref, mod_ref, gmix_ref, wqk_ref, wvt_ref, wg_ref, wih_ref, wil_ref, gains_ref,
                 cosh_ref, sinh_ref, cosi_ref, sini_ref,
                 qa_ref, ka_ref, qb_ref, kb_ref, vat_ref, vbt_ref, ga_ref, gb_ref, idx_ref, kmean_ref):
    x = x_ref[0]
    y = x * lax.rsqrt(jnp.mean(x * x, axis=-1, keepdims=True) + EPS) * gmix_ref[...]
    h = y * (1.0 + mod_ref[0, 1:2, :]) + mod_ref[0, 0:1, :]
    h_hi, h_lo = _split_bf16(h)
    tm = x.shape[0]
    cosh, sinh = cosh_ref[...], sinh_ref[...]
    scale = HEAD_DIM ** -0.5

    for gi, out_ref in enumerate((qa_ref, ka_ref, qb_ref, kb_ref)):
        p = _dot(h_hi, wqk_ref[:, gi * BR_W:(gi + 1) * BR_W])
        for hh in range(N_HEADS):
            ph = p[:, hh * HEAD_DIM:(hh + 1) * HEAD_DIM]
            yh = ph * lax.rsqrt(jnp.mean(ph * ph, axis=-1, keepdims=True) + EPS) * gains_ref[gi:gi + 1, :]
            r = yh * cosh + pltpu.roll(yh, HEAD_DIM // 2, 1) * sinh
            if gi == 3:
                for blk in range(tm // MOBA_BLOCK):
                    kmean_ref[0, blk, :, hh * HEAD_DIM:(hh + 1) * HEAD_DIM] = jnp.mean(
                        r[blk * MOBA_BLOCK:(blk + 1) * MOBA_BLOCK], axis=0, keepdims=True)
            if gi % 2 == 0:
                r = r * scale
            out_ref[0, :, hh * HEAD_DIM:(hh + 1) * HEAD_DIM] = r.astype(BF16)

    vt = _dot_nt(wvt_ref[...], h_hi)
    for blk in range(tm // TQ):
        vat_ref[0, blk] = vt[:BR_W, blk * TQ:(blk + 1) * TQ].astype(BF16)
        vbt_ref[0, blk] = vt[BR_W:, blk * TQ:(blk + 1) * TQ].astype(BF16)

    for gi, out_ref in enumerate((ga_ref, gb_ref)):
        g = _dot(h_hi, wg_ref[:, gi * D_MODEL:(gi + 1) * D_MODEL])
        out_ref[0] = jax.nn.sigmoid(g).astype(BF16)

    pi = _dot(h_hi, wih_ref[...]) + _dot(h_lo, wih_ref[...]) + _dot(h_hi, wil_ref[...])
    cosi, sini = cosi_ref[...], sini_ref[...]
    n_q = IDX_HEADS * IDX_DIM // LANES
    for g4 in range(n_q):
        xg = pi[:, g4 * LANES:(g4 + 1) * LANES]
        idx_ref[0, :, g4 * LANES:(g4 + 1) * LANES] = xg * cosi + _rope_partner_64(xg) * sini
    xg = pi[:, n_q * LANES:(n_q + 1) * LANES]
    lane = lax.broadcasted_iota(I32, xg.shape, 1)
    w_scale = (IDX_HEADS ** -0.5) * (IDX_DIM ** -0.5)
    roped = xg * cosi + _rope_partner_64(xg) * sini
    idx_ref[0, :, n_q * LANES:(n_q + 1) * LANES] = jnp.where(
        lane < IDX_DIM, roped, jnp.where(lane < IDX_DIM + IDX_HEADS, xg * w_scale, 0.0))


def _proj_call(x, mod3, gmix, wqk, wvt, wg, wih, wil, gains, cosh, sinh, cosi, sini):
    bsz, seq, d = x.shape
    tm = TM_PROJ
    nb = seq // MOBA_BLOCK
    const = lambda b, i: (0, 0)
    tok = lambda b, i: (b, i, 0)
    tab = lambda b, i: (i, 0)
    out_shape = (
        [jax.ShapeDtypeStruct((bsz, seq, BR_W), BF16)] * 4
        + [jax.ShapeDtypeStruct((bsz, seq // TQ, BR_W, TQ), BF16)] * 2
        + [jax.ShapeDtypeStruct((bsz, seq, D_MODEL), BF16)] * 2
        + [jax.ShapeDtypeStruct((bsz, seq, IDX_OUT_W), F32),
           jax.ShapeDtypeStruct((bsz, nb, 1, BR_W), F32)]
    )
    out_specs = (
        [pl.BlockSpec((1, tm, BR_W), tok)] * 4
        + [pl.BlockSpec((1, tm // TQ, BR_W, TQ), lambda b, i: (b, i, 0, 0))] * 2
        + [pl.BlockSpec((1, tm, D_MODEL), tok)] * 2
        + [pl.BlockSpec((1, tm, IDX_OUT_W), tok),
           pl.BlockSpec((1, tm // MOBA_BLOCK, 1, BR_W), lambda b, i: (b, i, 0, 0))]
    )
    return pl.pallas_call(
        _proj_kernel,
        grid=(bsz, seq // tm),
        in_specs=[
            pl.BlockSpec((1, tm, d), tok),
            pl.BlockSpec((1, N_MOD, d), lambda b, i: (b, 0, 0)),
            pl.BlockSpec((1, d), const),
            pl.BlockSpec(wqk.shape, const),
            pl.BlockSpec(wvt.shape, const),
            pl.BlockSpec(wg.shape, const),
            pl.BlockSpec(wih.shape, const),
            pl.BlockSpec(wil.shape, const),
            pl.BlockSpec(gains.shape, const),
            pl.BlockSpec((tm, LANES), tab),
            pl.BlockSpec((tm, LANES), tab),
            pl.BlockSpec((tm, LANES), tab),
            pl.BlockSpec((tm, LANES), tab),
        ],
        out_specs=out_specs,
        out_shape=out_shape,
        compiler_params=pltpu.CompilerParams(
            dimension_semantics=("arbitrary", "arbitrary"), vmem_limit_bytes=VMEM_LIMIT),
        name="proj",
    )(x, mod3, gmix, wqk, wvt, wg, wih, wil, gains, cosh, sinh, cosi, sini)


def _online_softmax_step(s, vt, m_ref, l_ref, acc_ref, hh):
    m_old = m_ref[hh]
    m_new = jnp.maximum(m_old, jnp.max(s, axis=0, keepdims=True))
    alpha = jnp.exp(m_old - m_new)
    p = jnp.exp(s - m_new)
    l_ref[hh] = alpha * l_ref[hh] + jnp.sum(p, axis=0, keepdims=True)
    acc_ref[hh] = alpha * acc_ref[hh] + _dot(vt, p.astype(BF16))
    m_ref[hh] = m_new


def _init_softmax_state(m_ref, l_ref, acc_ref):
    m_ref[...] = jnp.full(m_ref.shape, M_INIT, F32)
    l_ref[...] = jnp.zeros(l_ref.shape, F32)
    acc_ref[...] = jnp.zeros(acc_ref.shape, F32)


def _write_heads(o_ref, l_ref, acc_ref):
    for hh in range(N_HEADS):
        o_t = acc_ref[hh] / l_ref[hh]
        o_ref[0, :, hh * HEAD_DIM:(hh + 1) * HEAD_DIM] = o_t.T.astype(BF16)


def _sortable_key(score):
    bits = lax.bitcast_convert_type(score, I32)
    key = bits ^ ((bits >> 31) & jnp.int32(0x7FFFFFFF))
    return jnp.where(key == -1, 0, key)


def _dsa_kernel(a_ref, kc_ref, wk_ref, qa_ref, ka_ref, vat_ref, o_ref,
                keys_ref, m_ref, l_ref, acc_ref, *, topk, seq_bits):
    j = pl.program_id(1)
    n_tiles = j + 1
    w_t = wk_ref[0].T
    q_pos = j * TQ + lax.broadcasted_iota(I32, (TQ, TQ), 1)
    row = lax.broadcasted_iota(I32, (TQ, TQ), 0)

    def score_tile(c, carry):
        k0 = pl.multiple_of(c * TQ, TQ)
        kc = kc_ref[0, pl.ds(k0, TQ), :]
        acc = jnp.zeros((TQ, TQ), F32)
        for hh in range(IDX_HEADS):
            lg = _dot_nt(kc, a_ref[0, :, hh * IDX_K:(hh + 1) * IDX_K])
            acc = acc + w_t[IDX_DIM + hh:IDX_DIM + hh + 1, :] * jnp.maximum(lg, 0.0)
        acc = jnp.where(k0 + row <= q_pos, acc, NEG)
        keys_ref[pl.ds(k0, TQ), :] = _sortable_key(acc)
        return carry

    lax.fori_loop(0, n_tiles, score_tile, 0)

    n_steps = n_tiles * (TQ // SEL_ROWS)

    def count(pred):
        def body(c, cnt):
            r0 = pl.multiple_of(c * SEL_ROWS, SEL_ROWS)
            return cnt + jnp.where(pred(keys_ref[pl.ds(r0, SEL_ROWS), :], r0), 1.0, 0.0)
        cnt = lax.fori_loop(0, n_steps, body, jnp.zeros((SEL_ROWS, TQ), F32))
        return jnp.sum(cnt, axis=0, keepdims=True)

    kf = float(topk)
    c0 = count(lambda kk, r0: kk >= 0)
    cand = jnp.where(c0 >= kf, 0, INT_MIN).astype(I32)
    cge = jnp.where(c0 >= kf, c0, (n_tiles * TQ).astype(F32))

    def bit_step(i, carry):
        cand, cge = carry
        test = cand | (jnp.int32(1) << (30 - i))
        c = count(lambda kk, r0: kk >= test)
        ok = c >= kf
        return jnp.where(ok, test, cand), jnp.where(ok, c, cge)

    cand, cge = lax.fori_loop(0, 31, bit_step, (cand, cge))

    @pl.when(jnp.max(jnp.abs(cge - kf)) > 0.0)
    def _():
        need = kf - count(lambda kk, r0: kk > cand)
        sub = lax.broadcasted_iota(I32, (SEL_ROWS, TQ), 0)

        def cut_step(i, cut):
            test = cut | (jnp.int32(1) << (seq_bits - 1 - i))
            f = count(lambda kk, r0: (kk == cand) & (r0 + sub < test))
            return jnp.where(f < need, test, cut)

        cut = lax.fori_loop(0, seq_bits, cut_step, jnp.zeros((1, TQ), I32))

        def demote(c, carry):
            r0 = pl.multiple_of(c * SEL_ROWS, SEL_ROWS)
            kk = keys_ref[pl.ds(r0, SEL_ROWS), :]
            keys_ref[pl.ds(r0, SEL_ROWS), :] = jnp.where((kk == cand) & (r0 + sub > cut), kk - 1, kk)
            return carry

        lax.fori_loop(0, n_steps, demote, 0)

    _init_softmax_state(m_ref, l_ref, acc_ref)

    def attn_tile(c, carry):
        k0 = pl.multiple_of(c * TQ, TQ)
        sel = (keys_ref[pl.ds(k0, TQ), :] >= cand) & (k0 + row <= q_pos)
        for hh in range(N_HEADS):
            hs = slice(hh * HEAD_DIM, (hh + 1) * HEAD_DIM)
            s = _dot_nt(ka_ref[0, pl.ds(k0, TQ), hs], qa_ref[0, :, hs])
            _online_softmax_step(jnp.where(sel, s, NEG), vat_ref[0, c, hs, :], m_ref, l_ref, acc_ref, hh)
        return carry

    lax.fori_loop(0, n_tiles, attn_tile, 0)
    _write_heads(o_ref, l_ref, acc_ref)


def _dsa_call(a_mat, kc, idx_out, qa, ka, vat):
    bsz, seq, _ = qa.shape
    topk = min(DSA_TOPK_MAX, seq // 4)
    seq_bits = (seq - 1).bit_length()
    kern = functools.partial(_dsa_kernel, topk=topk, seq_bits=seq_bits)
    return pl.pallas_call(
        kern,
        grid=(bsz, seq // TQ),
        in_specs=[
            pl.BlockSpec((1, TQ, IDX_HEADS * IDX_K), lambda b, j: (b, j, 0)),
            pl.BlockSpec((1, seq, IDX_K), lambda b, j: (b, 0, 0)),
            pl.BlockSpec((1, TQ, LANES), lambda b, j: (b, j, IDX_OUT_W // LANES - 1)),
            pl.BlockSpec((1, TQ, BR_W), lambda b, j: (b, j, 0)),
            pl.BlockSpec((1, seq, BR_W), lambda b, j: (b, 0, 0)),
            pl.BlockSpec((1, seq // TQ, BR_W, TQ), lambda b, j: (b, 0, 0, 0)),
        ],
        out_specs=pl.BlockSpec((1, TQ, BR_W), lambda b, j: (b, j, 0)),
        out_shape=jax.ShapeDtypeStruct((bsz, seq, BR_W), BF16),
        scratch_shapes=[
            pltpu.VMEM((seq, TQ), I32),
            pltpu.VMEM((N_HEADS, 1, TQ), F32),
            pltpu.VMEM((N_HEADS, 1, TQ), F32),
            pltpu.VMEM((N_HEADS, HEAD_DIM, TQ), F32),
        ],
        compiler_params=pltpu.CompilerParams(
            dimension_semantics=("arbitrary", "arbitrary"), vmem_limit_bytes=VMEM_LIMIT),
        name="dsa",
    )(a_mat, kc, idx_out, qa, ka, vat)


def _moba_kernel(qb_ref, kb_ref, vbt_ref, kmean_ref, o_ref, sel_ref, m_ref, l_ref, acc_ref, *, n_sel):
    j = pl.program_id(1)
    nb = kmean_ref.shape[1]
    blk = lax.broadcasted_iota(I32, (nb, TQ), 0).astype(F32)
    jf = j.astype(F32)

    for hh in range(N_HEADS):
        hs = slice(hh * HEAD_DIM, (hh + 1) * HEAD_DIM)
        km_hi, km_lo = _split_bf16(kmean_ref[0, :, hs])
        q = qb_ref[0, :, hs]
        gate = jnp.where(blk < jf, _dot_nt(km_hi, q) + _dot_nt(km_lo, q), NEG)
        chosen = jnp.zeros((nb, TQ), F32)
        for _ in range(n_sel):
            best = jnp.max(gate, axis=0, keepdims=True)
            first = jnp.min(jnp.where(gate == best, blk, float(nb)), axis=0, keepdims=True)
            hit = blk == first
            chosen = jnp.where(hit, 1.0, chosen)
            gate = jnp.where(hit, -jnp.inf, gate)
        sel_ref[hh] = jnp.where(blk < jf, chosen, 0.0)

    _init_softmax_state(m_ref, l_ref, acc_ref)

    def past_tile(n, carry):
        k0 = pl.multiple_of(n * TQ, TQ)
        for hh in range(N_HEADS):
            hs = slice(hh * HEAD_DIM, (hh + 1) * HEAD_DIM)
            s = _dot_nt(kb_ref[0, pl.ds(k0, TQ), hs], qb_ref[0, :, hs])
            s = jnp.where(sel_ref[hh, pl.ds(n, 1), :] > 0.0, s, NEG)
            _online_softmax_step(s, vbt_ref[0, n, hs, :], m_ref, l_ref, acc_ref, hh)
        return carry

    lax.fori_loop(0, j, past_tile, 0)

    k0 = pl.multiple_of(j * TQ, TQ)
    causal = lax.broadcasted_iota(I32, (TQ, TQ), 0) <= lax.broadcasted_iota(I32, (TQ, TQ), 1)
    for hh in range(N_HEADS):
        hs = slice(hh * HEAD_DIM, (hh + 1) * HEAD_DIM)
        s = _dot_nt(kb_ref[0, pl.ds(k0, TQ), hs], qb_ref[0, :, hs])
        _online_softmax_step(jnp.where(causal, s, NEG), vbt_ref[0, j, hs, :], m_ref, l_ref, acc_ref, hh)

    _write_heads(o_ref, l_ref, acc_ref)


def _moba_call(qb, kb, vbt, kmean):
    bsz, seq, _ = qb.shape
    nb = seq // MOBA_BLOCK
    kern = functools.partial(_moba_kernel, n_sel=min(MOBA_TOPK, nb - 1))
    return pl.pallas_call(
        kern,
        grid=(bsz, seq // TQ),
        in_specs=[
            pl.BlockSpec((1, TQ, BR_W), lambda b, j: (b, j, 0)),
            pl.BlockSpec((1, seq, BR_W), lambda b, j: (b, 0, 0)),
            pl.BlockSpec((1, seq // TQ, BR_W, TQ), lambda b, j: (b, 0, 0, 0)),
            pl.BlockSpec((1, nb, BR_W), lambda b, j: (b, 0, 0)),
        ],
        out_specs=pl.BlockSpec((1, TQ, BR_W), lambda b, j: (b, j, 0)),
        out_shape=jax.ShapeDtypeStruct((bsz, seq, BR_W), BF16),
        scratch_shapes=[
            pltpu.VMEM((N_HEADS, nb, TQ), F32),
            pltpu.VMEM((N_HEADS, 1, TQ), F32),
            pltpu.VMEM((N_HEADS, 1, TQ), F32),
            pltpu.VMEM((N_HEADS, HEAD_DIM, TQ), F32),
        ],
        compiler_params=pltpu.CompilerParams(
            dimension_semantics=("arbitrary", "arbitrary"), vmem_limit_bytes=VMEM_LIMIT),
        name="moba",
    )(qb, kb, vbt, kmean)


def _out_ffn_kernel(x_ref, oa_ref, ob_ref, ga_ref, gb_ref, mod_ref, gffn_ref,
                    wba_ref, wbb_ref, wo_ref, wgu_ref, wd_ref, o_ref, act_ref):
    merged = (ga_ref[0].astype(F32) * _dot(oa_ref[0], wba_ref[...])
              + gb_ref[0].astype(F32) * _dot(ob_ref[0], wbb_ref[...]))
    x1 = x_ref[0] + mod_ref[0, 2:3, :] * _dot(merged.astype(BF16), wo_ref[...])
    y = x1 * lax.rsqrt(jnp.mean(x1 * x1, axis=-1, keepdims=True) + EPS) * gffn_ref[...]
    h = (y * (1.0 + mod_ref[0, 4:5, :]) + mod_ref[0, 3:4, :]).astype(BF16)
    for c0 in range(0, D_FF, FF_CHUNK):
        g = _dot(h, wgu_ref[:, c0:c0 + FF_CHUNK])
        u = _dot(h, wgu_ref[:, D_FF + c0:D_FF + c0 + FF_CHUNK])
        act_ref[:, c0:c0 + FF_CHUNK] = (g * jax.nn.sigmoid(g) * u).astype(BF16)
    o_ref[0] = x1 + mod_ref[0, 5:6, :] * _dot(act_ref[...], wd_ref[...])


def _out_ffn_call(x, oa, ob, ga, gb, mod3, gffn, wba, wbb, wo, wgu, wd):
    bsz, seq, d = x.shape
    tm = TM_FFN
    const = lambda b, i: (0, 0)
    tok = lambda b, i: (b, i, 0)
    resident = lambda a: pl.BlockSpec(a.shape, const, pipeline_mode=pl.Buffered(1))
    return pl.pallas_call(
        _out_ffn_kernel,
        grid=(bsz, seq // tm),
        in_specs=[
            pl.BlockSpec((1, tm, d), tok),
            pl.BlockSpec((1, tm, BR_W), tok),
            pl.BlockSpec((1, tm, BR_W), tok),
            pl.BlockSpec((1, tm, d), tok),
            pl.BlockSpec((1, tm, d), tok),
            pl.BlockSpec((1, N_MOD, d), lambda b, i: (b, 0, 0)),
            pl.BlockSpec((1, d), const),
            resident(wba), resident(wbb), resident(wo), resident(wgu), resident(wd),
        ],
        out_specs=pl.BlockSpec((1, tm, d), tok),
        out_shape=jax.ShapeDtypeStruct((bsz, seq, d), F32),
        scratch_shapes=[pltpu.VMEM((tm, D_FF), BF16)],
        compiler_params=pltpu.CompilerParams(
            dimension_semantics=("arbitrary", "arbitrary"), vmem_limit_bytes=VMEM_LIMIT),
        name="out_ffn",
    )(x, oa, ob, ga, gb, mod3, gffn, wba, wbb, wo, wgu, wd)


def _rope_tables(seq, dim):
    inv = ROPE_THETA ** (-jnp.arange(0, dim, 2, dtype=F32) / dim)
    ang = jnp.arange(seq, dtype=F32)[:, None] * inv[None, :]
    cos, sin = jnp.cos(ang), jnp.sin(ang)
    reps = LANES // dim
    return (jnp.tile(jnp.concatenate([cos, cos], axis=-1), (1, reps)),
            jnp.tile(jnp.concatenate([-sin, sin], axis=-1), (1, reps)))


def _layer(x, mod, g_mix, g_ffn, w_in, g_qa, g_ka, g_qb, g_kb, w_br_a, w_br_b, w_out, w_gu, w_down):
    bsz, seq, d = x.shape
    mod3 = mod.reshape(bsz, N_MOD, d)

    o = 0
    cols = {}
    for name, width in (("qa", BR_W), ("ka", BR_W), ("va", BR_W), ("qi", IDX_HEADS * IDX_DIM),
                        ("ki", IDX_DIM), ("wi", IDX_HEADS), ("qb", BR_W), ("kb", BR_W), ("vb", BR_W),
                        ("ga", D_MODEL), ("gb", D_MODEL)):
        cols[name] = w_in[:, o:o + width]
        o += width
    wqk = jnp.concatenate([cols["qa"], cols["ka"], cols["qb"], cols["kb"]], axis=1).astype(BF16)
    wvt = jnp.concatenate([cols["va"], cols["vb"]], axis=1).T.astype(BF16)
    wg = jnp.concatenate([cols["ga"], cols["gb"]], axis=1).astype(BF16)
    pad = IDX_OUT_W - IDX_HEADS * IDX_DIM - IDX_DIM - IDX_HEADS
    widx = jnp.concatenate([cols["qi"], cols["ki"], cols["wi"], jnp.zeros((d, pad), F32)], axis=1)
    wih, wil = _split_bf16(widx)
    gains = jnp.stack([g_qa, g_ka, g_qb, g_kb])
    cosh, sinh = _rope_tables(seq, HEAD_DIM)
    cosi, sini = _rope_tables(seq, IDX_DIM)

    qa, ka, qb, kb, vat, vbt, ga, gb, idx_out, kmean = _proj_call(
        x, mod3, g_mix.reshape(1, d), wqk, wvt, wg, wih, wil, gains, cosh, sinh, cosi, sini)

    qi = idx_out[:, :, :IDX_HEADS * IDX_DIM].reshape(bsz, seq, IDX_HEADS, IDX_DIM)
    qi_hi, qi_lo = _split_bf16(qi)
    a_mat = jnp.concatenate([qi_hi, qi_lo, qi_hi, jnp.zeros_like(qi_hi)], axis=-1)
    a_mat = a_mat.reshape(bsz, seq, IDX_HEADS * IDX_K)
    ki_hi, ki_lo = _split_bf16(idx_out[:, :, IDX_HEADS * IDX_DIM:IDX_HEADS * IDX_DIM + IDX_DIM])
    kc = jnp.concatenate([ki_hi, ki_hi, ki_lo, jnp.zeros_like(ki_hi)], axis=-1)

    oa = _dsa_call(a_mat, kc, idx_out, qa, ka, vat)
    ob = _moba_call(qb, kb, vbt, kmean.reshape(bsz, seq // MOBA_BLOCK, BR_W))

    return _out_ffn_call(x, oa, ob, ga, gb, mod3, g_ffn.reshape(1, d),
                         w_br_a.astype(BF16), w_br_b.astype(BF16), w_out.astype(BF16),
                         w_gu.astype(BF16), w_down.astype(BF16))


def kernel(x, c, w_mod, b_mod, g_mix_norm, g_ffn_norm, w_in, g_q_dsa, g_k_dsa, g_q_moba, g_k_moba,
           w_br_dsa, w_br_moba, w_out, w_gate_up, w_down):
    assert x.shape[1] % TM_PROJ == 0 and x.shape[2] == D_MODEL and w_gate_up.shape[2] == 2 * D_FF
    for l in range(w_mod.shape[0]):
        mod = _mod_call(c, w_mod[l], b_mod[l])
        x = _layer(x, mod, g_mix_norm[l], g_ffn_norm[l], w_in[l], g_q_dsa[l], g_k_dsa[l], g_q_moba[l],
                   g_k_moba[l], w_br_dsa[l], w_br_moba[l], w_out[l], w_gate_up[l], w_down[l])
    return x
```

```python
import functools

import jax
import jax.numpy as jnp
from jax import lax
from jax.experimental import pallas as pl
from jax.experimental.pallas import tpu as pltpu

F32 = jnp.float32
BF16 = jnp.bfloat16
I32 = jnp.int32

D_MODEL = 1024
HEAD_DIM = 128
N_HEADS = 4
BR_W = N_HEADS * HEAD_DIM
IDX_HEADS = 8
IDX_DIM = 64
DSA_TOPK_MAX = 256
MOBA_BLOCK = 256
MOBA_TOPK = 3
D_FF = 2816
ROPE_THETA = 10000.0
EPS = 1e-6
NEG = -1e30
LOG2_E = 1.4426950408889634
M_INIT = -1e29
N_MOD = 6
IDX_OUT_W = 640
IDX_K = 256
INT_MIN = -(2 ** 31)

LANES = 128
SUBLANES = 8
TQ = 256
TM_PROJ = 512
TM_FFN = 512
FF_CHUNK = 256
SEL_ROWS = 64
VMEM_LIMIT = 56 * 1024 * 1024

_NT = (((1,), (1,)), ((), ()))
_HEAD_SLICES = tuple(slice(h * HEAD_DIM, (h + 1) * HEAD_DIM) for h in range(N_HEADS))


def _split_bf16(x):
    hi = x.astype(BF16)
    lo = (x - hi.astype(F32)).astype(BF16)
    return hi, lo


def _dot(a, b):
    return jnp.dot(a, b, preferred_element_type=F32)


def _dot_nt(a, b):
    return lax.dot_general(a, b, _NT, preferred_element_type=F32)


def _mod_kernel(c_ref, w_ref, b_ref, o_ref):
    c = c_ref[...]
    a_hi, a_lo = _split_bf16(c * jax.nn.sigmoid(c))
    w_hi, w_lo = _split_bf16(w_ref[...])
    o_ref[...] = _dot(a_hi, w_hi) + _dot(a_lo, w_hi) + _dot(a_hi, w_lo) + b_ref[...]


def _mod_call(c, w_mod, b_mod):
    bsz, d = c.shape
    n = w_mod.shape[1]
    tn = 1024
    return pl.pallas_call(
        _mod_kernel,
        grid=(n // tn,),
        in_specs=[
            pl.BlockSpec((bsz, d), lambda i: (0, 0)),
            pl.BlockSpec((d, tn), lambda i: (0, i)),
            pl.BlockSpec((1, tn), lambda i: (0, i)),
        ],
        out_specs=pl.BlockSpec((bsz, tn), lambda i: (0, i)),
        out_shape=jax.ShapeDtypeStruct((bsz, n), F32),
        compiler_params=pltpu.CompilerParams(vmem_limit_bytes=VMEM_LIMIT),
        name="mod",
    )(c, w_mod, b_mod.reshape(1, n))


def _rope_partner_64(x):
    lane = lax.broadcasted_iota(I32, x.shape, 1)
    return jnp.where((lane % IDX_DIM) < IDX_DIM // 2,
                     pltpu.roll(x, LANES - IDX_DIM // 2, 1), pltpu.roll(x, IDX_DIM // 2, 1))


def _proj_kernel(x_ref, mod_ref, gmix_ref, wqk_ref, wvt_ref, wg_ref, wih_ref, wil_ref, gains_ref,
                 cosh_ref, sinh_ref, cosi_ref, sini_ref,
                 qa_ref, ka_ref, qb_ref, kb_ref, vat_ref, vbt_ref, ga_ref, gb_ref, idx_ref, kmean_ref):
    x = x_ref[0]
    y = x * lax.rsqrt(jnp.mean(x * x, axis=-1, keepdims=True) + EPS) * gmix_ref[...]
    h = y * (1.0 + mod_ref[0, 1:2, :]) + mod_ref[0, 0:1, :]
    h_hi, h_lo = _split_bf16(h)
    tm = x.shape[0]
    cosh, sinh = cosh_ref[...], sinh_ref[...]
    scale = HEAD_DIM ** -0.5 * LOG2_E

    for gi, out_ref in enumerate((qa_ref, ka_ref, qb_ref, kb_ref)):
        p = _dot(h_hi, wqk_ref[:, gi * BR_W:(gi + 1) * BR_W])
        for hh in range(N_HEADS):
            ph = p[:, hh * HEAD_DIM:(hh + 1) * HEAD_DIM]
            yh = ph * lax.rsqrt(jnp.mean(ph * ph, axis=-1, keepdims=True) + EPS) * gains_ref[gi:gi + 1, :]
            r = yh * cosh + pltpu.roll(yh, HEAD_DIM // 2, 1) * sinh
            if gi == 3:
                for blk in range(tm // MOBA_BLOCK):
                    kmean_ref[0, blk, :, hh * HEAD_DIM:(hh + 1) * HEAD_DIM] = jnp.mean(
                        r[blk * MOBA_BLOCK:(blk + 1) * MOBA_BLOCK], axis=0, keepdims=True)
            if gi % 2 == 0:
                r = r * scale
            out_ref[0, :, hh * HEAD_DIM:(hh + 1) * HEAD_DIM] = r.astype(BF16)

    vt = _dot_nt(wvt_ref[...], h_hi)
    for blk in range(tm // TQ):
        vat_ref[0, blk] = vt[:BR_W, blk * TQ:(blk + 1) * TQ].astype(BF16)
        vbt_ref[0, blk] = vt[BR_W:, blk * TQ:(blk + 1) * TQ].astype(BF16)

    for gi, out_ref in enumerate((ga_ref, gb_ref)):
        g = _dot(h_hi, wg_ref[:, gi * D_MODEL:(gi + 1) * D_MODEL])
        out_ref[0] = jax.nn.sigmoid(g).astype(BF16)

    pi = _dot(h_hi, wih_ref[...]) + _dot(h_lo, wih_ref[...]) + _dot(h_hi, wil_ref[...])
    cosi, sini = cosi_ref[...], sini_ref[...]
    n_q = IDX_HEADS * IDX_DIM // LANES
    for g4 in range(n_q):
        xg = pi[:, g4 * LANES:(g4 + 1) * LANES]
        idx_ref[0, :, g4 * LANES:(g4 + 1) * LANES] = xg * cosi + _rope_partner_64(xg) * sini
    xg = pi[:, n_q * LANES:(n_q + 1) * LANES]
    lane = lax.broadcasted_iota(I32, xg.shape, 1)
    w_scale = (IDX_HEADS ** -0.5) * (IDX_DIM ** -0.5)
    roped = xg * cosi + _rope_partner_64(xg) * sini
    idx_ref[0, :, n_q * LANES:(n_q + 1) * LANES] = jnp.where(
        lane < IDX_DIM, roped, jnp.where(lane < IDX_DIM + IDX_HEADS, xg * w_scale, 0.0))


def _proj_call(x, mod3, gmix, wqk, wvt, wg, wih, wil, gains, cosh, sinh, cosi, sini):
    bsz, seq, d = x.shape
    tm = TM_PROJ
    nb = seq // MOBA_BLOCK
    const = lambda b, i: (0, 0)
    tok = lambda b, i: (b, i, 0)
    tab = lambda b, i: (i, 0)
    out_shape = (
        [jax.ShapeDtypeStruct((bsz, seq, BR_W), BF16)] * 4
        + [jax.ShapeDtypeStruct((bsz, seq // TQ, BR_W, TQ), BF16)] * 2
        + [jax.ShapeDtypeStruct((bsz, seq, D_MODEL), BF16)] * 2
        + [jax.ShapeDtypeStruct((bsz, seq, IDX_OUT_W), F32),
           jax.ShapeDtypeStruct((bsz, nb, 1, BR_W), F32)]
    )
    out_specs = (
        [pl.BlockSpec((1, tm, BR_W), tok)] * 4
        + [pl.BlockSpec((1, tm // TQ, BR_W, TQ), lambda b, i: (b, i, 0, 0))] * 2
        + [pl.BlockSpec((1, tm, D_MODEL), tok)] * 2
        + [pl.BlockSpec((1, tm, IDX_OUT_W), tok),
           pl.BlockSpec((1, tm // MOBA_BLOCK, 1, BR_W), lambda b, i: (b, i, 0, 0))]
    )
    return pl.pallas_call(
        _proj_kernel,
        grid=(bsz, seq // tm),
        in_specs=[
            pl.BlockSpec((1, tm, d), tok),
            pl.BlockSpec((1, N_MOD, d), lambda b, i: (b, 0, 0)),
            pl.BlockSpec((1, d), const),
            pl.BlockSpec(wqk.shape, const),
            pl.BlockSpec(wvt.shape, const),
            pl.BlockSpec(wg.shape, const),
            pl.BlockSpec(wih.shape, const),
            pl.BlockSpec(wil.shape, const),
            pl.BlockSpec(gains.shape, const),
            pl.BlockSpec((tm, LANES), tab),
            pl.BlockSpec((tm, LANES), tab),
            pl.BlockSpec((tm, LANES), tab),
            pl.BlockSpec((tm, LANES), tab),
        ],
        out_specs=out_specs,
        out_shape=out_shape,
        compiler_params=pltpu.CompilerParams(
            dimension_semantics=("arbitrary", "arbitrary"), vmem_limit_bytes=VMEM_LIMIT),
        name="proj",
    )(x, mod3, gmix, wqk, wvt, wg, wih, wil, gains, cosh, sinh, cosi, sini)


def _tile3(x):
    return x.reshape(x.shape[0] // SUBLANES, SUBLANES, x.shape[1])


def _allreduce_sublanes(x, op):
    for shift in (4, 2, 1):
        x = op(x, pltpu.roll(x, shift, 0))
    return x


def _softmax_tile(s, vt, state, acc_ref, hs):
    m_old, l_old = state
    s3 = _tile3(s)
    m_new = jnp.maximum(m_old, _allreduce_sublanes(jnp.max(s3, axis=0), jnp.maximum))
    alpha = jnp.exp2(m_old - m_new)
    p3 = jnp.exp2(s3 - m_new[None])
    pv = _dot(vt, p3.reshape(s.shape).astype(BF16))
    acc_ref[hs, :] = (_tile3(acc_ref[hs, :]) * alpha[None]).reshape(HEAD_DIM, TQ) + pv
    return m_new, alpha * l_old + jnp.sum(p3, axis=0)


def _init_state(acc_ref):
    acc_ref[...] = jnp.zeros(acc_ref.shape, F32)
    one = (jnp.full((SUBLANES, TQ), M_INIT, F32), jnp.zeros((SUBLANES, TQ), F32))
    return (one,) * N_HEADS


def _write_heads(o_ref, state, acc_ref):
    for hh in range(N_HEADS):
        hs = slice(hh * HEAD_DIM, (hh + 1) * HEAD_DIM)
        inv = 1.0 / _allreduce_sublanes(state[hh][1], jnp.add)
        o_t = (_tile3(acc_ref[hs, :]) * inv[None]).reshape(HEAD_DIM, TQ)
        o_ref[0, :, hs] = o_t.T.astype(BF16)


def _sortable_key(score):
    bits = lax.bitcast_convert_type(score, I32)
    key = bits ^ ((bits >> 31) & jnp.int32(0x7FFFFFFF))
    return jnp.where(key == -1, 0, key)


def _dsa_kernel(a_ref, kc_ref, wk_ref, qa_ref, ka_ref, vat_ref, o_ref, keys_ref, acc_ref, *, topk, seq_bits):
    j = pl.program_id(1)
    n_tiles = j + 1
    w_rows = wk_ref[0].T[IDX_DIM:IDX_DIM + IDX_HEADS, :]
    w8 = [jnp.broadcast_to(w_rows[hh:hh + 1, :], (SUBLANES, TQ)) for hh in range(IDX_HEADS)]
    causal = lax.broadcasted_iota(I32, (TQ, TQ), 0) <= lax.broadcasted_iota(I32, (TQ, TQ), 1)

    def score_tile(c, diagonal):
        k0 = pl.multiple_of(c * TQ, TQ)
        kc = kc_ref[0, pl.ds(k0, TQ), :]
        acc = jnp.zeros((TQ // SUBLANES, SUBLANES, TQ), F32)
        for hh in range(IDX_HEADS):
            lg = _dot_nt(kc, a_ref[0, :, hh * IDX_K:(hh + 1) * IDX_K])
            acc = acc + w8[hh][None] * jnp.maximum(_tile3(lg), 0.0)
        acc = acc.reshape(TQ, TQ)
        if diagonal:
            acc = jnp.where(causal, acc, NEG)
        keys_ref[pl.ds(k0, TQ), :] = _sortable_key(acc)

    def score_body(c, carry):
        score_tile(c, False)
        return carry

    lax.fori_loop(0, j, score_body, 0)
    score_tile(j, True)

    def count(pred):
        def body(c, cnt):
            for u in range(TQ // SEL_ROWS):
                r0 = pl.multiple_of(c * TQ + u * SEL_ROWS, SEL_ROWS)
                cnt = cnt + jnp.where(pred(keys_ref[pl.ds(r0, SEL_ROWS), :], r0), 1.0, 0.0)
            return cnt
        cnt = lax.fori_loop(0, n_tiles, body, jnp.zeros((SEL_ROWS, TQ), F32))
        return jnp.sum(cnt, axis=0, keepdims=True)

    kf = float(topk)
    c0 = count(lambda kk, r0: kk >= 0)
    cand = jnp.where(c0 >= kf, 0, INT_MIN).astype(I32)
    cge = jnp.where(c0 >= kf, c0, (n_tiles * TQ).astype(F32))

    def bit_step(i, carry):
        cand, cge = carry
        test = cand | (jnp.int32(1) << (30 - i))
        c = count(lambda kk, r0: kk >= test)
        ok = c >= kf
        return jnp.where(ok, test, cand), jnp.where(ok, c, cge)

    cand, cge = lax.fori_loop(0, 31, bit_step, (cand, cge))

    @pl.when(jnp.max(jnp.abs(cge - kf)) > 0.0)
    def _():
        need = kf - count(lambda kk, r0: kk > cand)
        sub = lax.broadcasted_iota(I32, (SEL_ROWS, TQ), 0)

        def cut_step(i, cut):
            test = cut | (jnp.int32(1) << (seq_bits - 1 - i))
            f = count(lambda kk, r0: (kk == cand) & (r0 + sub < test))
            return jnp.where(f < need, test, cut)

        cut = lax.fori_loop(0, seq_bits, cut_step, jnp.zeros((1, TQ), I32))

        def demote(c, carry):
            r0 = pl.multiple_of(c * SEL_ROWS, SEL_ROWS)
            kk = keys_ref[pl.ds(r0, SEL_ROWS), :]
            keys_ref[pl.ds(r0, SEL_ROWS), :] = jnp.where((kk == cand) & (r0 + sub > cut), kk - 1, kk)
            return carry

        lax.fori_loop(0, n_tiles * (TQ // SEL_ROWS), demote, 0)

    def attn_tile(c, state, diagonal):
        k0 = pl.multiple_of(c * TQ, TQ)
        sel = keys_ref[pl.ds(k0, TQ), :] >= cand
        if diagonal:
            sel = sel & causal
        s_all = [_dot_nt(ka_ref[0, pl.ds(k0, TQ), hs], qa_ref[0, :, hs]) for hs in _HEAD_SLICES]
        return tuple(
            _softmax_tile(jnp.where(sel, s_all[hh], NEG), vat_ref[0, c, hs, :], state[hh], acc_ref, hs)
            for hh, hs in enumerate(_HEAD_SLICES))

    state = lax.fori_loop(0, j, lambda c, st: attn_tile(c, st, False), _init_state(acc_ref))
    state = attn_tile(j, state, True)
    _write_heads(o_ref, state, acc_ref)


def _dsa_call(a_mat, kc, idx_out, qa, ka, vat):
    bsz, seq, _ = qa.shape
    topk = min(DSA_TOPK_MAX, seq // 4)
    seq_bits = (seq - 1).bit_length()
    kern = functools.partial(_dsa_kernel, topk=topk, seq_bits=seq_bits)
    return pl.pallas_call(
        kern,
        grid=(bsz, seq // TQ),
        in_specs=[
            pl.BlockSpec((1, TQ, IDX_HEADS * IDX_K), lambda b, j: (b, j, 0)),
            pl.BlockSpec((1, seq, IDX_K), lambda b, j: (b, 0, 0)),
            pl.BlockSpec((1, TQ, LANES), lambda b, j: (b, j, IDX_OUT_W // LANES - 1)),
            pl.BlockSpec((1, TQ, BR_W), lambda b, j: (b, j, 0)),
            pl.BlockSpec((1, seq, BR_W), lambda b, j: (b, 0, 0)),
            pl.BlockSpec((1, seq // TQ, BR_W, TQ), lambda b, j: (b, 0, 0, 0)),
        ],
        out_specs=pl.BlockSpec((1, TQ, BR_W), lambda b, j: (b, j, 0)),
        out_shape=jax.ShapeDtypeStruct((bsz, seq, BR_W), BF16),
        scratch_shapes=[
            pltpu.VMEM((seq, TQ), I32),
            pltpu.VMEM((BR_W, TQ), F32),
        ],
        compiler_params=pltpu.CompilerParams(
            dimension_semantics=("arbitrary", "arbitrary"), vmem_limit_bytes=VMEM_LIMIT),
        name="dsa",
    )(a_mat, kc, idx_out, qa, ka, vat)


def _moba_kernel(qb_ref, kb_ref, vbt_ref, kmean_ref, o_ref, bias_ref, acc_ref, *, n_sel):
    j = pl.program_id(1)
    nb = kmean_ref.shape[1]
    blk = lax.broadcasted_iota(I32, (nb, TQ), 0).astype(F32)
    jf = j.astype(F32)

    for hh in range(N_HEADS):
        hs = slice(hh * HEAD_DIM, (hh + 1) * HEAD_DIM)
        km_hi, km_lo = _split_bf16(kmean_ref[0, :, hs])
        q = qb_ref[0, :, hs]
        gate = jnp.where(blk < jf, _dot_nt(km_hi, q) + _dot_nt(km_lo, q), NEG)
        chosen = jnp.zeros((nb, TQ), F32)
        for _ in range(n_sel):
            best = jnp.max(gate, axis=0, keepdims=True)
            first = jnp.min(jnp.where(gate == best, blk, float(nb)), axis=0, keepdims=True)
            hit = blk == first
            chosen = jnp.where(hit, 1.0, chosen)
            gate = jnp.where(hit, -jnp.inf, gate)
        bias = jnp.where((chosen > 0.0) & (blk < jf), 0.0, NEG)
        for n in range(nb):
            bias_ref[hh, n] = jnp.broadcast_to(bias[n:n + 1, :], (SUBLANES, TQ))

    def past_tile(n, state):
        k0 = pl.multiple_of(n * TQ, TQ)
        s_all = [_dot_nt(kb_ref[0, pl.ds(k0, TQ), hs], qb_ref[0, :, hs]) for hs in _HEAD_SLICES]
        new = []
        for hh, hs in enumerate(_HEAD_SLICES):
            s = (_tile3(s_all[hh]) + bias_ref[hh, n][None]).reshape(TQ, TQ)
            new.append(_softmax_tile(s, vbt_ref[0, n, hs, :], state[hh], acc_ref, hs))
        return tuple(new)

    state = lax.fori_loop(0, j, past_tile, _init_state(acc_ref))

    k0 = pl.multiple_of(j * TQ, TQ)
    causal = lax.broadcasted_iota(I32, (TQ, TQ), 0) <= lax.broadcasted_iota(I32, (TQ, TQ), 1)
    s_all = [_dot_nt(kb_ref[0, pl.ds(k0, TQ), hs], qb_ref[0, :, hs]) for hs in _HEAD_SLICES]
    state = tuple(
        _softmax_tile(jnp.where(causal, s_all[hh], NEG), vbt_ref[0, j, hs, :], state[hh], acc_ref, hs)
        for hh, hs in enumerate(_HEAD_SLICES))
    _write_heads(o_ref, state, acc_ref)


def _moba_call(qb, kb, vbt, kmean):
    bsz, seq, _ = qb.shape
    nb = seq // MOBA_BLOCK
    kern = functools.partial(_moba_kernel, n_sel=min(MOBA_TOPK, nb - 1))
    return pl.pallas_call(
        kern,
        grid=(bsz, seq // TQ),
        in_specs=[
            pl.BlockSpec((1, TQ, BR_W), lambda b, j: (b, j, 0)),
            pl.BlockSpec((1, seq, BR_W), lambda b, j: (b, 0, 0)),
            pl.BlockSpec((1, seq // TQ, BR_W, TQ), lambda b, j: (b, 0, 0, 0)),
            pl.BlockSpec((1, nb, BR_W), lambda b, j: (b, 0, 0)),
        ],
        out_specs=pl.BlockSpec((1, TQ, BR_W), lambda b, j: (b, j, 0)),
        out_shape=jax.ShapeDtypeStruct((bsz, seq, BR_W), BF16),
        scratch_shapes=[
            pltpu.VMEM((N_HEADS, nb, SUBLANES, TQ), F32),
            pltpu.VMEM((BR_W, TQ), F32),
        ],
        compiler_params=pltpu.CompilerParams(
            dimension_semantics=("arbitrary", "arbitrary"), vmem_limit_bytes=VMEM_LIMIT),
        name="moba",
    )(qb, kb, vbt, kmean)


def _out_ffn_kernel(x_ref, oa_ref, ob_ref, ga_ref, gb_ref, mod_ref, gffn_ref,
                    wba_ref, wbb_ref, wo_ref, wgu_ref, wd_ref, o_ref, act_ref):
    merged = (ga_ref[0].astype(F32) * _dot(oa_ref[0], wba_ref[...])
              + gb_ref[0].astype(F32) * _dot(ob_ref[0], wbb_ref[...]))
    x1 = x_ref[0] + mod_ref[0, 2:3, :] * _dot(merged.astype(BF16), wo_ref[...])
    y = x1 * lax.rsqrt(jnp.mean(x1 * x1, axis=-1, keepdims=True) + EPS) * gffn_ref[...]
    h = (y * (1.0 + mod_ref[0, 4:5, :]) + mod_ref[0, 3:4, :]).astype(BF16)
    for c0 in range(0, D_FF, FF_CHUNK):
        g = _dot(h, wgu_ref[:, c0:c0 + FF_CHUNK])
        u = _dot(h, wgu_ref[:, D_FF + c0:D_FF + c0 + FF_CHUNK])
        act_ref[:, c0:c0 + FF_CHUNK] = (g * jax.nn.sigmoid(g) * u).astype(BF16)
    o_ref[0] = x1 + mod_ref[0, 5:6, :] * _dot(act_ref[...], wd_ref[...])


def _out_ffn_call(x, oa, ob, ga, gb, mod3, gffn, wba, wbb, wo, wgu, wd):
    bsz, seq, d = x.shape
    tm = TM_FFN
    const = lambda b, i: (0, 0)
    tok = lambda b, i: (b, i, 0)
    resident = lambda a: pl.BlockSpec(a.shape, const, pipeline_mode=pl.Buffered(1))
    return pl.pallas_call(
        _out_ffn_kernel,
        grid=(bsz, seq // tm),
        in_specs=[
            pl.BlockSpec((1, tm, d), tok),
            pl.BlockSpec((1, tm, BR_W), tok),
            pl.BlockSpec((1, tm, BR_W), tok),
            pl.BlockSpec((1, tm, d), tok),
            pl.BlockSpec((1, tm, d), tok),
            pl.BlockSpec((1, N_MOD, d), lambda b, i: (b, 0, 0)),
            pl.BlockSpec((1, d), const),
            resident(wba), resident(wbb), resident(wo), resident(wgu), resident(wd),
        ],
        out_specs=pl.BlockSpec((1, tm, d), tok),
        out_shape=jax.ShapeDtypeStruct((bsz, seq, d), F32),
        scratch_shapes=[pltpu.VMEM((tm, D_FF), BF16)],
        compiler_params=pltpu.CompilerParams(
            dimension_semantics=("arbitrary", "arbitrary"), vmem_limit_bytes=VMEM_LIMIT),
        name="out_ffn",
    )(x, oa, ob, ga, gb, mod3, gffn, wba, wbb, wo, wgu, wd)


def _rope_tables(seq, dim):
    inv = ROPE_THETA ** (-jnp.arange(0, dim, 2, dtype=F32) / dim)
    ang = jnp.arange(seq, dtype=F32)[:, None] * inv[None, :]
    cos, sin = jnp.cos(ang), jnp.sin(ang)
    reps = LANES // dim
    return (jnp.tile(jnp.concatenate([cos, cos], axis=-1), (1, reps)),
            jnp.tile(jnp.concatenate([-sin, sin], axis=-1), (1, reps)))


def _layer(x, mod, g_mix, g_ffn, w_in, g_qa, g_ka, g_qb, g_kb, w_br_a, w_br_b, w_out, w_gu, w_down):
    bsz, seq, d = x.shape
    mod3 = mod.reshape(bsz, N_MOD, d)

    o = 0
    cols = {}
    for name, width in (("qa", BR_W), ("ka", BR_W), ("va", BR_W), ("qi", IDX_HEADS * IDX_DIM),
                        ("ki", IDX_DIM), ("wi", IDX_HEADS), ("qb", BR_W), ("kb", BR_W), ("vb", BR_W),
                        ("ga", D_MODEL), ("gb", D_MODEL)):
        cols[name] = w_in[:, o:o + width]
        o += width
    wqk = jnp.concatenate([cols["qa"], cols["ka"], cols["qb"], cols["kb"]], axis=1).astype(BF16)
    wvt = jnp.concatenate([cols["va"], cols["vb"]], axis=1).T.astype(BF16)
    wg = jnp.concatenate([cols["ga"], cols["gb"]], axis=1).astype(BF16)
    pad = IDX_OUT_W - IDX_HEADS * IDX_DIM - IDX_DIM - IDX_HEADS
    widx = jnp.concatenate([cols["qi"], cols["ki"], cols["wi"], jnp.zeros((d, pad), F32)], axis=1)
    wih, wil = _split_bf16(widx)
    gains = jnp.stack([g_qa, g_ka, g_qb, g_kb])
    cosh, sinh = _rope_tables(seq, HEAD_DIM)
    cosi, sini = _rope_tables(seq, IDX_DIM)

    qa, ka, qb, kb, vat, vbt, ga, gb, idx_out, kmean = _proj_call(
        x, mod3, g_mix.reshape(1, d), wqk, wvt, wg, wih, wil, gains, cosh, sinh, cosi, sini)

    qi = idx_out[:, :, :IDX_HEADS * IDX_DIM].reshape(bsz, seq, IDX_HEADS, IDX_DIM)
    qi_hi, qi_lo = _split_bf16(qi)
    a_mat = jnp.concatenate([qi_hi, qi_lo, qi_hi, jnp.zeros_like(qi_hi)], axis=-1)
    a_mat = a_mat.reshape(bsz, seq, IDX_HEADS * IDX_K)
    ki_hi, ki_lo = _split_bf16(idx_out[:, :, IDX_HEADS * IDX_DIM:IDX_HEADS * IDX_DIM + IDX_DIM])
    kc = jnp.concatenate([ki_hi, ki_hi, ki_lo, jnp.zeros_like(ki_hi)], axis=-1)

    oa = _dsa_call(a_mat, kc, idx_out, qa, ka, vat)
    ob = _moba_call(qb, kb, vbt, kmean.reshape(bsz, seq // MOBA_BLOCK, BR_W))

    return _out_ffn_call(x, oa, ob, ga, gb, mod3, g_ffn.reshape(1, d),
                         w_br_a.astype(BF16), w_br_b.astype(BF16), w_out.astype(BF16),
                         w_gu.astype(BF16), w_down.astype(BF16))


def kernel(x, c, w_mod, b_mod, g_mix_norm, g_ffn_norm, w_in, g_q_dsa, g_k_dsa, g_q_moba, g_k_moba,
           w_br_dsa, w_br_moba, w_out, w_gate_up, w_down):
    assert x.shape[1] % TM_PROJ == 0 and x.shape[2] == D_MODEL and w_gate_up.shape[2] == 2 * D_FF
    for l in range(w_mod.shape[0]):
        mod = _mod_call(c, w_mod[l], b_mod[l])
        x = _layer(x, mod, g_mix_norm[l], g_ffn_norm[l], w_in[l], g_q_dsa[l], g_k_dsa[l], g_q_moba[l],
                   g_k_moba[l], w_br_dsa[l], w_br_moba[l], w_out[l], w_gate_up[l], w_down[l])
    return x
```

```python
import functools

import jax
import jax.numpy as jnp
from jax import lax
from jax.experimental import pallas as pl
from jax.experimental.pallas import tpu as pltpu

F32 = jnp.float32
BF16 = jnp.bfloat16
I32 = jnp.int32
I16 = jnp.int16

D_MODEL = 1024
HEAD_DIM = 128
N_HEADS = 4
BR_W = N_HEADS * HEAD_DIM
IDX_HEADS = 8
IDX_DIM = 64
DSA_TOPK_MAX = 256
MOBA_BLOCK = 256
MOBA_TOPK = 3
D_FF = 2816
ROPE_THETA = 10000.0
EPS = 1e-6
NEG = -1e30
LOG2_E = 1.4426950408889634
M_INIT = -1e29
N_MOD = 6
IDX_OUT_W = 640
IDX_K = 256
INT_MIN = -(2 ** 31)
I16_MIN = -(2 ** 15)

LANES = 128
SUBLANES = 8
PACK_ROWS = 16
TQ = 256
TM_PROJ = 512
TM_FFN = 512
FF_CHUNK = 256
SEL_ROWS = 64
VMEM_LIMIT = 56 * 1024 * 1024

_NT = (((1,), (1,)), ((), ()))
_HEAD_SLICES = tuple(slice(h * HEAD_DIM, (h + 1) * HEAD_DIM) for h in range(N_HEADS))


def _split_bf16(x):
    hi = x.astype(BF16)
    lo = (x - hi.astype(F32)).astype(BF16)
    return hi, lo


def _dot(a, b):
    return jnp.dot(a, b, preferred_element_type=F32)


def _dot_nt(a, b):
    return lax.dot_general(a, b, _NT, preferred_element_type=F32)


def _mod_kernel(c_ref, w_ref, b_ref, o_ref):
    c = c_ref[...]
    a_hi, a_lo = _split_bf16(c * jax.nn.sigmoid(c))
    w_hi, w_lo = _split_bf16(w_ref[...])
    o_ref[...] = _dot(a_hi, w_hi) + _dot(a_lo, w_hi) + _dot(a_hi, w_lo) + b_ref[...]


def _mod_call(c, w_mod, b_mod):
    bsz, d = c.shape
    n = w_mod.shape[1]
    tn = 1024
    return pl.pallas_call(
        _mod_kernel,
        grid=(n // tn,),
        in_specs=[
            pl.BlockSpec((bsz, d), lambda i: (0, 0)),
            pl.BlockSpec((d, tn), lambda i: (0, i)),
            pl.BlockSpec((1, tn), lambda i: (0, i)),
        ],
        out_specs=pl.BlockSpec((bsz, tn), lambda i: (0, i)),
        out_shape=jax.ShapeDtypeStruct((bsz, n), F32),
        compiler_params=pltpu.CompilerParams(vmem_limit_bytes=VMEM_LIMIT),
        name="mod",
    )(c, w_mod, b_mod.reshape(1, n))


def _rope_partner_64(x):
    lane = lax.broadcasted_iota(I32, x.shape, 1)
    return jnp.where((lane % IDX_DIM) < IDX_DIM // 2,
                     pltpu.roll(x, LANES - IDX_DIM // 2, 1), pltpu.roll(x, IDX_DIM // 2, 1))


def _proj_kernel(x_ref, mod_ref, gmix_ref, wqk_ref, wvt_ref, wg_ref, wih_ref, wil_ref, gains_ref,
                 cosh_ref, sinh_ref, cosi_ref, sini_ref,
                 qa_ref, ka_ref, qb_ref, kb_ref, vat_ref, vbt_ref, ga_ref, gb_ref, a_ref, kc_ref, kw_ref, kmean_ref):
    x = x_ref[0]
    y = x * lax.rsqrt(jnp.mean(x * x, axis=-1, keepdims=True) + EPS) * gmix_ref[...]
    h = y * (1.0 + mod_ref[0, 1:2, :]) + mod_ref[0, 0:1, :]
    h_hi, h_lo = _split_bf16(h)
    tm = x.shape[0]
    cosh, sinh = cosh_ref[...], sinh_ref[...]
    scale = HEAD_DIM ** -0.5 * LOG2_E

    for gi, out_ref in enumerate((qa_ref, ka_ref, qb_ref, kb_ref)):
        p = _dot(h_hi, wqk_ref[:, gi * BR_W:(gi + 1) * BR_W])
        for hh in range(N_HEADS):
            ph = p[:, hh * HEAD_DIM:(hh + 1) * HEAD_DIM]
            yh = ph * lax.rsqrt(jnp.mean(ph * ph, axis=-1, keepdims=True) + EPS) * gains_ref[gi:gi + 1, :]
            r = yh * cosh + pltpu.roll(yh, HEAD_DIM // 2, 1) * sinh
            if gi == 3:
                for blk in range(tm // MOBA_BLOCK):
                    kmean_ref[0, blk, :, hh * HEAD_DIM:(hh + 1) * HEAD_DIM] = jnp.mean(
                        r[blk * MOBA_BLOCK:(blk + 1) * MOBA_BLOCK], axis=0, keepdims=True)
            if gi % 2 == 0:
                r = r * scale
            out_ref[0, :, hh * HEAD_DIM:(hh + 1) * HEAD_DIM] = r.astype(BF16)

    vt = _dot_nt(wvt_ref[...], h_hi)
    for blk in range(tm // TQ):
        vat_ref[0, blk] = vt[:BR_W, blk * TQ:(blk + 1) * TQ].astype(BF16)
        vbt_ref[0, blk] = vt[BR_W:, blk * TQ:(blk + 1) * TQ].astype(BF16)

    for gi, out_ref in enumerate((ga_ref, gb_ref)):
        g = _dot(h_hi, wg_ref[:, gi * D_MODEL:(gi + 1) * D_MODEL])
        out_ref[0] = jax.nn.sigmoid(g).astype(BF16)

    pi = _dot(h_hi, wih_ref[...]) + _dot(h_lo, wih_ref[...]) + _dot(h_hi, wil_ref[...])
    cosi, sini = cosi_ref[...], sini_ref[...]
    first = lax.broadcasted_iota(I32, (tm, LANES), 1) < IDX_DIM
    half = LANES // 2

    def hi_lo(v):
        hi = v.astype(BF16).astype(F32)
        return hi, v - hi

    n_q = IDX_HEADS * IDX_DIM // LANES
    for g4 in range(n_q):
        xg = pi[:, g4 * LANES:(g4 + 1) * LANES]
        hi, lo = hi_lo(xg * cosi + _rope_partner_64(xg) * sini)
        hi_r, lo_r = pltpu.roll(hi, half, 1), pltpu.roll(lo, half, 1)
        for head, parts in ((2 * g4, (jnp.where(first, hi, lo_r), jnp.where(first, hi, 0.0))),
                            (2 * g4 + 1, (jnp.where(first, hi_r, lo), jnp.where(first, hi_r, 0.0)))):
            for pi_, part in enumerate(parts):
                a_ref[0, :, head * IDX_K + pi_ * LANES:head * IDX_K + (pi_ + 1) * LANES] = part.astype(BF16)
    xg = pi[:, n_q * LANES:(n_q + 1) * LANES]
    roped = xg * cosi + _rope_partner_64(xg) * sini
    hi, lo = hi_lo(roped)
    kc_ref[0, :, :LANES] = jnp.where(first, hi, pltpu.roll(hi, half, 1)).astype(BF16)
    kc_ref[0, :, LANES:] = jnp.where(first, lo, 0.0).astype(BF16)
    w_scale = (IDX_HEADS ** -0.5) * (IDX_DIM ** -0.5)
    kw_ref[0] = jnp.where(first, roped, xg * w_scale)


def _proj_call(x, mod3, gmix, wqk, wvt, wg, wih, wil, gains, cosh, sinh, cosi, sini):
    bsz, seq, d = x.shape
    tm = TM_PROJ
    nb = seq // MOBA_BLOCK
    const = lambda b, i: (0, 0)
    tok = lambda b, i: (b, i, 0)
    tab = lambda b, i: (i, 0)
    out_shape = (
        [jax.ShapeDtypeStruct((bsz, seq, BR_W), BF16)] * 4
        + [jax.ShapeDtypeStruct((bsz, seq // TQ, BR_W, TQ), BF16)] * 2
        + [jax.ShapeDtypeStruct((bsz, seq, D_MODEL), BF16)] * 2
        + [jax.ShapeDtypeStruct((bsz, seq, IDX_HEADS * IDX_K), BF16),
           jax.ShapeDtypeStruct((bsz, seq, IDX_K), BF16),
           jax.ShapeDtypeStruct((bsz, seq, LANES), F32),
           jax.ShapeDtypeStruct((bsz, nb, 1, BR_W), F32)]
    )
    out_specs = (
        [pl.BlockSpec((1, tm, BR_W), tok)] * 4
        + [pl.BlockSpec((1, tm // TQ, BR_W, TQ), lambda b, i: (b, i, 0, 0))] * 2
        + [pl.BlockSpec((1, tm, D_MODEL), tok)] * 2
        + [pl.BlockSpec((1, tm, IDX_HEADS * IDX_K), tok),
           pl.BlockSpec((1, tm, IDX_K), tok),
           pl.BlockSpec((1, tm, LANES), tok),
           pl.BlockSpec((1, tm // MOBA_BLOCK, 1, BR_W), lambda b, i: (b, i, 0, 0))]
    )
    return pl.pallas_call(
        _proj_kernel,
        grid=(bsz, seq // tm),
        in_specs=[
            pl.BlockSpec((1, tm, d), tok),
            pl.BlockSpec((1, N_MOD, d), lambda b, i: (b, 0, 0)),
            pl.BlockSpec((1, d), const),
            pl.BlockSpec(wqk.shape, const),
            pl.BlockSpec(wvt.shape, const),
            pl.BlockSpec(wg.shape, const),
            pl.BlockSpec(wih.shape, const),
            pl.BlockSpec(wil.shape, const),
            pl.BlockSpec(gains.shape, const),
            pl.BlockSpec((tm, LANES), tab),
            pl.BlockSpec((tm, LANES), tab),
            pl.BlockSpec((tm, LANES), tab),
            pl.BlockSpec((tm, LANES), tab),
        ],
        out_specs=out_specs,
        out_shape=out_shape,
        compiler_params=pltpu.CompilerParams(
            dimension_semantics=("arbitrary", "arbitrary"), vmem_limit_bytes=VMEM_LIMIT),
        name="proj",
    )(x, mod3, gmix, wqk, wvt, wg, wih, wil, gains, cosh, sinh, cosi, sini)


def _tile3(x):
    return x.reshape(x.shape[0] // SUBLANES, SUBLANES, x.shape[1])


def _allreduce_sublanes(x, op):
    for shift in (4, 2, 1):
        x = op(x, pltpu.roll(x, shift, 0))
    return x


def _softmax_tile(s, vt, state, acc_ref, hs):
    m_old, l_old = state
    s3 = _tile3(s)
    m_new = jnp.maximum(m_old, _allreduce_sublanes(jnp.max(s3, axis=0), jnp.maximum))
    alpha = jnp.exp2(m_old - m_new)
    p3 = jnp.exp2(s3 - m_new[None])
    pv = _dot(vt, p3.reshape(s.shape).astype(BF16))
    acc_ref[hs, :] = (_tile3(acc_ref[hs, :]) * alpha[None]).reshape(HEAD_DIM, TQ) + pv
    return m_new, alpha * l_old + jnp.sum(p3, axis=0)


def _init_state(acc_ref):
    acc_ref[...] = jnp.zeros(acc_ref.shape, F32)
    one = (jnp.full((SUBLANES, TQ), M_INIT, F32), jnp.zeros((SUBLANES, TQ), F32))
    return (one,) * N_HEADS


def _write_heads(o_ref, state, acc_ref):
    for hh in range(N_HEADS):
        hs = slice(hh * HEAD_DIM, (hh + 1) * HEAD_DIM)
        inv = 1.0 / _allreduce_sublanes(state[hh][1], jnp.add)
        o_t = (_tile3(acc_ref[hs, :]) * inv[None]).reshape(HEAD_DIM, TQ)
        o_ref[0, :, hs] = o_t.T.astype(BF16)


def _sortable_key(score):
    bits = lax.bitcast_convert_type(score, I32)
    key = bits ^ ((bits >> 31) & jnp.int32(0x7FFFFFFF))
    return jnp.where(key == -1, 0, key)


def _dsa_kernel(a_ref, kc_ref, wk_ref, qa_ref, ka_ref, vat_ref, o_ref,
                keys_ref, hi_ref, lo_ref, lo_sel_ref, acc_ref, *, topk, seq_bits):
    j = pl.program_id(1)
    n_tiles = j + 1
    w_rows = wk_ref[0].T[IDX_DIM:IDX_DIM + IDX_HEADS, :]
    w8 = [jnp.broadcast_to(w_rows[hh:hh + 1, :], (SUBLANES, TQ)) for hh in range(IDX_HEADS)]
    causal = lax.broadcasted_iota(I32, (TQ, TQ), 0) <= lax.broadcasted_iota(I32, (TQ, TQ), 1)

    def score_tile(c, diagonal):
        k0 = pl.multiple_of(c * TQ, TQ)
        kc = kc_ref[0, pl.ds(k0, TQ), :]
        acc = jnp.zeros((TQ // SUBLANES, SUBLANES, TQ), F32)
        for hh in range(IDX_HEADS):
            lg = _dot_nt(kc, a_ref[0, :, hh * IDX_K:(hh + 1) * IDX_K])
            acc = acc + w8[hh][None] * jnp.maximum(_tile3(lg), 0.0)
        acc = acc.reshape(TQ, TQ)
        if diagonal:
            acc = jnp.where(causal, acc, NEG)
        key = _sortable_key(acc)
        keys_ref[pl.ds(k0, TQ), :] = key
        hi_ref[pl.ds(k0, TQ), :] = (key >> 16).astype(I16)
        lo_ref[pl.ds(k0, TQ), :] = ((key & 0xFFFF) + I16_MIN).astype(I16)

    def score_body(c, carry):
        score_tile(c, False)
        return carry

    lax.fori_loop(0, j, score_body, 0)
    score_tile(j, True)

    def count(pred):
        def body(c, cnt):
            for u in range(TQ // SEL_ROWS):
                r0 = pl.multiple_of(c * TQ + u * SEL_ROWS, SEL_ROWS)
                cnt = cnt + jnp.where(pred(keys_ref[pl.ds(r0, SEL_ROWS), :], r0), 1.0, 0.0)
            return cnt
        cnt = lax.fori_loop(0, n_tiles, body, jnp.zeros((SEL_ROWS, TQ), F32))
        return jnp.sum(cnt, axis=0, keepdims=True)

    def packed(v32):
        return jnp.broadcast_to(v32, (PACK_ROWS, TQ)).astype(I16)

    def load16(ref, c, u):
        r0 = pl.multiple_of(c * TQ + u * SEL_ROWS, SEL_ROWS)
        return ref[pl.ds(r0, SEL_ROWS), :].reshape(SEL_ROWS // PACK_ROWS, PACK_ROWS, TQ)

    def reduce16(cnt):
        return jnp.sum(jnp.sum(cnt.astype(F32), axis=0), axis=0, keepdims=True)

    cnt16_zero = jnp.zeros((SEL_ROWS // PACK_ROWS, PACK_ROWS, TQ), I16)
    one16, zero16 = jnp.int16(1), jnp.int16(0)

    def count16(ref, pred):
        def body(c, cnt):
            for u in range(TQ // SEL_ROWS):
                cnt = cnt + jnp.where(pred(load16(ref, c, u)), one16, zero16)
            return cnt
        return reduce16(lax.fori_loop(0, n_tiles, body, cnt16_zero))

    def radix16(ref, k_needed, count_all):
        c0 = count16(ref, lambda ch: ch >= zero16)
        ok0 = c0 >= k_needed
        start = (jnp.where(ok0, 0, I16_MIN).astype(I32), jnp.where(ok0, c0, count_all))

        def bit_step(i, carry):
            val, cge = carry
            test = val | (jnp.int32(1) << (14 - i))
            t16 = packed(test)[None]
            c = count16(ref, lambda ch: ch >= t16)
            ok = c >= k_needed
            return jnp.where(ok, test, val), jnp.where(ok, c, cge)

        return lax.fori_loop(0, 15, bit_step, start)

    kf = float(topk)
    hi_k, _ = radix16(hi_ref, kf, (n_tiles * TQ).astype(F32))
    hi_k16 = packed(hi_k)[None]
    need_lo = kf - count16(hi_ref, lambda ch: ch > hi_k16)

    def build_lo(c, cnt):
        for u in range(TQ // SEL_ROWS):
            r0 = pl.multiple_of(c * TQ + u * SEL_ROWS, SEL_ROWS)
            tie = load16(hi_ref, c, u) == hi_k16
            lo_sel_ref[pl.ds(r0, SEL_ROWS), :] = jnp.where(tie, load16(lo_ref, c, u), jnp.int16(I16_MIN)).reshape(
                SEL_ROWS, TQ)
            cnt = cnt + jnp.where(tie, one16, zero16)
        return cnt

    n_tie_hi = reduce16(lax.fori_loop(0, n_tiles, build_lo, cnt16_zero))
    lo_k, c_lo = radix16(lo_sel_ref, need_lo, n_tie_hi)
    cand = (hi_k << 16) + (lo_k - I16_MIN)
    cge = kf - need_lo + c_lo

    @pl.when(jnp.max(jnp.abs(cge - kf)) > 0.0)
    def _():
        need = kf - count(lambda kk, r0: kk > cand)
        sub = lax.broadcasted_iota(I32, (SEL_ROWS, TQ), 0)

        def cut_step(i, cut):
            test = cut | (jnp.int32(1) << (seq_bits - 1 - i))
            f = count(lambda kk, r0: (kk == cand) & (r0 + sub < test))
            return jnp.where(f < need, test, cut)

        cut = lax.fori_loop(0, seq_bits, cut_step, jnp.zeros((1, TQ), I32))

        def demote(c, carry):
            r0 = pl.multiple_of(c * SEL_ROWS, SEL_ROWS)
            kk = keys_ref[pl.ds(r0, SEL_ROWS), :]
            keys_ref[pl.ds(r0, SEL_ROWS), :] = jnp.where((kk == cand) & (r0 + sub > cut), kk - 1, kk)
            return carry

        lax.fori_loop(0, n_tiles * (TQ // SEL_ROWS), demote, 0)

    def attn_tile(c, state, diagonal):
        k0 = pl.multiple_of(c * TQ, TQ)
        sel = keys_ref[pl.ds(k0, TQ), :] >= cand
        if diagonal:
            sel = sel & causal
        s_all = [_dot_nt(ka_ref[0, pl.ds(k0, TQ), hs], qa_ref[0, :, hs]) for hs in _HEAD_SLICES]
        return tuple(
            _softmax_tile(jnp.where(sel, s_all[hh], NEG), vat_ref[0, c, hs, :], state[hh], acc_ref, hs)
            for hh, hs in enumerate(_HEAD_SLICES))

    state = lax.fori_loop(0, j, lambda c, st: attn_tile(c, st, False), _init_state(acc_ref))
    state = attn_tile(j, state, True)
    _write_heads(o_ref, state, acc_ref)


def _dsa_call(a_mat, kc, idx_out, qa, ka, vat):
    bsz, seq, _ = qa.shape
    topk = min(DSA_TOPK_MAX, seq // 4)
    seq_bits = (seq - 1).bit_length()
    kern = functools.partial(_dsa_kernel, topk=topk, seq_bits=seq_bits)
    return pl.pallas_call(
        kern,
        grid=(bsz, seq // TQ),
        in_specs=[
            pl.BlockSpec((1, TQ, IDX_HEADS * IDX_K), lambda b, j: (b, j, 0)),
            pl.BlockSpec((1, seq, IDX_K), lambda b, j: (b, 0, 0)),
            pl.BlockSpec((1, TQ, LANES), lambda b, j: (b, j, 0)),
            pl.BlockSpec((1, TQ, BR_W), lambda b, j: (b, j, 0)),
            pl.BlockSpec((1, seq, BR_W), lambda b, j: (b, 0, 0)),
            pl.BlockSpec((1, seq // TQ, BR_W, TQ), lambda b, j: (b, 0, 0, 0)),
        ],
        out_specs=pl.BlockSpec((1, TQ, BR_W), lambda b, j: (b, j, 0)),
        out_shape=jax.ShapeDtypeStruct((bsz, seq, BR_W), BF16),
        scratch_shapes=[
            pltpu.VMEM((seq, TQ), I32),
            pltpu.VMEM((seq, TQ), I16),
            pltpu.VMEM((seq, TQ), I16),
            pltpu.VMEM((seq, TQ), I16),
            pltpu.VMEM((BR_W, TQ), F32),
        ],
        compiler_params=pltpu.CompilerParams(
            dimension_semantics=("arbitrary", "arbitrary"), vmem_limit_bytes=VMEM_LIMIT),
        name="dsa",
    )(a_mat, kc, idx_out, qa, ka, vat)


def _moba_kernel(qb_ref, kb_ref, vbt_ref, kmean_ref, o_ref, bias_ref, acc_ref, *, n_sel):
    j = pl.program_id(1)
    nb = kmean_ref.shape[1]
    blk = lax.broadcasted_iota(I32, (nb, TQ), 0).astype(F32)
    jf = j.astype(F32)

    for hh in range(N_HEADS):
        hs = slice(hh * HEAD_DIM, (hh + 1) * HEAD_DIM)
        km_hi, km_lo = _split_bf16(kmean_ref[0, :, hs])
        q = qb_ref[0, :, hs]
        gate = jnp.where(blk < jf, _dot_nt(km_hi, q) + _dot_nt(km_lo, q), NEG)
        chosen = jnp.zeros((nb, TQ), F32)
        for _ in range(n_sel):
            best = jnp.max(gate, axis=0, keepdims=True)
            first = jnp.min(jnp.where(gate == best, blk, float(nb)), axis=0, keepdims=True)
            hit = blk == first
            chosen = jnp.where(hit, 1.0, chosen)
            gate = jnp.where(hit, -jnp.inf, gate)
        bias = jnp.where((chosen > 0.0) & (blk < jf), 0.0, NEG)
        for n in range(nb):
            bias_ref[hh, n] = jnp.broadcast_to(bias[n:n + 1, :], (SUBLANES, TQ))

    def past_tile(n, state):
        k0 = pl.multiple_of(n * TQ, TQ)
        s_all = [_dot_nt(kb_ref[0, pl.ds(k0, TQ), hs], qb_ref[0, :, hs]) for hs in _HEAD_SLICES]
        new = []
        for hh, hs in enumerate(_HEAD_SLICES):
            s = (_tile3(s_all[hh]) + bias_ref[hh, n][None]).reshape(TQ, TQ)
            new.append(_softmax_tile(s, vbt_ref[0, n, hs, :], state[hh], acc_ref, hs))
        return tuple(new)

    state = lax.fori_loop(0, j, past_tile, _init_state(acc_ref))

    k0 = pl.multiple_of(j * TQ, TQ)
    causal = lax.broadcasted_iota(I32, (TQ, TQ), 0) <= lax.broadcasted_iota(I32, (TQ, TQ), 1)
    s_all = [_dot_nt(kb_ref[0, pl.ds(k0, TQ), hs], qb_ref[0, :, hs]) for hs in _HEAD_SLICES]
    state = tuple(
        _softmax_tile(jnp.where(causal, s_all[hh], NEG), vbt_ref[0, j, hs, :], state[hh], acc_ref, hs)
        for hh, hs in enumerate(_HEAD_SLICES))
    _write_heads(o_ref, state, acc_ref)


def _moba_call(qb, kb, vbt, kmean):
    bsz, seq, _ = qb.shape
    nb = seq // MOBA_BLOCK
    kern = functools.partial(_moba_kernel, n_sel=min(MOBA_TOPK, nb - 1))
    return pl.pallas_call(
        kern,
        grid=(bsz, seq // TQ),
        in_specs=[
            pl.BlockSpec((1, TQ, BR_W), lambda b, j: (b, j, 0)),
            pl.BlockSpec((1, seq, BR_W), lambda b, j: (b, 0, 0)),
            pl.BlockSpec((1, seq // TQ, BR_W, TQ), lambda b, j: (b, 0, 0, 0)),
            pl.BlockSpec((1, nb, BR_W), lambda b, j: (b, 0, 0)),
        ],
        out_specs=pl.BlockSpec((1, TQ, BR_W), lambda b, j: (b, j, 0)),
        out_shape=jax.ShapeDtypeStruct((bsz, seq, BR_W), BF16),
        scratch_shapes=[
            pltpu.VMEM((N_HEADS, nb, SUBLANES, TQ), F32),
            pltpu.VMEM((BR_W, TQ), F32),
        ],
        compiler_params=pltpu.CompilerParams(
            dimension_semantics=("arbitrary", "arbitrary"), vmem_limit_bytes=VMEM_LIMIT),
        name="moba",
    )(qb, kb, vbt, kmean)


def _out_ffn_kernel(x_ref, oa_ref, ob_ref, ga_ref, gb_ref, mod_ref, gffn_ref,
                    wba_ref, wbb_ref, wo_ref, wgu_ref, wd_ref, o_ref, act_ref):
    merged = (ga_ref[0].astype(F32) * _dot(oa_ref[0], wba_ref[...])
              + gb_ref[0].astype(F32) * _dot(ob_ref[0], wbb_ref[...]))
    x1 = x_ref[0] + mod_ref[0, 2:3, :] * _dot(merged.astype(BF16), wo_ref[...])
    y = x1 * lax.rsqrt(jnp.mean(x1 * x1, axis=-1, keepdims=True) + EPS) * gffn_ref[...]
    h = (y * (1.0 + mod_ref[0, 4:5, :]) + mod_ref[0, 3:4, :]).astype(BF16)
    for c0 in range(0, D_FF, FF_CHUNK):
        g = _dot(h, wgu_ref[:, c0:c0 + FF_CHUNK])
        u = _dot(h, wgu_ref[:, D_FF + c0:D_FF + c0 + FF_CHUNK])
        act_ref[:, c0:c0 + FF_CHUNK] = (g * jax.nn.sigmoid(g) * u).astype(BF16)
    o_ref[0] = x1 + mod_ref[0, 5:6, :] * _dot(act_ref[...], wd_ref[...])


def _out_ffn_call(x, oa, ob, ga, gb, mod3, gffn, wba, wbb, wo, wgu, wd):
    bsz, seq, d = x.shape
    tm = TM_FFN
    const = lambda b, i: (0, 0)
    tok = lambda b, i: (b, i, 0)
    resident = lambda a: pl.BlockSpec(a.shape, const, pipeline_mode=pl.Buffered(1))
    return pl.pallas_call(
        _out_ffn_kernel,
        grid=(bsz, seq // tm),
        in_specs=[
            pl.BlockSpec((1, tm, d), tok),
            pl.BlockSpec((1, tm, BR_W), tok),
            pl.BlockSpec((1, tm, BR_W), tok),
            pl.BlockSpec((1, tm, d), tok),
            pl.BlockSpec((1, tm, d), tok),
            pl.BlockSpec((1, N_MOD, d), lambda b, i: (b, 0, 0)),
            pl.BlockSpec((1, d), const),
            resident(wba), resident(wbb), resident(wo), resident(wgu), resident(wd),
        ],
        out_specs=pl.BlockSpec((1, tm, d), tok),
        out_shape=jax.ShapeDtypeStruct((bsz, seq, d), F32),
        scratch_shapes=[pltpu.VMEM((tm, D_FF), BF16)],
        compiler_params=pltpu.CompilerParams(
            dimension_semantics=("arbitrary", "arbitrary"), vmem_limit_bytes=VMEM_LIMIT),
        name="out_ffn",
    )(x, oa, ob, ga, gb, mod3, gffn, wba, wbb, wo, wgu, wd)


def _rope_tables(seq, dim):
    inv = ROPE_THETA ** (-jnp.arange(0, dim, 2, dtype=F32) / dim)
    ang = jnp.arange(seq, dtype=F32)[:, None] * inv[None, :]
    cos, sin = jnp.cos(ang), jnp.sin(ang)
    reps = LANES // dim
    return (jnp.tile(jnp.concatenate([cos, cos], axis=-1), (1, reps)),
            jnp.tile(jnp.concatenate([-sin, sin], axis=-1), (1, reps)))


def _layer(x, mod, g_mix, g_ffn, w_in, g_qa, g_ka, g_qb, g_kb, w_br_a, w_br_b, w_out, w_gu, w_down):
    bsz, seq, d = x.shape
    mod3 = mod.reshape(bsz, N_MOD, d)

    o = 0
    cols = {}
    for name, width in (("qa", BR_W), ("ka", BR_W), ("va", BR_W), ("qi", IDX_HEADS * IDX_DIM),
                        ("ki", IDX_DIM), ("wi", IDX_HEADS), ("qb", BR_W), ("kb", BR_W), ("vb", BR_W),
                        ("ga", D_MODEL), ("gb", D_MODEL)):
        cols[name] = w_in[:, o:o + width]
        o += width
    wqk = jnp.concatenate([cols["qa"], cols["ka"], cols["qb"], cols["kb"]], axis=1).astype(BF16)
    wvt = jnp.concatenate([cols["va"], cols["vb"]], axis=1).T.astype(BF16)
    wg = jnp.concatenate([cols["ga"], cols["gb"]], axis=1).astype(BF16)
    pad = IDX_OUT_W - IDX_HEADS * IDX_DIM - IDX_DIM - IDX_HEADS
    widx = jnp.concatenate([cols["qi"], cols["ki"], cols["wi"], jnp.zeros((d, pad), F32)], axis=1)
    wih, wil = _split_bf16(widx)
    gains = jnp.stack([g_qa, g_ka, g_qb, g_kb])
    cosh, sinh = _rope_tables(seq, HEAD_DIM)
    cosi, sini = _rope_tables(seq, IDX_DIM)

    qa, ka, qb, kb, vat, vbt, ga, gb, a_mat, kc, kw, kmean = _proj_call(
        x, mod3, g_mix.reshape(1, d), wqk, wvt, wg, wih, wil, gains, cosh, sinh, cosi, sini)

    oa = _dsa_call(a_mat, kc, kw, qa, ka, vat)
    ob = _moba_call(qb, kb, vbt, kmean.reshape(bsz, seq // MOBA_BLOCK, BR_W))

    return _out_ffn_call(x, oa, ob, ga, gb, mod3, g_ffn.reshape(1, d),
                         w_br_a.astype(BF16), w_br_b.astype(BF16), w_out.astype(BF16),
                         w_gu.astype(BF16), w_down.astype(BF16))


def kernel(x, c, w_mod, b_mod, g_mix_norm, g_ffn_norm, w_in, g_q_dsa, g_k_dsa, g_q_moba, g_k_moba,
           w_br_dsa, w_br_moba, w_out, w_gate_up, w_down):
    assert x.shape[1] % TM_PROJ == 0 and x.shape[2] == D_MODEL and w_gate_up.shape[2] == 2 * D_FF
    for l in range(w_mod.shape[0]):
        mod = _mod_call(c, w_mod[l], b_mod[l])
        x = _layer(x, mod, g_mix_norm[l], g_ffn_norm[l], w_in[l], g_q_dsa[l], g_k_dsa[l], g_q_moba[l],
                   g_k_moba[l], w_br_dsa[l], w_br_moba[l], w_out[l], w_gate_up[l], w_down[l])
    return x
```

```python
import functools

import jax
import jax.numpy as jnp
from jax import lax
from jax.experimental import pallas as pl
from jax.experimental.pallas import tpu as pltpu

F32 = jnp.float32
BF16 = jnp.bfloat16
I32 = jnp.int32
I16 = jnp.int16

D_MODEL = 1024
HEAD_DIM = 128
N_HEADS = 4
BR_W = N_HEADS * HEAD_DIM
IDX_HEADS = 8
IDX_DIM = 64
DSA_TOPK_MAX = 256
MOBA_BLOCK = 256
MOBA_TOPK = 3
D_FF = 2816
ROPE_THETA = 10000.0
EPS = 1e-6
NEG = -1e30
LOG2_E = 1.4426950408889634
M_INIT = -1e29
N_MOD = 6
IDX_OUT_W = 640
IDX_K = 256
INT_MIN = -(2 ** 31)
I16_MIN = -(2 ** 15)

LANES = 128
SUBLANES = 8
PACK_ROWS = 16
TQ = 256
TM_PROJ = 512
TM_FFN = 512
FF_CHUNK = 256
SEL_ROWS = 64
VMEM_LIMIT = 56 * 1024 * 1024

_NT = (((1,), (1,)), ((), ()))
_HEAD_SLICES = tuple(slice(h * HEAD_DIM, (h + 1) * HEAD_DIM) for h in range(N_HEADS))


def _split_bf16(x):
    hi = x.astype(BF16)
    lo = (x - hi.astype(F32)).astype(BF16)
    return hi, lo


def _dot(a, b):
    return jnp.dot(a, b, preferred_element_type=F32)


def _dot_nt(a, b):
    return lax.dot_general(a, b, _NT, preferred_element_type=F32)


def _mod_kernel(c_ref, w_ref, b_ref, o_ref):
    c = c_ref[...]
    a_hi, a_lo = _split_bf16(c * jax.nn.sigmoid(c))
    w_hi, w_lo = _split_bf16(w_ref[...])
    o_ref[...] = _dot(a_hi, w_hi) + _dot(a_lo, w_hi) + _dot(a_hi, w_lo) + b_ref[...]


def _mod_call(c, w_mod, b_mod):
    bsz, d = c.shape
    n = w_mod.shape[1]
    tn = 1024
    return pl.pallas_call(
        _mod_kernel,
        grid=(n // tn,),
        in_specs=[
            pl.BlockSpec((bsz, d), lambda i: (0, 0)),
            pl.BlockSpec((d, tn), lambda i: (0, i)),
            pl.BlockSpec((1, tn), lambda i: (0, i)),
        ],
        out_specs=pl.BlockSpec((bsz, tn), lambda i: (0, i)),
        out_shape=jax.ShapeDtypeStruct((bsz, n), F32),
        compiler_params=pltpu.CompilerParams(vmem_limit_bytes=VMEM_LIMIT),
        name="mod",
    )(c, w_mod, b_mod.reshape(1, n))


def _rope_partner_64(x):
    lane = lax.broadcasted_iota(I32, x.shape, 1)
    return jnp.where((lane % IDX_DIM) < IDX_DIM // 2,
                     pltpu.roll(x, LANES - IDX_DIM // 2, 1), pltpu.roll(x, IDX_DIM // 2, 1))


def _proj_kernel(x_ref, mod_ref, gmix_ref, wqk_ref, wvt_ref, wg_ref, wih_ref, wil_ref, gains_ref,
                 cosh_ref, sinh_ref, cosi_ref, sini_ref,
                 qa_ref, ka_ref, qb_ref, kb_ref, vat_ref, vbt_ref, ga_ref, gb_ref, a_ref, kc_ref, kw_ref, kmean_ref):
    x = x_ref[0]
    y = x * lax.rsqrt(jnp.mean(x * x, axis=-1, keepdims=True) + EPS) * gmix_ref[...]
    h = y * (1.0 + mod_ref[0, 1:2, :]) + mod_ref[0, 0:1, :]
    h_hi, h_lo = _split_bf16(h)
    tm = x.shape[0]
    cosh, sinh = cosh_ref[...], sinh_ref[...]
    scale = HEAD_DIM ** -0.5 * LOG2_E

    for gi, out_ref in enumerate((qa_ref, ka_ref, qb_ref, kb_ref)):
        p = _dot(h_hi, wqk_ref[:, gi * BR_W:(gi + 1) * BR_W])
        for hh in range(N_HEADS):
            ph = p[:, hh * HEAD_DIM:(hh + 1) * HEAD_DIM]
            yh = ph * lax.rsqrt(jnp.mean(ph * ph, axis=-1, keepdims=True) + EPS) * gains_ref[gi:gi + 1, :]
            r = yh * cosh + pltpu.roll(yh, HEAD_DIM // 2, 1) * sinh
            if gi == 3:
                for blk in range(tm // MOBA_BLOCK):
                    kmean_ref[0, blk, :, hh * HEAD_DIM:(hh + 1) * HEAD_DIM] = jnp.mean(
                        r[blk * MOBA_BLOCK:(blk + 1) * MOBA_BLOCK], axis=0, keepdims=True)
            if gi % 2 == 0:
                r = r * scale
            out_ref[0, :, hh * HEAD_DIM:(hh + 1) * HEAD_DIM] = r.astype(BF16)

    vt = _dot_nt(wvt_ref[...], h_hi)
    for blk in range(tm // TQ):
        vat_ref[0, blk] = vt[:BR_W, blk * TQ:(blk + 1) * TQ].astype(BF16)
        vbt_ref[0, blk] = vt[BR_W:, blk * TQ:(blk + 1) * TQ].astype(BF16)

    for gi, out_ref in enumerate((ga_ref, gb_ref)):
        g = _dot(h_hi, wg_ref[:, gi * D_MODEL:(gi + 1) * D_MODEL])
        out_ref[0] = jax.nn.sigmoid(g).astype(BF16)

    pi = _dot(h_hi, wih_ref[...]) + _dot(h_lo, wih_ref[...]) + _dot(h_hi, wil_ref[...])
    cosi, sini = cosi_ref[...], sini_ref[...]
    first = lax.broadcasted_iota(I32, (tm, LANES), 1) < IDX_DIM
    half = LANES // 2

    def hi_lo(v):
        hi = v.astype(BF16).astype(F32)
        return hi, v - hi

    n_q = IDX_HEADS * IDX_DIM // LANES
    for g4 in range(n_q):
        xg = pi[:, g4 * LANES:(g4 + 1) * LANES]
        hi, lo = hi_lo(xg * cosi + _rope_partner_64(xg) * sini)
        hi_r, lo_r = pltpu.roll(hi, half, 1), pltpu.roll(lo, half, 1)
        for head, parts in ((2 * g4, (jnp.where(first, hi, lo_r), jnp.where(first, hi, 0.0))),
                            (2 * g4 + 1, (jnp.where(first, hi_r, lo), jnp.where(first, hi_r, 0.0)))):
            for pi_, part in enumerate(parts):
                a_ref[0, :, head * IDX_K + pi_ * LANES:head * IDX_K + (pi_ + 1) * LANES] = part.astype(BF16)
    xg = pi[:, n_q * LANES:(n_q + 1) * LANES]
    roped = xg * cosi + _rope_partner_64(xg) * sini
    hi, lo = hi_lo(roped)
    kc_ref[0, :, :LANES] = jnp.where(first, hi, pltpu.roll(hi, half, 1)).astype(BF16)
    kc_ref[0, :, LANES:] = jnp.where(first, lo, 0.0).astype(BF16)
    w_scale = (IDX_HEADS ** -0.5) * (IDX_DIM ** -0.5)
    kw_ref[0] = jnp.where(first, roped, xg * w_scale)


def _proj_call(x, mod3, gmix, wqk, wvt, wg, wih, wil, gains, cosh, sinh, cosi, sini):
    bsz, seq, d = x.shape
    tm = TM_PROJ
    nb = seq // MOBA_BLOCK
    const = lambda b, i: (0, 0)
    tok = lambda b, i: (b, i, 0)
    tab = lambda b, i: (i, 0)
    out_shape = (
        [jax.ShapeDtypeStruct((bsz, seq, BR_W), BF16)] * 4
        + [jax.ShapeDtypeStruct((bsz, seq // TQ, BR_W, TQ), BF16)] * 2
        + [jax.ShapeDtypeStruct((bsz, seq, D_MODEL), BF16)] * 2
        + [jax.ShapeDtypeStruct((bsz, seq, IDX_HEADS * IDX_K), BF16),
           jax.ShapeDtypeStruct((bsz, seq, IDX_K), BF16),
           jax.ShapeDtypeStruct((bsz, seq, LANES), F32),
           jax.ShapeDtypeStruct((bsz, nb, 1, BR_W), F32)]
    )
    out_specs = (
        [pl.BlockSpec((1, tm, BR_W), tok)] * 4
        + [pl.BlockSpec((1, tm // TQ, BR_W, TQ), lambda b, i: (b, i, 0, 0))] * 2
        + [pl.BlockSpec((1, tm, D_MODEL), tok)] * 2
        + [pl.BlockSpec((1, tm, IDX_HEADS * IDX_K), tok),
           pl.BlockSpec((1, tm, IDX_K), tok),
           pl.BlockSpec((1, tm, LANES), tok),
           pl.BlockSpec((1, tm // MOBA_BLOCK, 1, BR_W), lambda b, i: (b, i, 0, 0))]
    )
    return pl.pallas_call(
        _proj_kernel,
        grid=(bsz, seq // tm),
        in_specs=[
            pl.BlockSpec((1, tm, d), tok),
            pl.BlockSpec((1, N_MOD, d), lambda b, i: (b, 0, 0)),
            pl.BlockSpec((1, d), const),
            pl.BlockSpec(wqk.shape, const),
            pl.BlockSpec(wvt.shape, const),
            pl.BlockSpec(wg.shape, const),
            pl.BlockSpec(wih.shape, const),
            pl.BlockSpec(wil.shape, const),
            pl.BlockSpec(gains.shape, const),
            pl.BlockSpec((tm, LANES), tab),
            pl.BlockSpec((tm, LANES), tab),
            pl.BlockSpec((tm, LANES), tab),
            pl.BlockSpec((tm, LANES), tab),
        ],
        out_specs=out_specs,
        out_shape=out_shape,
        compiler_params=pltpu.CompilerParams(
            dimension_semantics=("arbitrary", "arbitrary"), vmem_limit_bytes=VMEM_LIMIT),
        name="proj",
    )(x, mod3, gmix, wqk, wvt, wg, wih, wil, gains, cosh, sinh, cosi, sini)


def _tile3(x):
    return x.reshape(x.shape[0] // SUBLANES, SUBLANES, x.shape[1])


def _allreduce_sublanes(x, op):
    for shift in (4, 2, 1):
        x = op(x, pltpu.roll(x, shift, 0))
    return x


def _softmax_tile(s, vt, state, acc_ref, hs):
    m_old, l_old = state
    s3 = _tile3(s)
    m_new = jnp.maximum(m_old, _allreduce_sublanes(jnp.max(s3, axis=0), jnp.maximum))
    alpha = jnp.exp2(m_old - m_new)
    p3 = jnp.exp2(s3 - m_new[None])
    pv = _dot(vt, p3.reshape(s.shape).astype(BF16))
    acc_ref[hs, :] = (_tile3(acc_ref[hs, :]) * alpha[None]).reshape(HEAD_DIM, TQ) + pv
    return m_new, alpha * l_old + jnp.sum(p3, axis=0)


def _qk_tile(k_ref, q_ref, t, s_ref):
    k0 = pl.multiple_of(t * TQ, TQ)
    for hh, hs in enumerate(_HEAD_SLICES):
        s_ref[hh] = _dot_nt(k_ref[0, pl.ds(k0, TQ), hs], q_ref[0, :, hs])


def _edge_mask(j, t_true):
    row_minus_col = lax.broadcasted_iota(I32, (TQ, TQ), 0) - lax.broadcasted_iota(I32, (TQ, TQ), 1)
    return row_minus_col <= (j - t_true) * TQ


def _sweep_tiles(j, s_refs, qk, consume, state):
    s0, s1 = s_refs
    last_pair = j // 2
    qk(0, s0)

    def body(p, st):
        t0 = 2 * p
        qk(t0 + 1, s1)
        st = consume(t0, t0, s0, st, False)
        qk(t0 + 2, s0)
        return consume(t0 + 1, t0 + 1, s1, st, False)

    state = lax.fori_loop(0, last_pair, body, state)
    ta = 2 * last_pair
    tb = jnp.minimum(ta + 1, j)
    qk(tb, s1)
    state = consume(ta, ta, s0, state, True)
    return consume(tb, ta + 1, s1, state, True)


def _init_state(acc_ref):
    acc_ref[...] = jnp.zeros(acc_ref.shape, F32)
    one = (jnp.full((SUBLANES, TQ), M_INIT, F32), jnp.zeros((SUBLANES, TQ), F32))
    return (one,) * N_HEADS


def _write_heads(o_ref, state, acc_ref):
    for hh in range(N_HEADS):
        hs = slice(hh * HEAD_DIM, (hh + 1) * HEAD_DIM)
        inv = 1.0 / _allreduce_sublanes(state[hh][1], jnp.add)
        o_t = (_tile3(acc_ref[hs, :]) * inv[None]).reshape(HEAD_DIM, TQ)
        o_ref[0, :, hs] = o_t.T.astype(BF16)


def _sortable_key(score):
    bits = lax.bitcast_convert_type(score, I32)
    key = bits ^ ((bits >> 31) & jnp.int32(0x7FFFFFFF))
    return jnp.where(key == -1, 0, key)


def _dsa_kernel(a_ref, kc_ref, wk_ref, qa_ref, ka_ref, vat_ref, o_ref,
                keys_ref, hi_ref, lo_ref, lo_sel_ref, s0_ref, s1_ref, acc_ref, *, topk, seq_bits):
    j = pl.program_id(1)
    n_tiles = j + 1
    w_rows = wk_ref[0].T[IDX_DIM:IDX_DIM + IDX_HEADS, :]
    w8 = [jnp.broadcast_to(w_rows[hh:hh + 1, :], (SUBLANES, TQ)) for hh in range(IDX_HEADS)]
    causal = lax.broadcasted_iota(I32, (TQ, TQ), 0) <= lax.broadcasted_iota(I32, (TQ, TQ), 1)

    def score_tile(c, diagonal):
        k0 = pl.multiple_of(c * TQ, TQ)
        kc = kc_ref[0, pl.ds(k0, TQ), :]
        acc = jnp.zeros((TQ // SUBLANES, SUBLANES, TQ), F32)
        for hh in range(IDX_HEADS):
            lg = _dot_nt(kc, a_ref[0, :, hh * IDX_K:(hh + 1) * IDX_K])
            acc = acc + w8[hh][None] * jnp.maximum(_tile3(lg), 0.0)
        acc = acc.reshape(TQ, TQ)
        if diagonal:
            acc = jnp.where(causal, acc, NEG)
        key = _sortable_key(acc)
        keys_ref[pl.ds(k0, TQ), :] = key
        hi_ref[pl.ds(k0, TQ), :] = (key >> 16).astype(I16)
        lo_ref[pl.ds(k0, TQ), :] = ((key & 0xFFFF) + I16_MIN).astype(I16)

    def score_pair(p, carry):
        score_tile(2 * p, False)
        score_tile(2 * p + 1, False)
        return carry

    lax.fori_loop(0, j // 2, score_pair, 0)

    @pl.when(j % 2 == 1)
    def _():
        score_tile(j - 1, False)

    score_tile(j, True)

    def count(pred):
        def body(c, cnt):
            for u in range(TQ // SEL_ROWS):
                r0 = pl.multiple_of(c * TQ + u * SEL_ROWS, SEL_ROWS)
                cnt = cnt + jnp.where(pred(keys_ref[pl.ds(r0, SEL_ROWS), :], r0), 1.0, 0.0)
            return cnt
        cnt = lax.fori_loop(0, n_tiles, body, jnp.zeros((SEL_ROWS, TQ), F32))
        return jnp.sum(cnt, axis=0, keepdims=True)

    def packed(v32):
        return jnp.broadcast_to(v32, (PACK_ROWS, TQ)).astype(I16)

    def load16(ref, c, u):
        r0 = pl.multiple_of(c * TQ + u * SEL_ROWS, SEL_ROWS)
        return ref[pl.ds(r0, SEL_ROWS), :].reshape(SEL_ROWS // PACK_ROWS, PACK_ROWS, TQ)

    def reduce16(cnt):
        return jnp.sum(jnp.sum(cnt.astype(F32), axis=0), axis=0, keepdims=True)

    cnt16_zero = jnp.zeros((SEL_ROWS // PACK_ROWS, PACK_ROWS, TQ), I16)
    one16, zero16 = jnp.int16(1), jnp.int16(0)

    def count16(ref, pred):
        def body(c, cnt):
            for u in range(TQ // SEL_ROWS):
                cnt = cnt + jnp.where(pred(load16(ref, c, u)), one16, zero16)
            return cnt
        return reduce16(lax.fori_loop(0, n_tiles, body, cnt16_zero))

    def radix16(ref, k_needed, count_all):
        c0 = count16(ref, lambda ch: ch >= zero16)
        ok0 = c0 >= k_needed
        start = (jnp.where(ok0, 0, I16_MIN).astype(I32), jnp.where(ok0, c0, count_all))

        def bit_step(i, carry):
            val, cge = carry
            test = val | (jnp.int32(1) << (14 - i))
            t16 = packed(test)[None]
            c = count16(ref, lambda ch: ch >= t16)
            ok = c >= k_needed
            return jnp.where(ok, test, val), jnp.where(ok, c, cge)

        return lax.fori_loop(0, 15, bit_step, start)

    kf = float(topk)
    hi_k, _ = radix16(hi_ref, kf, (n_tiles * TQ).astype(F32))
    hi_k16 = packed(hi_k)[None]
    need_lo = kf - count16(hi_ref, lambda ch: ch > hi_k16)

    def build_lo(c, cnt):
        for u in range(TQ // SEL_ROWS):
            r0 = pl.multiple_of(c * TQ + u * SEL_ROWS, SEL_ROWS)
            tie = load16(hi_ref, c, u) == hi_k16
            lo_sel_ref[pl.ds(r0, SEL_ROWS), :] = jnp.where(tie, load16(lo_ref, c, u), jnp.int16(I16_MIN)).reshape(
                SEL_ROWS, TQ)
            cnt = cnt + jnp.where(tie, one16, zero16)
        return cnt

    n_tie_hi = reduce16(lax.fori_loop(0, n_tiles, build_lo, cnt16_zero))
    lo_k, c_lo = radix16(lo_sel_ref, need_lo, n_tie_hi)
    cand = (hi_k << 16) + (lo_k - I16_MIN)
    cge = kf - need_lo + c_lo

    @pl.when(jnp.max(jnp.abs(cge - kf)) > 0.0)
    def _():
        need = kf - count(lambda kk, r0: kk > cand)
        sub = lax.broadcasted_iota(I32, (SEL_ROWS, TQ), 0)

        def cut_step(i, cut):
            test = cut | (jnp.int32(1) << (seq_bits - 1 - i))
            f = count(lambda kk, r0: (kk == cand) & (r0 + sub < test))
            return jnp.where(f < need, test, cut)

        cut = lax.fori_loop(0, seq_bits, cut_step, jnp.zeros((1, TQ), I32))

        def demote(c, carry):
            r0 = pl.multiple_of(c * SEL_ROWS, SEL_ROWS)
            kk = keys_ref[pl.ds(r0, SEL_ROWS), :]
            keys_ref[pl.ds(r0, SEL_ROWS), :] = jnp.where((kk == cand) & (r0 + sub > cut), kk - 1, kk)
            return carry

        lax.fori_loop(0, n_tiles * (TQ // SEL_ROWS), demote, 0)

    def consume(t, t_true, s_ref, state, edge):
        k0 = pl.multiple_of(t * TQ, TQ)
        sel = keys_ref[pl.ds(k0, TQ), :] >= cand
        if edge:
            sel = sel & _edge_mask(j, t_true)
        return tuple(
            _softmax_tile(jnp.where(sel, s_ref[hh], NEG), vat_ref[0, t, hs, :], state[hh], acc_ref, hs)
            for hh, hs in enumerate(_HEAD_SLICES))

    qk = functools.partial(_qk_tile, ka_ref, qa_ref)
    state = _sweep_tiles(j, (s0_ref, s1_ref), qk, consume, _init_state(acc_ref))
    _write_heads(o_ref, state, acc_ref)


def _dsa_call(a_mat, kc, idx_out, qa, ka, vat):
    bsz, seq, _ = qa.shape
    topk = min(DSA_TOPK_MAX, seq // 4)
    seq_bits = (seq - 1).bit_length()
    kern = functools.partial(_dsa_kernel, topk=topk, seq_bits=seq_bits)
    return pl.pallas_call(
        kern,
        grid=(bsz, seq // TQ),
        in_specs=[
            pl.BlockSpec((1, TQ, IDX_HEADS * IDX_K), lambda b, j: (b, j, 0)),
            pl.BlockSpec((1, seq, IDX_K), lambda b, j: (b, 0, 0)),
            pl.BlockSpec((1, TQ, LANES), lambda b, j: (b, j, 0)),
            pl.BlockSpec((1, TQ, BR_W), lambda b, j: (b, j, 0)),
            pl.BlockSpec((1, seq, BR_W), lambda b, j: (b, 0, 0)),
            pl.BlockSpec((1, seq // TQ, BR_W, TQ), lambda b, j: (b, 0, 0, 0)),
        ],
        out_specs=pl.BlockSpec((1, TQ, BR_W), lambda b, j: (b, j, 0)),
        out_shape=jax.ShapeDtypeStruct((bsz, seq, BR_W), BF16),
        scratch_shapes=[
            pltpu.VMEM((seq, TQ), I32),
            pltpu.VMEM((seq, TQ), I16),
            pltpu.VMEM((seq, TQ), I16),
            pltpu.VMEM((seq, TQ), I16),
            pltpu.VMEM((N_HEADS, TQ, TQ), F32),
            pltpu.VMEM((N_HEADS, TQ, TQ), F32),
            pltpu.VMEM((BR_W, TQ), F32),
        ],
        compiler_params=pltpu.CompilerParams(
            dimension_semantics=("arbitrary", "arbitrary"), vmem_limit_bytes=VMEM_LIMIT),
        name="dsa",
    )(a_mat, kc, idx_out, qa, ka, vat)


def _moba_kernel(qb_ref, kb_ref, vbt_ref, kmean_ref, o_ref, bias_ref, s0_ref, s1_ref, acc_ref, *, n_sel):
    j = pl.program_id(1)
    nb = kmean_ref.shape[1]
    blk = lax.broadcasted_iota(I32, (nb, TQ), 0).astype(F32)
    jf = j.astype(F32)

    for hh in range(N_HEADS):
        hs = slice(hh * HEAD_DIM, (hh + 1) * HEAD_DIM)
        km_hi, km_lo = _split_bf16(kmean_ref[0, :, hs])
        q = qb_ref[0, :, hs]
        gate = jnp.where(blk < jf, _dot_nt(km_hi, q) + _dot_nt(km_lo, q), NEG)
        chosen = jnp.zeros((nb, TQ), F32)
        for _ in range(n_sel):
            best = jnp.max(gate, axis=0, keepdims=True)
            first = jnp.min(jnp.where(gate == best, blk, float(nb)), axis=0, keepdims=True)
            hit = blk == first
            chosen = jnp.where(hit, 1.0, chosen)
            gate = jnp.where(hit, -jnp.inf, gate)
        bias = jnp.where((chosen > 0.0) & (blk < jf), 0.0, NEG)
        for n in range(nb):
            bias_ref[hh, n] = jnp.broadcast_to(bias[n:n + 1, :], (SUBLANES, TQ))

    def consume(t, t_true, s_ref, state, edge):
        new = []
        for hh, hs in enumerate(_HEAD_SLICES):
            bias = bias_ref[hh, t]
            if edge:
                bias = jnp.where(t_true < j, bias, 0.0)
            s = (_tile3(s_ref[hh]) + bias[None]).reshape(TQ, TQ)
            if edge:
                s = jnp.where(_edge_mask(j, t_true), s, NEG)
            new.append(_softmax_tile(s, vbt_ref[0, t, hs, :], state[hh], acc_ref, hs))
        return tuple(new)

    qk = functools.partial(_qk_tile, kb_ref, qb_ref)
    state = _sweep_tiles(j, (s0_ref, s1_ref), qk, consume, _init_state(acc_ref))
    _write_heads(o_ref, state, acc_ref)


def _moba_call(qb, kb, vbt, kmean):
    bsz, seq, _ = qb.shape
    nb = seq // MOBA_BLOCK
    kern = functools.partial(_moba_kernel, n_sel=min(MOBA_TOPK, nb - 1))
    return pl.pallas_call(
        kern,
        grid=(bsz, seq // TQ),
        in_specs=[
            pl.BlockSpec((1, TQ, BR_W), lambda b, j: (b, j, 0)),
            pl.BlockSpec((1, seq, BR_W), lambda b, j: (b, 0, 0)),
            pl.BlockSpec((1, seq // TQ, BR_W, TQ), lambda b, j: (b, 0, 0, 0)),
            pl.BlockSpec((1, nb, BR_W), lambda b, j: (b, 0, 0)),
        ],
        out_specs=pl.BlockSpec((1, TQ, BR_W), lambda b, j: (b, j, 0)),
        out_shape=jax.ShapeDtypeStruct((bsz, seq, BR_W), BF16),
        scratch_shapes=[
            pltpu.VMEM((N_HEADS, nb, SUBLANES, TQ), F32),
            pltpu.VMEM((N_HEADS, TQ, TQ), F32),
            pltpu.VMEM((N_HEADS, TQ, TQ), F32),
            pltpu.VMEM((BR_W, TQ), F32),
        ],
        compiler_params=pltpu.CompilerParams(
            dimension_semantics=("arbitrary", "arbitrary"), vmem_limit_bytes=VMEM_LIMIT),
        name="moba",
    )(qb, kb, vbt, kmean)


def _out_ffn_kernel(x_ref, oa_ref, ob_ref, ga_ref, gb_ref, mod_ref, gffn_ref,
                    wba_ref, wbb_ref, wo_ref, wgu_ref, wd_ref, o_ref, act_ref):
    merged = (ga_ref[0].astype(F32) * _dot(oa_ref[0], wba_ref[...])
              + gb_ref[0].astype(F32) * _dot(ob_ref[0], wbb_ref[...]))
    x1 = x_ref[0] + mod_ref[0, 2:3, :] * _dot(merged.astype(BF16), wo_ref[...])
    y = x1 * lax.rsqrt(jnp.mean(x1 * x1, axis=-1, keepdims=True) + EPS) * gffn_ref[...]
    h = (y * (1.0 + mod_ref[0, 4:5, :]) + mod_ref[0, 3:4, :]).astype(BF16)
    for c0 in range(0, D_FF, FF_CHUNK):
        g = _dot(h, wgu_ref[:, c0:c0 + FF_CHUNK])
        u = _dot(h, wgu_ref[:, D_FF + c0:D_FF + c0 + FF_CHUNK])
        act_ref[:, c0:c0 + FF_CHUNK] = (g * jax.nn.sigmoid(g) * u).astype(BF16)
    o_ref[0] = x1 + mod_ref[0, 5:6, :] * _dot(act_ref[...], wd_ref[...])


def _out_ffn_call(x, oa, ob, ga, gb, mod3, gffn, wba, wbb, wo, wgu, wd):
    bsz, seq, d = x.shape
    tm = TM_FFN
    const = lambda b, i: (0, 0)
    tok = lambda b, i: (b, i, 0)
    resident = lambda a: pl.BlockSpec(a.shape, const, pipeline_mode=pl.Buffered(1))
    return pl.pallas_call(
        _out_ffn_kernel,
        grid=(bsz, seq // tm),
        in_specs=[
            pl.BlockSpec((1, tm, d), tok),
            pl.BlockSpec((1, tm, BR_W), tok),
            pl.BlockSpec((1, tm, BR_W), tok),
            pl.BlockSpec((1, tm, d), tok),
            pl.BlockSpec((1, tm, d), tok),
            pl.BlockSpec((1, N_MOD, d), lambda b, i: (b, 0, 0)),
            pl.BlockSpec((1, d), const),
            resident(wba), resident(wbb), resident(wo), resident(wgu), resident(wd),
        ],
        out_specs=pl.BlockSpec((1, tm, d), tok),
        out_shape=jax.ShapeDtypeStruct((bsz, seq, d), F32),
        scratch_shapes=[pltpu.VMEM((tm, D_FF), BF16)],
        compiler_params=pltpu.CompilerParams(
            dimension_semantics=("arbitrary", "arbitrary"), vmem_limit_bytes=VMEM_LIMIT),
        name="out_ffn",
    )(x, oa, ob, ga, gb, mod3, gffn, wba, wbb, wo, wgu, wd)


def _rope_tables(seq, dim):
    inv = ROPE_THETA ** (-jnp.arange(0, dim, 2, dtype=F32) / dim)
    ang = jnp.arange(seq, dtype=F32)[:, None] * inv[None, :]
    cos, sin = jnp.cos(ang), jnp.sin(ang)
    reps = LANES // dim
    return (jnp.tile(jnp.concatenate([cos, cos], axis=-1), (1, reps)),
            jnp.tile(jnp.concatenate([-sin, sin], axis=-1), (1, reps)))


def _layer(x, mod, g_mix, g_ffn, w_in, g_qa, g_ka, g_qb, g_kb, w_br_a, w_br_b, w_out, w_gu, w_down):
    bsz, seq, d = x.shape
    mod3 = mod.reshape(bsz, N_MOD, d)

    o = 0
    cols = {}
    for name, width in (("qa", BR_W), ("ka", BR_W), ("va", BR_W), ("qi", IDX_HEADS * IDX_DIM),
                        ("ki", IDX_DIM), ("wi", IDX_HEADS), ("qb", BR_W), ("kb", BR_W), ("vb", BR_W),
                        ("ga", D_MODEL), ("gb", D_MODEL)):
        cols[name] = w_in[:, o:o + width]
        o += width
    wqk = jnp.concatenate([cols["qa"], cols["ka"], cols["qb"], cols["kb"]], axis=1).astype(BF16)
    wvt = jnp.concatenate([cols["va"], cols["vb"]], axis=1).T.astype(BF16)
    wg = jnp.concatenate([cols["ga"], cols["gb"]], axis=1).astype(BF16)
    pad = IDX_OUT_W - IDX_HEADS * IDX_DIM - IDX_DIM - IDX_HEADS
    widx = jnp.concatenate([cols["qi"], cols["ki"], cols["wi"], jnp.zeros((d, pad), F32)], axis=1)
    wih, wil = _split_bf16(widx)
    gains = jnp.stack([g_qa, g_ka, g_qb, g_kb])
    cosh, sinh = _rope_tables(seq, HEAD_DIM)
    cosi, sini = _rope_tables(seq, IDX_DIM)

    qa, ka, qb, kb, vat, vbt, ga, gb, a_mat, kc, kw, kmean = _proj_call(
        x, mod3, g_mix.reshape(1, d), wqk, wvt, wg, wih, wil, gains, cosh, sinh, cosi, sini)

    oa = _dsa_call(a_mat, kc, kw, qa, ka, vat)
    ob = _moba_call(qb, kb, vbt, kmean.reshape(bsz, seq // MOBA_BLOCK, BR_W))

    return _out_ffn_call(x, oa, ob, ga, gb, mod3, g_ffn.reshape(1, d),
                         w_br_a.astype(BF16), w_br_b.astype(BF16), w_out.astype(BF16),
                         w_gu.astype(BF16), w_down.astype(BF16))


def kernel(x, c, w_mod, b_mod, g_mix_norm, g_ffn_norm, w_in, g_q_dsa, g_k_dsa, g_q_moba, g_k_moba,
           w_br_dsa, w_br_moba, w_out, w_gate_up, w_down):
    assert x.shape[1] % TM_PROJ == 0 and x.shape[2] == D_MODEL and w_gate_up.shape[2] == 2 * D_FF
    for l in range(w_mod.shape[0]):
        mod = _mod_call(c, w_mod[l], b_mod[l])
        x = _layer(x, mod, g_mix_norm[l], g_ffn_norm[l], w_in[l], g_q_dsa[l], g_k_dsa[l], g_q_moba[l],
                   g_k_moba[l], w_br_dsa[l], w_br_moba[l], w_out[l], w_gate_up[l], w_down[l])
    return x
```

```python
import functools

import jax
import jax.numpy as jnp
from jax import lax
from jax.experimental import pallas as pl
from jax.experimental.pallas import tpu as pltpu

F32 = jnp.float32
BF16 = jnp.bfloat16
I32 = jnp.int32
I16 = jnp.int16

D_MODEL = 1024
HEAD_DIM = 128
N_HEADS = 4
BR_W = N_HEADS * HEAD_DIM
IDX_HEADS = 8
IDX_DIM = 64
DSA_TOPK_MAX = 256
MOBA_BLOCK = 256
MOBA_TOPK = 3
D_FF = 2816
ROPE_THETA = 10000.0
EPS = 1e-6
NEG = -1e30
LOG2_E = 1.4426950408889634
M_INIT = -1e29
N_MOD = 6
IDX_OUT_W = 640
IDX_K = 256
INT_MIN = -(2 ** 31)
I16_MIN = -(2 ** 15)

LANES = 128
SUBLANES = 8
PACK_ROWS = 16
VT_ROWS = HEAD_DIM + PACK_ROWS
TQ = 256
TM_PROJ = 512
TM_FFN = 512
FF_CHUNK = 256
SEL_ROWS = 64
VMEM_LIMIT = 56 * 1024 * 1024

_NT = (((1,), (1,)), ((), ()))
_HEAD_SLICES = tuple(slice(h * HEAD_DIM, (h + 1) * HEAD_DIM) for h in range(N_HEADS))


def _split_bf16(x):
    hi = x.astype(BF16)
    lo = (x - hi.astype(F32)).astype(BF16)
    return hi, lo


def _dot(a, b):
    return jnp.dot(a, b, preferred_element_type=F32)


def _dot_nt(a, b):
    return lax.dot_general(a, b, _NT, preferred_element_type=F32)


def _mod_kernel(c_ref, w_ref, b_ref, o_ref):
    c = c_ref[...]
    a_hi, a_lo = _split_bf16(c * jax.nn.sigmoid(c))
    w_hi, w_lo = _split_bf16(w_ref[...])
    o_ref[...] = _dot(a_hi, w_hi) + _dot(a_lo, w_hi) + _dot(a_hi, w_lo) + b_ref[...]


def _mod_call(c, w_mod, b_mod):
    bsz, d = c.shape
    n = w_mod.shape[1]
    tn = 1024
    return pl.pallas_call(
        _mod_kernel,
        grid=(n // tn,),
        in_specs=[
            pl.BlockSpec((bsz, d), lambda i: (0, 0)),
            pl.BlockSpec((d, tn), lambda i: (0, i)),
            pl.BlockSpec((1, tn), lambda i: (0, i)),
        ],
        out_specs=pl.BlockSpec((bsz, tn), lambda i: (0, i)),
        out_shape=jax.ShapeDtypeStruct((bsz, n), F32),
        compiler_params=pltpu.CompilerParams(vmem_limit_bytes=VMEM_LIMIT),
        name="mod",
    )(c, w_mod, b_mod.reshape(1, n))


def _rope_partner_64(x):
    lane = lax.broadcasted_iota(I32, x.shape, 1)
    return jnp.where((lane % IDX_DIM) < IDX_DIM // 2,
                     pltpu.roll(x, LANES - IDX_DIM // 2, 1), pltpu.roll(x, IDX_DIM // 2, 1))


def _proj_kernel(x_ref, mod_ref, gmix_ref, wqk_ref, wvt_ref, wg_ref, wi_ref, gains_ref,
                 cosh_ref, sinh_ref, cosi_ref, sini_ref,
                 qa_ref, ka_ref, qb_ref, kb_ref, vat_ref, vbt_ref, ga_ref, gb_ref, a_ref, kc_ref, kw_ref, kmean_ref):
    x = x_ref[0]
    y = x * lax.rsqrt(jnp.mean(x * x, axis=-1, keepdims=True) + EPS) * gmix_ref[...]
    h = y * (1.0 + mod_ref[0, 1:2, :]) + mod_ref[0, 0:1, :]
    h_hi = h.astype(BF16)
    tm = x.shape[0]
    cosh, sinh = cosh_ref[...], sinh_ref[...]
    scale = HEAD_DIM ** -0.5 * LOG2_E

    for gi, out_ref in enumerate((qa_ref, ka_ref, qb_ref, kb_ref)):
        p = _dot(h_hi, wqk_ref[:, gi * BR_W:(gi + 1) * BR_W])
        for hh in range(N_HEADS):
            ph = p[:, hh * HEAD_DIM:(hh + 1) * HEAD_DIM]
            yh = ph * lax.rsqrt(jnp.mean(ph * ph, axis=-1, keepdims=True) + EPS) * gains_ref[gi:gi + 1, :]
            r = yh * cosh + pltpu.roll(yh, HEAD_DIM // 2, 1) * sinh
            if gi == 3:
                for blk in range(tm // MOBA_BLOCK):
                    kmean_ref[0, blk, :, hh * HEAD_DIM:(hh + 1) * HEAD_DIM] = jnp.mean(
                        r[blk * MOBA_BLOCK:(blk + 1) * MOBA_BLOCK], axis=0, keepdims=True)
            if gi % 2 == 0:
                r = r * scale
            out_ref[0, :, hh * HEAD_DIM:(hh + 1) * HEAD_DIM] = r.astype(BF16)

    vt = _dot_nt(wvt_ref[...], h_hi)
    ones = jnp.ones((VT_ROWS - HEAD_DIM, TQ), BF16)
    for blk in range(tm // TQ):
        for gi, out_ref in enumerate((vat_ref, vbt_ref)):
            for hh in range(N_HEADS):
                src = gi * BR_W + hh * HEAD_DIM
                out_ref[0, blk, hh * VT_ROWS:hh * VT_ROWS + HEAD_DIM, :] = vt[
                    src:src + HEAD_DIM, blk * TQ:(blk + 1) * TQ].astype(BF16)
                out_ref[0, blk, hh * VT_ROWS + HEAD_DIM:(hh + 1) * VT_ROWS, :] = ones

    for gi, out_ref in enumerate((ga_ref, gb_ref)):
        g = _dot(h_hi, wg_ref[:, gi * D_MODEL:(gi + 1) * D_MODEL])
        out_ref[0] = jax.nn.sigmoid(g).astype(BF16)

    pi = _dot(h_hi, wi_ref[...])
    cosi, sini = cosi_ref[...], sini_ref[...]
    first = lax.broadcasted_iota(I32, (tm, LANES), 1) < IDX_DIM
    half = LANES // 2

    def hi_lo(v):
        hi = v.astype(BF16).astype(F32)
        return hi, v - hi

    n_q = IDX_HEADS * IDX_DIM // LANES
    for g4 in range(n_q):
        xg = pi[:, g4 * LANES:(g4 + 1) * LANES]
        hi, lo = hi_lo(xg * cosi + _rope_partner_64(xg) * sini)
        hi_r, lo_r = pltpu.roll(hi, half, 1), pltpu.roll(lo, half, 1)
        for head, parts in ((2 * g4, (jnp.where(first, hi, lo_r), jnp.where(first, hi, 0.0))),
                            (2 * g4 + 1, (jnp.where(first, hi_r, lo), jnp.where(first, hi_r, 0.0)))):
            for pi_, part in enumerate(parts):
                a_ref[0, :, head * IDX_K + pi_ * LANES:head * IDX_K + (pi_ + 1) * LANES] = part.astype(BF16)
    xg = pi[:, n_q * LANES:(n_q + 1) * LANES]
    roped = xg * cosi + _rope_partner_64(xg) * sini
    hi, lo = hi_lo(roped)
    kc_ref[0, :, :LANES] = jnp.where(first, hi, pltpu.roll(hi, half, 1)).astype(BF16)
    kc_ref[0, :, LANES:] = jnp.where(first, lo, 0.0).astype(BF16)
    w_scale = (IDX_HEADS ** -0.5) * (IDX_DIM ** -0.5)
    kw_ref[0] = jnp.where(first, roped, xg * w_scale)


def _proj_call(x, mod3, gmix, wqk, wvt, wg, wi, gains, cosh, sinh, cosi, sini):
    bsz, seq, d = x.shape
    tm = TM_PROJ
    nb = seq // MOBA_BLOCK
    const = lambda b, i: (0, 0)
    tok = lambda b, i: (b, i, 0)
    tab = lambda b, i: (i, 0)
    out_shape = (
        [jax.ShapeDtypeStruct((bsz, seq, BR_W), BF16)] * 4
        + [jax.ShapeDtypeStruct((bsz, seq // TQ, N_HEADS * VT_ROWS, TQ), BF16)] * 2
        + [jax.ShapeDtypeStruct((bsz, seq, D_MODEL), BF16)] * 2
        + [jax.ShapeDtypeStruct((bsz, seq, IDX_HEADS * IDX_K), BF16),
           jax.ShapeDtypeStruct((bsz, seq, IDX_K), BF16),
           jax.ShapeDtypeStruct((bsz, seq, LANES), F32),
           jax.ShapeDtypeStruct((bsz, nb, 1, BR_W), F32)]
    )
    out_specs = (
        [pl.BlockSpec((1, tm, BR_W), tok)] * 4
        + [pl.BlockSpec((1, tm // TQ, N_HEADS * VT_ROWS, TQ), lambda b, i: (b, i, 0, 0))] * 2
        + [pl.BlockSpec((1, tm, D_MODEL), tok)] * 2
        + [pl.BlockSpec((1, tm, IDX_HEADS * IDX_K), tok),
           pl.BlockSpec((1, tm, IDX_K), tok),
           pl.BlockSpec((1, tm, LANES), tok),
           pl.BlockSpec((1, tm // MOBA_BLOCK, 1, BR_W), lambda b, i: (b, i, 0, 0))]
    )
    return pl.pallas_call(
        _proj_kernel,
        grid=(bsz, seq // tm),
        in_specs=[
            pl.BlockSpec((1, tm, d), tok),
            pl.BlockSpec((1, N_MOD, d), lambda b, i: (b, 0, 0)),
            pl.BlockSpec((1, d), const),
            pl.BlockSpec(wqk.shape, const),
            pl.BlockSpec(wvt.shape, const),
            pl.BlockSpec(wg.shape, const),
            pl.BlockSpec(wi.shape, const),
            pl.BlockSpec(gains.shape, const),
            pl.BlockSpec((tm, LANES), tab),
            pl.BlockSpec((tm, LANES), tab),
            pl.BlockSpec((tm, LANES), tab),
            pl.BlockSpec((tm, LANES), tab),
        ],
        out_specs=out_specs,
        out_shape=out_shape,
        compiler_params=pltpu.CompilerParams(
            dimension_semantics=("arbitrary", "arbitrary"), vmem_limit_bytes=VMEM_LIMIT),
        name="proj",
    )(x, mod3, gmix, wqk, wvt, wg, wi, gains, cosh, sinh, cosi, sini)


def _tile3(x):
    return x.reshape(x.shape[0] // SUBLANES, SUBLANES, x.shape[1])


def _allreduce_sublanes(x, op):
    for shift in (4, 2, 1):
        x = op(x, pltpu.roll(x, shift, 0))
    return x


def _softmax_tile(s3, vt, m_old, acc_ref, hh):
    rows = slice(hh * VT_ROWS, (hh + 1) * VT_ROWS)
    m_new = jnp.maximum(m_old, _allreduce_sublanes(jnp.max(s3, axis=0), jnp.maximum))
    alpha = jnp.exp2(m_old - m_new)
    p = jnp.exp2(s3 - m_new[None]).reshape(s3.shape[0] * SUBLANES, TQ).astype(BF16)
    acc_ref[rows, :] = (_tile3(acc_ref[rows, :]) * alpha[None]).reshape(VT_ROWS, TQ) + _dot(vt, p)
    return m_new


def _qk_tile(k_ref, q_ref, t, s_ref):
    k0 = pl.multiple_of(t * TQ, TQ)
    for hh, hs in enumerate(_HEAD_SLICES):
        s_ref[hh] = _dot_nt(k_ref[0, pl.ds(k0, TQ), hs], q_ref[0, :, hs])


def _edge_mask(j, t_true):
    shape = (TQ // SUBLANES, SUBLANES, TQ)
    row = lax.broadcasted_iota(I32, shape, 0) * SUBLANES + lax.broadcasted_iota(I32, shape, 1)
    return row - lax.broadcasted_iota(I32, shape, 2) <= (j - t_true) * TQ


def _sweep_tiles(j, s_refs, qk, consume, state):
    s0, s1 = s_refs
    last_pair = j // 2
    qk(0, s0)

    def body(p, st):
        t0 = 2 * p
        qk(t0 + 1, s1)
        st = consume(t0, t0, s0, st, False)
        qk(t0 + 2, s0)
        return consume(t0 + 1, t0 + 1, s1, st, False)

    state = lax.fori_loop(0, last_pair, body, state)
    ta = 2 * last_pair
    tb = jnp.minimum(ta + 1, j)
    qk(tb, s1)
    state = consume(ta, ta, s0, state, True)
    return consume(tb, ta + 1, s1, state, True)


def _init_state(acc_ref):
    acc_ref[...] = jnp.zeros(acc_ref.shape, F32)
    return (jnp.full((SUBLANES, TQ), M_INIT, F32),) * N_HEADS


def _write_heads(o_ref, acc_ref):
    for hh, hs in enumerate(_HEAD_SLICES):
        base = hh * VT_ROWS
        inv = 1.0 / acc_ref[base + HEAD_DIM:base + HEAD_DIM + SUBLANES, :]
        o_t = (_tile3(acc_ref[base:base + HEAD_DIM, :]) * inv[None]).reshape(HEAD_DIM, TQ)
        o_ref[0, :, hs] = o_t.T.astype(BF16)


def _sortable_key(score):
    bits = lax.bitcast_convert_type(score, I32)
    key = bits ^ ((bits >> 31) & jnp.int32(0x7FFFFFFF))
    return jnp.where(key == -1, 0, key)


def _dsa_kernel(a_ref, kc_ref, wk_ref, qa_ref, ka_ref, vat_ref, o_ref,
                keys_ref, hi_ref, lo_ref, lo_sel_ref, s0_ref, s1_ref, acc_ref, *, topk, seq_bits):
    j = pl.program_id(1)
    n_tiles = j + 1
    w_rows = wk_ref[0].T[IDX_DIM:IDX_DIM + IDX_HEADS, :]
    w8 = [jnp.broadcast_to(w_rows[hh:hh + 1, :], (SUBLANES, TQ)) for hh in range(IDX_HEADS)]
    causal = lax.broadcasted_iota(I32, (TQ, TQ), 0) <= lax.broadcasted_iota(I32, (TQ, TQ), 1)

    def score_tile(c, diagonal):
        k0 = pl.multiple_of(c * TQ, TQ)
        kc = kc_ref[0, pl.ds(k0, TQ), :]
        acc = jnp.zeros((TQ // SUBLANES, SUBLANES, TQ), F32)
        for hh in range(IDX_HEADS):
            lg = _dot_nt(kc, a_ref[0, :, hh * IDX_K:(hh + 1) * IDX_K])
            acc = acc + w8[hh][None] * jnp.maximum(_tile3(lg), 0.0)
        acc = acc.reshape(TQ, TQ)
        if diagonal:
            acc = jnp.where(causal, acc, NEG)
        key = _sortable_key(acc)
        keys_ref[pl.ds(k0, TQ), :] = key
        hi_ref[pl.ds(k0, TQ), :] = (key >> 16).astype(I16)
        lo_ref[pl.ds(k0, TQ), :] = ((key & 0xFFFF) + I16_MIN).astype(I16)

    def score_pair(p, carry):
        score_tile(2 * p, False)
        score_tile(2 * p + 1, False)
        return carry

    lax.fori_loop(0, j // 2, score_pair, 0)

    @pl.when(j % 2 == 1)
    def _():
        score_tile(j - 1, False)

    score_tile(j, True)

    def count(pred):
        def body(c, cnt):
            for u in range(TQ // SEL_ROWS):
                r0 = pl.multiple_of(c * TQ + u * SEL_ROWS, SEL_ROWS)
                cnt = cnt + jnp.where(pred(keys_ref[pl.ds(r0, SEL_ROWS), :], r0), 1.0, 0.0)
            return cnt
        cnt = lax.fori_loop(0, n_tiles, body, jnp.zeros((SEL_ROWS, TQ), F32))
        return jnp.sum(cnt, axis=0, keepdims=True)

    def packed(v32):
        return jnp.broadcast_to(v32, (PACK_ROWS, TQ)).astype(I16)

    def load16(ref, c, u):
        r0 = pl.multiple_of(c * TQ + u * SEL_ROWS, SEL_ROWS)
        return ref[pl.ds(r0, SEL_ROWS), :].reshape(SEL_ROWS // PACK_ROWS, PACK_ROWS, TQ)

    def reduce16(cnt):
        return jnp.sum(jnp.sum(cnt.astype(F32), axis=0), axis=0, keepdims=True)

    cnt16_zero = jnp.zeros((SEL_ROWS // PACK_ROWS, PACK_ROWS, TQ), I16)
    one16, zero16 = jnp.int16(1), jnp.int16(0)

    def count16(ref, pred):
        def body(c, cnt):
            for u in range(TQ // SEL_ROWS):
                cnt = cnt + jnp.where(pred(load16(ref, c, u)), one16, zero16)
            return cnt
        return reduce16(lax.fori_loop(0, n_tiles, body, cnt16_zero))

    def radix16(ref, k_needed, count_all):
        c0 = count16(ref, lambda ch: ch >= zero16)
        ok0 = c0 >= k_needed
        start = (jnp.where(ok0, 0, I16_MIN).astype(I32), jnp.where(ok0, c0, count_all))

        def bit_step(i, carry):
            val, cge = carry
            test = val | (jnp.int32(1) << (14 - i))
            t16 = packed(test)[None]
            c = count16(ref, lambda ch: ch >= t16)
            ok = c >= k_needed
            return jnp.where(ok, test, val), jnp.where(ok, c, cge)

        return lax.fori_loop(0, 15, bit_step, start)

    kf = float(topk)
    hi_k, _ = radix16(hi_ref, kf, (n_tiles * TQ).astype(F32))
    hi_k16 = packed(hi_k)[None]
    need_lo = kf - count16(hi_ref, lambda ch: ch > hi_k16)

    def build_lo(c, cnt):
        for u in range(TQ // SEL_ROWS):
            r0 = pl.multiple_of(c * TQ + u * SEL_ROWS, SEL_ROWS)
            tie = load16(hi_ref, c, u) == hi_k16
            lo_sel_ref[pl.ds(r0, SEL_ROWS), :] = jnp.where(tie, load16(lo_ref, c, u), jnp.int16(I16_MIN)).reshape(
                SEL_ROWS, TQ)
            cnt = cnt + jnp.where(tie, one16, zero16)
        return cnt

    n_tie_hi = reduce16(lax.fori_loop(0, n_tiles, build_lo, cnt16_zero))
    lo_k, c_lo = radix16(lo_sel_ref, need_lo, n_tie_hi)
    cand = (hi_k << 16) + (lo_k - I16_MIN)
    cge = kf - need_lo + c_lo

    @pl.when(jnp.max(jnp.abs(cge - kf)) > 0.0)
    def _():
        need = kf - count(lambda kk, r0: kk > cand)
        sub = lax.broadcasted_iota(I32, (SEL_ROWS, TQ), 0)

        def cut_step(i, cut):
            test = cut | (jnp.int32(1) << (seq_bits - 1 - i))
            f = count(lambda kk, r0: (kk == cand) & (r0 + sub < test))
            return jnp.where(f < need, test, cut)

        cut = lax.fori_loop(0, seq_bits, cut_step, jnp.zeros((1, TQ), I32))

        def demote(c, carry):
            r0 = pl.multiple_of(c * SEL_ROWS, SEL_ROWS)
            kk = keys_ref[pl.ds(r0, SEL_ROWS), :]
            keys_ref[pl.ds(r0, SEL_ROWS), :] = jnp.where((kk == cand) & (r0 + sub > cut), kk - 1, kk)
            return carry

        lax.fori_loop(0, n_tiles * (TQ // SEL_ROWS), demote, 0)

    def consume(t, t_true, s_ref, state, edge):
        k0 = pl.multiple_of(t * TQ, TQ)
        sel = _tile3(keys_ref[pl.ds(k0, TQ), :]) >= cand[None]
        if edge:
            sel = sel & _edge_mask(j, t_true)
        bias = jnp.where(sel, 0.0, NEG)
        return tuple(
            _softmax_tile(_tile3(s_ref[hh]) + bias, vat_ref[0, t, hh * VT_ROWS:(hh + 1) * VT_ROWS, :],
                          state[hh], acc_ref, hh)
            for hh in range(N_HEADS))

    qk = functools.partial(_qk_tile, ka_ref, qa_ref)
    _sweep_tiles(j, (s0_ref, s1_ref), qk, consume, _init_state(acc_ref))
    _write_heads(o_ref, acc_ref)


def _dsa_call(a_mat, kc, idx_out, qa, ka, vat):
    bsz, seq, _ = qa.shape
    topk = min(DSA_TOPK_MAX, seq // 4)
    seq_bits = (seq - 1).bit_length()
    kern = functools.partial(_dsa_kernel, topk=topk, seq_bits=seq_bits)
    return pl.pallas_call(
        kern,
        grid=(bsz, seq // TQ),
        in_specs=[
            pl.BlockSpec((1, TQ, IDX_HEADS * IDX_K), lambda b, j: (b, j, 0)),
            pl.BlockSpec((1, seq, IDX_K), lambda b, j: (b, 0, 0)),
            pl.BlockSpec((1, TQ, LANES), lambda b, j: (b, j, 0)),
            pl.BlockSpec((1, TQ, BR_W), lambda b, j: (b, j, 0)),
            pl.BlockSpec((1, seq, BR_W), lambda b, j: (b, 0, 0)),
            pl.BlockSpec((1, seq // TQ, N_HEADS * VT_ROWS, TQ), lambda b, j: (b, 0, 0, 0)),
        ],
        out_specs=pl.BlockSpec((1, TQ, BR_W), lambda b, j: (b, j, 0)),
        out_shape=jax.ShapeDtypeStruct((bsz, seq, BR_W), BF16),
        scratch_shapes=[
            pltpu.VMEM((seq, TQ), I32),
            pltpu.VMEM((seq, TQ), I16),
            pltpu.VMEM((seq, TQ), I16),
            pltpu.VMEM((seq, TQ), I16),
            pltpu.VMEM((N_HEADS, TQ, TQ), F32),
            pltpu.VMEM((N_HEADS, TQ, TQ), F32),
            pltpu.VMEM((N_HEADS * VT_ROWS, TQ), F32),
        ],
        compiler_params=pltpu.CompilerParams(
            dimension_semantics=("arbitrary", "arbitrary"), vmem_limit_bytes=VMEM_LIMIT),
        name="dsa",
    )(a_mat, kc, idx_out, qa, ka, vat)


def _moba_kernel(qb_ref, kb_ref, vbt_ref, kmean_ref, o_ref, bias_ref, s0_ref, s1_ref, acc_ref, *, n_sel):
    j = pl.program_id(1)
    nb = kmean_ref.shape[1]
    blk = lax.broadcasted_iota(I32, (nb, TQ), 0).astype(F32)
    jf = j.astype(F32)

    for hh in range(N_HEADS):
        hs = slice(hh * HEAD_DIM, (hh + 1) * HEAD_DIM)
        km_hi, km_lo = _split_bf16(kmean_ref[0, :, hs])
        q = qb_ref[0, :, hs]
        gate = jnp.where(blk < jf, _dot_nt(km_hi, q) + _dot_nt(km_lo, q), NEG)
        chosen = jnp.zeros((nb, TQ), F32)
        for _ in range(n_sel):
            best = jnp.max(gate, axis=0, keepdims=True)
            first = jnp.min(jnp.where(gate == best, blk, float(nb)), axis=0, keepdims=True)
            hit = blk == first
            chosen = jnp.where(hit, 1.0, chosen)
            gate = jnp.where(hit, -jnp.inf, gate)
        bias = jnp.where((chosen > 0.0) & (blk < jf), 0.0, NEG)
        for n in range(nb):
            bias_ref[hh, n] = jnp.broadcast_to(bias[n:n + 1, :], (SUBLANES, TQ))

    def consume(t, t_true, s_ref, state, edge):
        new = []
        for hh in range(N_HEADS):
            bias = bias_ref[hh, t]
            if edge:
                bias = jnp.where(t_true < j, bias, 0.0)
            s3 = _tile3(s_ref[hh]) + bias[None]
            if edge:
                s3 = jnp.where(_edge_mask(j, t_true), s3, NEG)
            new.append(_softmax_tile(s3, vbt_ref[0, t, hh * VT_ROWS:(hh + 1) * VT_ROWS, :], state[hh], acc_ref, hh))
        return tuple(new)

    qk = functools.partial(_qk_tile, kb_ref, qb_ref)
    _sweep_tiles(j, (s0_ref, s1_ref), qk, consume, _init_state(acc_ref))
    _write_heads(o_ref, acc_ref)


def _moba_call(qb, kb, vbt, kmean):
    bsz, seq, _ = qb.shape
    nb = seq // MOBA_BLOCK
    kern = functools.partial(_moba_kernel, n_sel=min(MOBA_TOPK, nb - 1))
    return pl.pallas_call(
        kern,
        grid=(bsz, seq // TQ),
        in_specs=[
            pl.BlockSpec((1, TQ, BR_W), lambda b, j: (b, j, 0)),
            pl.BlockSpec((1, seq, BR_W), lambda b, j: (b, 0, 0)),
            pl.BlockSpec((1, seq // TQ, N_HEADS * VT_ROWS, TQ), lambda b, j: (b, 0, 0, 0)),
            pl.BlockSpec((1, nb, BR_W), lambda b, j: (b, 0, 0)),
        ],
        out_specs=pl.BlockSpec((1, TQ, BR_W), lambda b, j: (b, j, 0)),
        out_shape=jax.ShapeDtypeStruct((bsz, seq, BR_W), BF16),
        scratch_shapes=[
            pltpu.VMEM((N_HEADS, nb, SUBLANES, TQ), F32),
            pltpu.VMEM((N_HEADS, TQ, TQ), F32),
            pltpu.VMEM((N_HEADS, TQ, TQ), F32),
            pltpu.VMEM((N_HEADS * VT_ROWS, TQ), F32),
        ],
        compiler_params=pltpu.CompilerParams(
            dimension_semantics=("arbitrary", "arbitrary"), vmem_limit_bytes=VMEM_LIMIT),
        name="moba",
    )(qb, kb, vbt, kmean)


def _out_ffn_kernel(x_ref, oa_ref, ob_ref, ga_ref, gb_ref, mod_ref, gffn_ref,
                    wba_ref, wbb_ref, wo_ref, wgu_ref, wd_ref, o_ref, act_ref):
    merged = (ga_ref[0].astype(F32) * _dot(oa_ref[0], wba_ref[...])
              + gb_ref[0].astype(F32) * _dot(ob_ref[0], wbb_ref[...]))
    x1 = x_ref[0] + mod_ref[0, 2:3, :] * _dot(merged.astype(BF16), wo_ref[...])
    y = x1 * lax.rsqrt(jnp.mean(x1 * x1, axis=-1, keepdims=True) + EPS) * gffn_ref[...]
    h = (y * (1.0 + mod_ref[0, 4:5, :]) + mod_ref[0, 3:4, :]).astype(BF16)
    for c0 in range(0, D_FF, FF_CHUNK):
        g = _dot(h, wgu_ref[:, c0:c0 + FF_CHUNK])
        u = _dot(h, wgu_ref[:, D_FF + c0:D_FF + c0 + FF_CHUNK])
        act_ref[:, c0:c0 + FF_CHUNK] = (g * jax.nn.sigmoid(g) * u).astype(BF16)
    o_ref[0] = x1 + mod_ref[0, 5:6, :] * _dot(act_ref[...], wd_ref[...])


def _out_ffn_call(x, oa, ob, ga, gb, mod3, gffn, wba, wbb, wo, wgu, wd):
    bsz, seq, d = x.shape
    tm = TM_FFN
    const = lambda b, i: (0, 0)
    tok = lambda b, i: (b, i, 0)
    resident = lambda a: pl.BlockSpec(a.shape, const, pipeline_mode=pl.Buffered(1))
    return pl.pallas_call(
        _out_ffn_kernel,
        grid=(bsz, seq // tm),
        in_specs=[
            pl.BlockSpec((1, tm, d), tok),
            pl.BlockSpec((1, tm, BR_W), tok),
            pl.BlockSpec((1, tm, BR_W), tok),
            pl.BlockSpec((1, tm, d), tok),
            pl.BlockSpec((1, tm, d), tok),
            pl.BlockSpec((1, N_MOD, d), lambda b, i: (b, 0, 0)),
            pl.BlockSpec((1, d), const),
            resident(wba), resident(wbb), resident(wo), resident(wgu), resident(wd),
        ],
        out_specs=pl.BlockSpec((1, tm, d), tok),
        out_shape=jax.ShapeDtypeStruct((bsz, seq, d), F32),
        scratch_shapes=[pltpu.VMEM((tm, D_FF), BF16)],
        compiler_params=pltpu.CompilerParams(
            dimension_semantics=("arbitrary", "arbitrary"), vmem_limit_bytes=VMEM_LIMIT),
        name="out_ffn",
    )(x, oa, ob, ga, gb, mod3, gffn, wba, wbb, wo, wgu, wd)


def _rope_tables(seq, dim):
    inv = ROPE_THETA ** (-jnp.arange(0, dim, 2, dtype=F32) / dim)
    ang = jnp.arange(seq, dtype=F32)[:, None] * inv[None, :]
    cos, sin = jnp.cos(ang), jnp.sin(ang)
    reps = LANES // dim
    return (jnp.tile(jnp.concatenate([cos, cos], axis=-1), (1, reps)),
            jnp.tile(jnp.concatenate([-sin, sin], axis=-1), (1, reps)))


def _layer(x, mod, g_mix, g_ffn, w_in, g_qa, g_ka, g_qb, g_kb, w_br_a, w_br_b, w_out, w_gu, w_down):
    bsz, seq, d = x.shape
    mod3 = mod.reshape(bsz, N_MOD, d)

    o = 0
    cols = {}
    for name, width in (("qa", BR_W), ("ka", BR_W), ("va", BR_W), ("qi", IDX_HEADS * IDX_DIM),
                        ("ki", IDX_DIM), ("wi", IDX_HEADS), ("qb", BR_W), ("kb", BR_W), ("vb", BR_W),
                        ("ga", D_MODEL), ("gb", D_MODEL)):
        cols[name] = w_in[:, o:o + width]
        o += width
    wqk = jnp.concatenate([cols["qa"], cols["ka"], cols["qb"], cols["kb"]], axis=1).astype(BF16)
    wvt = jnp.concatenate([cols["va"], cols["vb"]], axis=1).T.astype(BF16)
    wg = jnp.concatenate([cols["ga"], cols["gb"]], axis=1).astype(BF16)
    pad = IDX_OUT_W - IDX_HEADS * IDX_DIM - IDX_DIM - IDX_HEADS
    wi = jnp.concatenate([cols["qi"], cols["ki"], cols["wi"], jnp.zeros((d, pad), F32)], axis=1).astype(BF16)
    gains = jnp.stack([g_qa, g_ka, g_qb, g_kb])
    cosh, sinh = _rope_tables(seq, HEAD_DIM)
    cosi, sini = _rope_tables(seq, IDX_DIM)

    qa, ka, qb, kb, vat, vbt, ga, gb, a_mat, kc, kw, kmean = _proj_call(
        x, mod3, g_mix.reshape(1, d), wqk, wvt, wg, wi, gains, cosh, sinh, cosi, sini)

    oa = _dsa_call(a_mat, kc, kw, qa, ka, vat)
    ob = _moba_call(qb, kb, vbt, kmean.reshape(bsz, seq // MOBA_BLOCK, BR_W))

    return _out_ffn_call(x, oa, ob, ga, gb, mod3, g_ffn.reshape(1, d),
                         w_br_a.astype(BF16), w_br_b.astype(BF16), w_out.astype(BF16),
                         w_gu.astype(BF16), w_down.astype(BF16))


def kernel(x, c, w_mod, b_mod, g_mix_norm, g_ffn_norm, w_in, g_q_dsa, g_k_dsa, g_q_moba, g_k_moba,
           w_br_dsa, w_br_moba, w_out, w_gate_up, w_down):
    assert x.shape[1] % TM_PROJ == 0 and x.shape[2] == D_MODEL and w_gate_up.shape[2] == 2 * D_FF
    for l in range(w_mod.shape[0]):
        mod = _mod_call(c, w_mod[l], b_mod[l])
        x = _layer(x, mod, g_mix_norm[l], g_ffn_norm[l], w_in[l], g_q_dsa[l], g_k_dsa[l], g_q_moba[l],
                   g_k_moba[l], w_br_dsa[l], w_br_moba[l], w_out[l], w_gate_up[l], w_down[l])
    return x
```

```python
import functools

import jax
import jax.numpy as jnp
from jax import lax
from jax.experimental import pallas as pl
from jax.experimental.pallas import tpu as pltpu

F32 = jnp.float32
BF16 = jnp.bfloat16
I32 = jnp.int32
I16 = jnp.int16

D_MODEL = 1024
HEAD_DIM = 128
N_HEADS = 4
BR_W = N_HEADS * HEAD_DIM
IDX_HEADS = 8
IDX_DIM = 64
DSA_TOPK_MAX = 256
MOBA_BLOCK = 256
MOBA_TOPK = 3
D_FF = 2816
ROPE_THETA = 10000.0
EPS = 1e-6
NEG = -1e30
LOG2_E = 1.4426950408889634
M_INIT = -1e29
N_MOD = 6
IDX_OUT_W = 640
IDX_K = 256
INT_MIN = -(2 ** 31)
INT_MAX = 2 ** 31 - 1
I16_MIN = -(2 ** 15)

LANES = 128
SUBLANES = 8
PACK_ROWS = 16
VT_ROWS = HEAD_DIM + PACK_ROWS
TQ = 256
TM_PROJ = 512
PROJ_ROWS = 256
TM_FFN = 512
FF_CHUNK = 256
SEL_ROWS = 64
VMEM_LIMIT = 56 * 1024 * 1024

_NT = (((1,), (1,)), ((), ()))
_HEAD_SLICES = tuple(slice(h * HEAD_DIM, (h + 1) * HEAD_DIM) for h in range(N_HEADS))


def _split_bf16(x):
    hi = x.astype(BF16)
    lo = (x - hi.astype(F32)).astype(BF16)
    return hi, lo


def _dot(a, b):
    return jnp.dot(a, b, preferred_element_type=F32)


def _dot_nt(a, b):
    return lax.dot_general(a, b, _NT, preferred_element_type=F32)


def _mod_kernel(c_ref, w_ref, b_ref, o_ref):
    c = c_ref[...]
    a_hi, a_lo = _split_bf16(c * jax.nn.sigmoid(c))
    w_hi, w_lo = _split_bf16(w_ref[...])
    o_ref[...] = _dot(a_hi, w_hi) + _dot(a_lo, w_hi) + _dot(a_hi, w_lo) + b_ref[...]


def _mod_call(c, w_mod, b_mod):
    bsz, d = c.shape
    n = w_mod.shape[1]
    tn = 1024
    return pl.pallas_call(
        _mod_kernel,
        grid=(n // tn,),
        in_specs=[
            pl.BlockSpec((bsz, d), lambda i: (0, 0)),
            pl.BlockSpec((d, tn), lambda i: (0, i)),
            pl.BlockSpec((1, tn), lambda i: (0, i)),
        ],
        out_specs=pl.BlockSpec((bsz, tn), lambda i: (0, i)),
        out_shape=jax.ShapeDtypeStruct((bsz, n), F32),
        compiler_params=pltpu.CompilerParams(vmem_limit_bytes=VMEM_LIMIT),
        name="mod",
    )(c, w_mod, b_mod.reshape(1, n))


def _rope_partner_64(x):
    lane = lax.broadcasted_iota(I32, x.shape, 1)
    return jnp.where((lane % IDX_DIM) < IDX_DIM // 2,
                     pltpu.roll(x, LANES - IDX_DIM // 2, 1), pltpu.roll(x, IDX_DIM // 2, 1))


def _proj_kernel(x_ref, mod_ref, gmix_ref, wqk_ref, wvt_ref, wg_ref, wi_ref, gains_ref,
                 cosh_ref, sinh_ref, cosi_ref, sini_ref,
                 qa_ref, ka_ref, qb_ref, kb_ref, vat_ref, vbt_ref, ga_ref, gb_ref, a_ref, kc_ref, kw_ref, kmean_ref):
    for blk in range(x_ref.shape[1] // PROJ_ROWS):
        _proj_rows(blk, x_ref, mod_ref, gmix_ref, wqk_ref, wvt_ref, wg_ref, wi_ref, gains_ref,
                   cosh_ref, sinh_ref, cosi_ref, sini_ref,
                   qa_ref, ka_ref, qb_ref, kb_ref, vat_ref, vbt_ref, ga_ref, gb_ref, a_ref, kc_ref, kw_ref, kmean_ref)


def _proj_rows(blk, x_ref, mod_ref, gmix_ref, wqk_ref, wvt_ref, wg_ref, wi_ref, gains_ref,
               cosh_ref, sinh_ref, cosi_ref, sini_ref,
               qa_ref, ka_ref, qb_ref, kb_ref, vat_ref, vbt_ref, ga_ref, gb_ref, a_ref, kc_ref, kw_ref, kmean_ref):
    rows = slice(blk * PROJ_ROWS, (blk + 1) * PROJ_ROWS)
    x = x_ref[0, rows, :]
    y = x * lax.rsqrt(jnp.mean(x * x, axis=-1, keepdims=True) + EPS) * gmix_ref[...]
    h = y * (1.0 + mod_ref[0, 1:2, :]) + mod_ref[0, 0:1, :]
    h_hi = h.astype(BF16)
    cosh, sinh = cosh_ref[rows, :], sinh_ref[rows, :]
    scale = HEAD_DIM ** -0.5 * LOG2_E

    for gi, out_ref in enumerate((qa_ref, ka_ref, qb_ref, kb_ref)):
        p = _dot(h_hi, wqk_ref[:, gi * BR_W:(gi + 1) * BR_W])
        for hh in range(N_HEADS):
            ph = p[:, hh * HEAD_DIM:(hh + 1) * HEAD_DIM]
            yh = ph * lax.rsqrt(jnp.mean(ph * ph, axis=-1, keepdims=True) + EPS) * gains_ref[gi:gi + 1, :]
            r = yh * cosh + pltpu.roll(yh, HEAD_DIM // 2, 1) * sinh
            if gi == 3:
                kmean_ref[0, blk, :, hh * HEAD_DIM:(hh + 1) * HEAD_DIM] = jnp.mean(r, axis=0, keepdims=True)
            if gi % 2 == 0:
                r = r * scale
            out_ref[0, rows, hh * HEAD_DIM:(hh + 1) * HEAD_DIM] = r.astype(BF16)

    vt = _dot_nt(wvt_ref[...], h_hi)
    ones = jnp.ones((VT_ROWS - HEAD_DIM, TQ), BF16)
    for gi, out_ref in enumerate((vat_ref, vbt_ref)):
        for hh in range(N_HEADS):
            src = gi * BR_W + hh * HEAD_DIM
            out_ref[0, blk, hh * VT_ROWS:hh * VT_ROWS + HEAD_DIM, :] = vt[src:src + HEAD_DIM, :].astype(BF16)
            out_ref[0, blk, hh * VT_ROWS + HEAD_DIM:(hh + 1) * VT_ROWS, :] = ones

    for gi, out_ref in enumerate((ga_ref, gb_ref)):
        g = _dot(h_hi, wg_ref[:, gi * D_MODEL:(gi + 1) * D_MODEL])
        out_ref[0, rows, :] = jax.nn.sigmoid(g).astype(BF16)

    pi = _dot(h_hi, wi_ref[...])
    cosi, sini = cosi_ref[rows, :], sini_ref[rows, :]
    first = lax.broadcasted_iota(I32, (PROJ_ROWS, LANES), 1) < IDX_DIM
    half = LANES // 2

    def hi_lo(v):
        hi = v.astype(BF16).astype(F32)
        return hi, v - hi

    n_q = IDX_HEADS * IDX_DIM // LANES
    for g4 in range(n_q):
        xg = pi[:, g4 * LANES:(g4 + 1) * LANES]
        hi, lo = hi_lo(xg * cosi + _rope_partner_64(xg) * sini)
        hi_r, lo_r = pltpu.roll(hi, half, 1), pltpu.roll(lo, half, 1)
        for head, parts in ((2 * g4, (jnp.where(first, hi, lo_r), jnp.where(first, hi, 0.0))),
                            (2 * g4 + 1, (jnp.where(first, hi_r, lo), jnp.where(first, hi_r, 0.0)))):
            for pi_, part in enumerate(parts):
                a_ref[0, rows, head * IDX_K + pi_ * LANES:head * IDX_K + (pi_ + 1) * LANES] = part.astype(BF16)
    xg = pi[:, n_q * LANES:(n_q + 1) * LANES]
    roped = xg * cosi + _rope_partner_64(xg) * sini
    hi, lo = hi_lo(roped)
    kc_ref[0, rows, :LANES] = jnp.where(first, hi, pltpu.roll(hi, half, 1)).astype(BF16)
    kc_ref[0, rows, LANES:] = jnp.where(first, lo, 0.0).astype(BF16)
    w_scale = (IDX_HEADS ** -0.5) * (IDX_DIM ** -0.5)
    kw_ref[0, rows, :] = jnp.where(first, roped, xg * w_scale)


def _proj_call(x, mod3, gmix, wqk, wvt, wg, wi, gains, cosh, sinh, cosi, sini):
    bsz, seq, d = x.shape
    tm = TM_PROJ
    nb = seq // MOBA_BLOCK
    const = lambda b, i: (0, 0)
    tok = lambda b, i: (b, i, 0)
    tab = lambda b, i: (i, 0)
    resident = lambda a: pl.BlockSpec(a.shape, const, pipeline_mode=pl.Buffered(1))
    out_shape = (
        [jax.ShapeDtypeStruct((bsz, seq, BR_W), BF16)] * 4
        + [jax.ShapeDtypeStruct((bsz, seq // TQ, N_HEADS * VT_ROWS, TQ), BF16)] * 2
        + [jax.ShapeDtypeStruct((bsz, seq, D_MODEL), BF16)] * 2
        + [jax.ShapeDtypeStruct((bsz, seq, IDX_HEADS * IDX_K), BF16),
           jax.ShapeDtypeStruct((bsz, seq, IDX_K), BF16),
           jax.ShapeDtypeStruct((bsz, seq, LANES), F32),
           jax.ShapeDtypeStruct((bsz, nb, 1, BR_W), F32)]
    )
    out_specs = (
        [pl.BlockSpec((1, tm, BR_W), tok)] * 4
        + [pl.BlockSpec((1, tm // TQ, N_HEADS * VT_ROWS, TQ), lambda b, i: (b, i, 0, 0))] * 2
        + [pl.BlockSpec((1, tm, D_MODEL), tok)] * 2
        + [pl.BlockSpec((1, tm, IDX_HEADS * IDX_K), tok),
           pl.BlockSpec((1, tm, IDX_K), tok),
           pl.BlockSpec((1, tm, LANES), tok),
           pl.BlockSpec((1, tm // MOBA_BLOCK, 1, BR_W), lambda b, i: (b, i, 0, 0))]
    )
    return pl.pallas_call(
        _proj_kernel,
        grid=(bsz, seq // tm),
        in_specs=[
            pl.BlockSpec((1, tm, d), tok),
            pl.BlockSpec((1, N_MOD, d), lambda b, i: (b, 0, 0)),
            pl.BlockSpec((1, d), const),
            resident(wqk), resident(wvt), resident(wg), resident(wi),
            pl.BlockSpec(gains.shape, const),
            pl.BlockSpec((tm, LANES), tab),
            pl.BlockSpec((tm, LANES), tab),
            pl.BlockSpec((tm, LANES), tab),
            pl.BlockSpec((tm, LANES), tab),
        ],
        out_specs=out_specs,
        out_shape=out_shape,
        compiler_params=pltpu.CompilerParams(
            dimension_semantics=("arbitrary", "arbitrary"), vmem_limit_bytes=VMEM_LIMIT),
        name="proj",
    )(x, mod3, gmix, wqk, wvt, wg, wi, gains, cosh, sinh, cosi, sini)


def _tile3(x):
    return x.reshape(x.shape[0] // SUBLANES, SUBLANES, x.shape[1])


def _allreduce_sublanes(x, op):
    for shift in (4, 2, 1):
        x = op(x, pltpu.roll(x, shift, 0))
    return x


def _softmax_tile(s3, vt, m_old, acc_ref, hh):
    rows = slice(hh * VT_ROWS, (hh + 1) * VT_ROWS)
    m_new = jnp.maximum(m_old, _allreduce_sublanes(jnp.max(s3, axis=0), jnp.maximum))
    alpha = jnp.exp2(m_old - m_new)
    p = jnp.exp2(s3 - m_new[None]).reshape(s3.shape[0] * SUBLANES, TQ).astype(BF16)
    acc_ref[rows, :] = (_tile3(acc_ref[rows, :]) * alpha[None]).reshape(VT_ROWS, TQ) + _dot(vt, p)
    return m_new


def _qk_tile(k_ref, q_ref, t, s_ref):
    k0 = pl.multiple_of(t * TQ, TQ)
    for hh, hs in enumerate(_HEAD_SLICES):
        s_ref[hh] = _dot_nt(k_ref[0, pl.ds(k0, TQ), hs], q_ref[0, :, hs])


def _causal_triangle():
    shape = (TQ // SUBLANES, SUBLANES, TQ)
    row = lax.broadcasted_iota(I32, shape, 0) * SUBLANES + lax.broadcasted_iota(I32, shape, 1)
    return row <= lax.broadcasted_iota(I32, shape, 2)


def _sweep_tiles(j, s_refs, qk, consume, state):
    s0, s1 = s_refs
    odd = j % 2
    qk(0, s0)

    def body(p, st):
        t0 = 2 * p - odd
        qk(t0 + 1, s1)
        st = consume(t0, s0, st, "first")
        qk(t0 + 2, s0)
        return consume(t0 + 1, s1, st, "past")

    state = lax.fori_loop(0, (j + odd) // 2, body, state)
    return consume(j, s0, state, "diagonal")


def _init_state(acc_ref):
    acc_ref[...] = jnp.zeros(acc_ref.shape, F32)
    return (jnp.full((SUBLANES, TQ), M_INIT, F32),) * N_HEADS


def _write_heads(o_ref, acc_ref):
    for hh, hs in enumerate(_HEAD_SLICES):
        base = hh * VT_ROWS
        inv = 1.0 / acc_ref[base + HEAD_DIM:base + HEAD_DIM + SUBLANES, :]
        o_t = (_tile3(acc_ref[base:base + HEAD_DIM, :]) * inv[None]).reshape(HEAD_DIM, TQ)
        o_ref[0, :, hs] = o_t.T.astype(BF16)


def _sortable_key(score):
    bits = lax.bitcast_convert_type(score, I32)
    key = bits ^ ((bits >> 31) & jnp.int32(0x7FFFFFFF))
    return jnp.where(key == -1, 0, key)


def _dsa_kernel(a_ref, kc_ref, wk_ref, qa_ref, ka_ref, vat_ref, o_ref,
                keys_ref, hi_ref, lo_ref, lo_sel_ref, s0_ref, s1_ref, acc_ref, *, topk, seq_bits):
    j = pl.program_id(1)
    n_tiles = j + 1
    w_rows = wk_ref[0].T[IDX_DIM:IDX_DIM + IDX_HEADS, :]
    w8 = [jnp.broadcast_to(w_rows[hh:hh + 1, :], (SUBLANES, TQ)) for hh in range(IDX_HEADS)]

    def score_tile(c, diagonal):
        k0 = pl.multiple_of(c * TQ, TQ)
        kc = kc_ref[0, pl.ds(k0, TQ), :]
        acc = jnp.zeros((TQ // SUBLANES, SUBLANES, TQ), F32)
        for hh in range(IDX_HEADS):
            lg = _dot_nt(kc, a_ref[0, :, hh * IDX_K:(hh + 1) * IDX_K])
            acc = acc + w8[hh][None] * jnp.maximum(_tile3(lg), 0.0)
        if diagonal:
            acc = jnp.where(_causal_triangle(), acc, NEG)
        key = _sortable_key(acc.reshape(TQ, TQ))
        keys_ref[pl.ds(k0, TQ), :] = key
        hi_ref[pl.ds(k0, TQ), :] = (key >> 16).astype(I16)
        lo_ref[pl.ds(k0, TQ), :] = ((key & 0xFFFF) + I16_MIN).astype(I16)

    def score_pair(p, carry):
        score_tile(2 * p, False)
        score_tile(2 * p + 1, False)
        return carry

    lax.fori_loop(0, j // 2, score_pair, 0)

    @pl.when(j % 2 == 1)
    def _():
        score_tile(j - 1, False)

    score_tile(j, True)

    def count(pred):
        def body(c, cnt):
            for u in range(TQ // SEL_ROWS):
                r0 = pl.multiple_of(c * TQ + u * SEL_ROWS, SEL_ROWS)
                cnt = cnt + jnp.where(pred(keys_ref[pl.ds(r0, SEL_ROWS), :], r0), 1.0, 0.0)
            return cnt
        cnt = lax.fori_loop(0, n_tiles, body, jnp.zeros((SEL_ROWS, TQ), F32))
        return jnp.sum(cnt, axis=0, keepdims=True)

    def packed(v32):
        return jnp.broadcast_to(v32, (PACK_ROWS, TQ)).astype(I16)

    def load16(ref, c, u):
        r0 = pl.multiple_of(c * TQ + u * SEL_ROWS, SEL_ROWS)
        return ref[pl.ds(r0, SEL_ROWS), :].reshape(SEL_ROWS // PACK_ROWS, PACK_ROWS, TQ)

    def reduce16(cnt):
        return jnp.sum(jnp.sum(cnt.astype(F32), axis=0), axis=0, keepdims=True)

    cnt16_zero = jnp.zeros((SEL_ROWS // PACK_ROWS, PACK_ROWS, TQ), I16)
    one16, zero16 = jnp.int16(1), jnp.int16(0)

    def count16(ref, pred):
        def body(c, cnt):
            for u in range(TQ // SEL_ROWS):
                cnt = cnt + jnp.where(pred(load16(ref, c, u)), one16, zero16)
            return cnt
        return reduce16(lax.fori_loop(0, n_tiles, body, cnt16_zero))

    def radix16(ref, k_needed, count_all):
        c0 = count16(ref, lambda ch: ch >= zero16)
        ok0 = c0 >= k_needed
        start = (jnp.where(ok0, 0, I16_MIN).astype(I32), jnp.where(ok0, c0, count_all), jnp.where(ok0, 0.0, c0))

        def bit_step(i, carry):
            val, cge, cgt = carry
            test = val | (jnp.int32(1) << (14 - i))
            t16 = packed(test)[None]
            c = count16(ref, lambda ch: ch >= t16)
            ok = c >= k_needed
            return jnp.where(ok, test, val), jnp.where(ok, c, cge), jnp.where(ok, cgt, c)

        return lax.fori_loop(0, 15, bit_step, start)

    kf = float(topk)
    hi_k, _, above_hi = radix16(hi_ref, kf, (n_tiles * TQ).astype(F32))
    hi_k16 = packed(hi_k)[None]
    need_lo = kf - above_hi

    def build_lo(c, cnt):
        for u in range(TQ // SEL_ROWS):
            r0 = pl.multiple_of(c * TQ + u * SEL_ROWS, SEL_ROWS)
            tie = load16(hi_ref, c, u) == hi_k16
            lo_sel_ref[pl.ds(r0, SEL_ROWS), :] = jnp.where(tie, load16(lo_ref, c, u), jnp.int16(I16_MIN)).reshape(
                SEL_ROWS, TQ)
            cnt = cnt + jnp.where(tie, one16, zero16)
        return cnt

    n_tie_hi = reduce16(lax.fori_loop(0, n_tiles, build_lo, cnt16_zero))
    lo_k, c_lo, _ = radix16(lo_sel_ref, need_lo, n_tie_hi)
    cand = (hi_k << 16) + (lo_k - I16_MIN)
    cge = kf - need_lo + c_lo

    @pl.when(jnp.max(jnp.abs(cge - kf)) > 0.0)
    def _():
        need = kf - count(lambda kk, r0: kk > cand)
        sub = lax.broadcasted_iota(I32, (SEL_ROWS, TQ), 0)

        def cut_step(i, cut):
            test = cut | (jnp.int32(1) << (seq_bits - 1 - i))
            f = count(lambda kk, r0: (kk == cand) & (r0 + sub < test))
            return jnp.where(f < need, test, cut)

        cut = lax.fori_loop(0, seq_bits, cut_step, jnp.zeros((1, TQ), I32))

        def demote(c, carry):
            r0 = pl.multiple_of(c * SEL_ROWS, SEL_ROWS)
            kk = keys_ref[pl.ds(r0, SEL_ROWS), :]
            keys_ref[pl.ds(r0, SEL_ROWS), :] = jnp.where((kk == cand) & (r0 + sub > cut), kk - 1, kk)
            return carry

        lax.fori_loop(0, n_tiles * (TQ // SEL_ROWS), demote, 0)

    def consume(t, s_ref, state, kind):
        thr = cand
        if kind == "first":
            thr = jnp.where(t < 0, INT_MAX, cand)
            t = jnp.maximum(t, 0)
        k0 = pl.multiple_of(t * TQ, TQ)
        sel = _tile3(keys_ref[pl.ds(k0, TQ), :]) >= thr[None]
        if kind == "diagonal":
            sel = sel & _causal_triangle()
        bias = jnp.where(sel, 0.0, NEG)
        return tuple(
            _softmax_tile(_tile3(s_ref[hh]) + bias, vat_ref[0, t, hh * VT_ROWS:(hh + 1) * VT_ROWS, :],
                          state[hh], acc_ref, hh)
            for hh in range(N_HEADS))

    qk = functools.partial(_qk_tile, ka_ref, qa_ref)
    _sweep_tiles(j, (s0_ref, s1_ref), qk, consume, _init_state(acc_ref))
    _write_heads(o_ref, acc_ref)


def _dsa_call(a_mat, kc, idx_out, qa, ka, vat):
    bsz, seq, _ = qa.shape
    topk = min(DSA_TOPK_MAX, seq // 4)
    seq_bits = (seq - 1).bit_length()
    kern = functools.partial(_dsa_kernel, topk=topk, seq_bits=seq_bits)
    return pl.pallas_call(
        kern,
        grid=(bsz, seq // TQ),
        in_specs=[
            pl.BlockSpec((1, TQ, IDX_HEADS * IDX_K), lambda b, j: (b, j, 0)),
            pl.BlockSpec((1, seq, IDX_K), lambda b, j: (b, 0, 0)),
            pl.BlockSpec((1, TQ, LANES), lambda b, j: (b, j, 0)),
            pl.BlockSpec((1, TQ, BR_W), lambda b, j: (b, j, 0)),
            pl.BlockSpec((1, seq, BR_W), lambda b, j: (b, 0, 0)),
            pl.BlockSpec((1, seq // TQ, N_HEADS * VT_ROWS, TQ), lambda b, j: (b, 0, 0, 0)),
        ],
        out_specs=pl.BlockSpec((1, TQ, BR_W), lambda b, j: (b, j, 0)),
        out_shape=jax.ShapeDtypeStruct((bsz, seq, BR_W), BF16),
        scratch_shapes=[
            pltpu.VMEM((seq, TQ), I32),
            pltpu.VMEM((seq, TQ), I16),
            pltpu.VMEM((seq, TQ), I16),
            pltpu.VMEM((seq, TQ), I16),
            pltpu.VMEM((N_HEADS, TQ, TQ), F32),
            pltpu.VMEM((N_HEADS, TQ, TQ), F32),
            pltpu.VMEM((N_HEADS * VT_ROWS, TQ), F32),
        ],
        compiler_params=pltpu.CompilerParams(
            dimension_semantics=("arbitrary", "arbitrary"), vmem_limit_bytes=VMEM_LIMIT),
        name="dsa",
    )(a_mat, kc, idx_out, qa, ka, vat)


def _moba_kernel(qb_ref, kb_ref, vbt_ref, kmean_ref, o_ref, bias_ref, s0_ref, s1_ref, acc_ref, *, n_sel):
    j = pl.program_id(1)
    nb = kmean_ref.shape[1]
    blk = lax.broadcasted_iota(I32, (nb, TQ), 0).astype(F32)
    jf = j.astype(F32)

    for hh in range(N_HEADS):
        hs = slice(hh * HEAD_DIM, (hh + 1) * HEAD_DIM)
        km_hi, km_lo = _split_bf16(kmean_ref[0, :, hs])
        q = qb_ref[0, :, hs]
        gate = jnp.where(blk < jf, _dot_nt(km_hi, q) + _dot_nt(km_lo, q), NEG)
        chosen = jnp.zeros((nb, TQ), F32)
        for _ in range(n_sel):
            best = jnp.max(gate, axis=0, keepdims=True)
            first = jnp.min(jnp.where(gate == best, blk, float(nb)), axis=0, keepdims=True)
            hit = blk == first
            chosen = jnp.where(hit, 1.0, chosen)
            gate = jnp.where(hit, -jnp.inf, gate)
        bias = jnp.where((chosen > 0.0) & (blk < jf), 0.0, NEG)
        bias_ref[hh, 0] = jnp.full((SUBLANES, TQ), NEG, F32)
        for n in range(nb):
            bias_ref[hh, n + 1] = jnp.broadcast_to(bias[n:n + 1, :], (SUBLANES, TQ))

    def consume(t, s_ref, state, kind):
        t_data = jnp.maximum(t, 0) if kind == "first" else t
        new = []
        for hh in range(N_HEADS):
            if kind == "diagonal":
                s3 = jnp.where(_causal_triangle(), _tile3(s_ref[hh]), NEG)
            else:
                s3 = _tile3(s_ref[hh]) + bias_ref[hh, t + 1][None]
            new.append(_softmax_tile(s3, vbt_ref[0, t_data, hh * VT_ROWS:(hh + 1) * VT_ROWS, :],
                                     state[hh], acc_ref, hh))
        return tuple(new)

    qk = functools.partial(_qk_tile, kb_ref, qb_ref)
    _sweep_tiles(j, (s0_ref, s1_ref), qk, consume, _init_state(acc_ref))
    _write_heads(o_ref, acc_ref)


def _moba_call(qb, kb, vbt, kmean):
    bsz, seq, _ = qb.shape
    nb = seq // MOBA_BLOCK
    kern = functools.partial(_moba_kernel, n_sel=min(MOBA_TOPK, nb - 1))
    return pl.pallas_call(
        kern,
        grid=(bsz, seq // TQ),
        in_specs=[
            pl.BlockSpec((1, TQ, BR_W), lambda b, j: (b, j, 0)),
            pl.BlockSpec((1, seq, BR_W), lambda b, j: (b, 0, 0)),
            pl.BlockSpec((1, seq // TQ, N_HEADS * VT_ROWS, TQ), lambda b, j: (b, 0, 0, 0)),
            pl.BlockSpec((1, nb, BR_W), lambda b, j: (b, 0, 0)),
        ],
        out_specs=pl.BlockSpec((1, TQ, BR_W), lambda b, j: (b, j, 0)),
        out_shape=jax.ShapeDtypeStruct((bsz, seq, BR_W), BF16),
        scratch_shapes=[
            pltpu.VMEM((N_HEADS, nb + 1, SUBLANES, TQ), F32),
            pltpu.VMEM((N_HEADS, TQ, TQ), F32),
            pltpu.VMEM((N_HEADS, TQ, TQ), F32),
            pltpu.VMEM((N_HEADS * VT_ROWS, TQ), F32),
        ],
        compiler_params=pltpu.CompilerParams(
            dimension_semantics=("arbitrary", "arbitrary"), vmem_limit_bytes=VMEM_LIMIT),
        name="moba",
    )(qb, kb, vbt, kmean)


def _out_ffn_kernel(x_ref, oa_ref, ob_ref, ga_ref, gb_ref, mod_ref, gffn_ref,
                    wba_ref, wbb_ref, wo_ref, wgu_ref, wd_ref, o_ref, act_ref):
    merged = (ga_ref[0].astype(F32) * _dot(oa_ref[0], wba_ref[...])
              + gb_ref[0].astype(F32) * _dot(ob_ref[0], wbb_ref[...]))
    x1 = x_ref[0] + mod_ref[0, 2:3, :] * _dot(merged.astype(BF16), wo_ref[...])
    y = x1 * lax.rsqrt(jnp.mean(x1 * x1, axis=-1, keepdims=True) + EPS) * gffn_ref[...]
    h = (y * (1.0 + mod_ref[0, 4:5, :]) + mod_ref[0, 3:4, :]).astype(BF16)
    for c0 in range(0, D_FF, FF_CHUNK):
        g = _dot(h, wgu_ref[:, c0:c0 + FF_CHUNK])
        u = _dot(h, wgu_ref[:, D_FF + c0:D_FF + c0 + FF_CHUNK])
        act_ref[:, c0:c0 + FF_CHUNK] = (g * jax.nn.sigmoid(g) * u).astype(BF16)
    o_ref[0] = x1 + mod_ref[0, 5:6, :] * _dot(act_ref[...], wd_ref[...])


def _out_ffn_call(x, oa, ob, ga, gb, mod3, gffn, wba, wbb, wo, wgu, wd):
    bsz, seq, d = x.shape
    tm = TM_FFN
    const = lambda b, i: (0, 0)
    tok = lambda b, i: (b, i, 0)
    resident = lambda a: pl.BlockSpec(a.shape, const, pipeline_mode=pl.Buffered(1))
    return pl.pallas_call(
        _out_ffn_kernel,
        grid=(bsz, seq // tm),
        in_specs=[
            pl.BlockSpec((1, tm, d), tok),
            pl.BlockSpec((1, tm, BR_W), tok),
            pl.BlockSpec((1, tm, BR_W), tok),
            pl.BlockSpec((1, tm, d), tok),
            pl.BlockSpec((1, tm, d), tok),
            pl.BlockSpec((1, N_MOD, d), lambda b, i: (b, 0, 0)),
            pl.BlockSpec((1, d), const),
            resident(wba), resident(wbb), resident(wo), resident(wgu), resident(wd),
        ],
        out_specs=pl.BlockSpec((1, tm, d), tok),
        out_shape=jax.ShapeDtypeStruct((bsz, seq, d), F32),
        scratch_shapes=[pltpu.VMEM((tm, D_FF), BF16)],
        compiler_params=pltpu.CompilerParams(
            dimension_semantics=("arbitrary", "arbitrary"), vmem_limit_bytes=VMEM_LIMIT),
        name="out_ffn",
    )(x, oa, ob, ga, gb, mod3, gffn, wba, wbb, wo, wgu, wd)


def _rope_tables(seq, dim):
    inv = ROPE_THETA ** (-jnp.arange(0, dim, 2, dtype=F32) / dim)
    ang = jnp.arange(seq, dtype=F32)[:, None] * inv[None, :]
    cos, sin = jnp.cos(ang), jnp.sin(ang)
    reps = LANES // dim
    return (jnp.tile(jnp.concatenate([cos, cos], axis=-1), (1, reps)),
            jnp.tile(jnp.concatenate([-sin, sin], axis=-1), (1, reps)))


def _layer(x, mod, g_mix, g_ffn, w_in, g_qa, g_ka, g_qb, g_kb, w_br_a, w_br_b, w_out, w_gu, w_down):
    bsz, seq, d = x.shape
    mod3 = mod.reshape(bsz, N_MOD, d)

    o = 0
    cols = {}
    for name, width in (("qa", BR_W), ("ka", BR_W), ("va", BR_W), ("qi", IDX_HEADS * IDX_DIM),
                        ("ki", IDX_DIM), ("wi", IDX_HEADS), ("qb", BR_W), ("kb", BR_W), ("vb", BR_W),
                        ("ga", D_MODEL), ("gb", D_MODEL)):
        cols[name] = w_in[:, o:o + width]
        o += width
    wqk = jnp.concatenate([cols["qa"], cols["ka"], cols["qb"], cols["kb"]], axis=1).astype(BF16)
    wvt = jnp.concatenate([cols["va"], cols["vb"]], axis=1).T.astype(BF16)
    wg = jnp.concatenate([cols["ga"], cols["gb"]], axis=1).astype(BF16)
    pad = IDX_OUT_W - IDX_HEADS * IDX_DIM - IDX_DIM - IDX_HEADS
    wi = jnp.concatenate([cols["qi"], cols["ki"], cols["wi"], jnp.zeros((d, pad), F32)], axis=1).astype(BF16)
    gains = jnp.stack([g_qa, g_ka, g_qb, g_kb])
    cosh, sinh = _rope_tables(seq, HEAD_DIM)
    cosi, sini = _rope_tables(seq, IDX_DIM)

    qa, ka, qb, kb, vat, vbt, ga, gb, a_mat, kc, kw, kmean = _proj_call(
        x, mod3, g_mix.reshape(1, d), wqk, wvt, wg, wi, gains, cosh, sinh, cosi, sini)

    oa = _dsa_call(a_mat, kc, kw, qa, ka, vat)
    ob = _moba_call(qb, kb, vbt, kmean.reshape(bsz, seq // MOBA_BLOCK, BR_W))

    return _out_ffn_call(x, oa, ob, ga, gb, mod3, g_ffn.reshape(1, d),
                         w_br_a.astype(BF16), w_br_b.astype(BF16), w_out.astype(BF16),
                         w_gu.astype(BF16), w_down.astype(BF16))


def kernel(x, c, w_mod, b_mod, g_mix_norm, g_ffn_norm, w_in, g_q_dsa, g_k_dsa, g_q_moba, g_k_moba,
           w_br_dsa, w_br_moba, w_out, w_gate_up, w_down):
    assert x.shape[1] % TM_PROJ == 0 and x.shape[2] == D_MODEL and w_gate_up.shape[2] == 2 * D_FF
    for l in range(w_mod.shape[0]):
        mod = _mod_call(c, w_mod[l], b_mod[l])
        x = _layer(x, mod, g_mix_norm[l], g_ffn_norm[l], w_in[l], g_q_dsa[l], g_k_dsa[l], g_q_moba[l],
                   g_k_moba[l], w_br_dsa[l], w_br_moba[l], w_out[l], w_gate_up[l], w_down[l])
    return x
```

```python
import functools

import jax
import jax.numpy as jnp
from jax import lax
from jax.experimental import pallas as pl
from jax.experimental.pallas import tpu as pltpu

F32 = jnp.float32
BF16 = jnp.bfloat16
I32 = jnp.int32
I16 = jnp.int16

D_MODEL = 1024
HEAD_DIM = 128
N_HEADS = 4
BR_W = N_HEADS * HEAD_DIM
IDX_HEADS = 8
IDX_DIM = 64
DSA_TOPK_MAX = 256
MOBA_BLOCK = 256
MOBA_TOPK = 3
D_FF = 2816
ROPE_THETA = 10000.0
EPS = 1e-6
NEG = -1e30
LOG2_E = 1.4426950408889634
M_INIT = -1e29
N_MOD = 6
IDX_OUT_W = 640
IDX_K = 256
INT_MIN = -(2 ** 31)
I16_MIN = -(2 ** 15)

LANES = 128
SUBLANES = 8
PACK_ROWS = 16
VT_ROWS = HEAD_DIM + PACK_ROWS
TQ = 256
TM_PROJ = 512
PROJ_ROWS = 256
TM_FFN = 512
FF_CHUNK = 256
SEL_ROWS = 32
VMEM_LIMIT = 56 * 1024 * 1024

_NT = (((1,), (1,)), ((), ()))
_HEAD_SLICES = tuple(slice(h * HEAD_DIM, (h + 1) * HEAD_DIM) for h in range(N_HEADS))


def _split_bf16(x):
    hi = x.astype(BF16)
    lo = (x - hi.astype(F32)).astype(BF16)
    return hi, lo


def _dot(a, b):
    return jnp.dot(a, b, preferred_element_type=F32)


def _dot_nt(a, b):
    return lax.dot_general(a, b, _NT, preferred_element_type=F32)


def _mod_kernel(c_ref, w_ref, b_ref, o_ref):
    c = c_ref[...]
    a_hi, a_lo = _split_bf16(c * jax.nn.sigmoid(c))
    w_hi, w_lo = _split_bf16(w_ref[...])
    o_ref[...] = _dot(a_hi, w_hi) + _dot(a_lo, w_hi) + _dot(a_hi, w_lo) + b_ref[...]


def _mod_call(c, w_mod, b_mod):
    bsz, d = c.shape
    n = w_mod.shape[1]
    tn = 1024
    return pl.pallas_call(
        _mod_kernel,
        grid=(n // tn,),
        in_specs=[
            pl.BlockSpec((bsz, d), lambda i: (0, 0)),
            pl.BlockSpec((d, tn), lambda i: (0, i)),
            pl.BlockSpec((1, tn), lambda i: (0, i)),
        ],
        out_specs=pl.BlockSpec((bsz, tn), lambda i: (0, i)),
        out_shape=jax.ShapeDtypeStruct((bsz, n), F32),
        compiler_params=pltpu.CompilerParams(vmem_limit_bytes=VMEM_LIMIT),
        name="mod",
    )(c, w_mod, b_mod.reshape(1, n))


def _rope_partner_64(x):
    lane = lax.broadcasted_iota(I32, x.shape, 1)
    return jnp.where((lane % IDX_DIM) < IDX_DIM // 2,
                     pltpu.roll(x, LANES - IDX_DIM // 2, 1), pltpu.roll(x, IDX_DIM // 2, 1))


def _proj_kernel(x_ref, mod_ref, gmix_ref, wqk_ref, wvt_ref, wg_ref, wi_ref, gains_ref,
                 cosh_ref, sinh_ref, cosi_ref, sini_ref,
                 qa_ref, ka_ref, qb_ref, kb_ref, vat_ref, vbt_ref, ga_ref, gb_ref, a_ref, kc_ref, kw_ref, kmean_ref):
    for blk in range(x_ref.shape[1] // PROJ_ROWS):
        _proj_rows(blk, x_ref, mod_ref, gmix_ref, wqk_ref, wvt_ref, wg_ref, wi_ref, gains_ref,
                   cosh_ref, sinh_ref, cosi_ref, sini_ref,
                   qa_ref, ka_ref, qb_ref, kb_ref, vat_ref, vbt_ref, ga_ref, gb_ref, a_ref, kc_ref, kw_ref, kmean_ref)


def _proj_rows(blk, x_ref, mod_ref, gmix_ref, wqk_ref, wvt_ref, wg_ref, wi_ref, gains_ref,
               cosh_ref, sinh_ref, cosi_ref, sini_ref,
               qa_ref, ka_ref, qb_ref, kb_ref, vat_ref, vbt_ref, ga_ref, gb_ref, a_ref, kc_ref, kw_ref, kmean_ref):
    rows = slice(blk * PROJ_ROWS, (blk + 1) * PROJ_ROWS)
    x = x_ref[0, rows, :]
    y = x * lax.rsqrt(jnp.mean(x * x, axis=-1, keepdims=True) + EPS) * gmix_ref[...]
    h = y * (1.0 + mod_ref[0, 1:2, :]) + mod_ref[0, 0:1, :]
    h_hi = h.astype(BF16)
    cosh, sinh = cosh_ref[rows, :], sinh_ref[rows, :]
    scale = HEAD_DIM ** -0.5 * LOG2_E

    for gi, out_ref in enumerate((qa_ref, ka_ref, qb_ref, kb_ref)):
        p = _dot(h_hi, wqk_ref[:, gi * BR_W:(gi + 1) * BR_W])
        for hh in range(N_HEADS):
            ph = p[:, hh * HEAD_DIM:(hh + 1) * HEAD_DIM]
            yh = ph * lax.rsqrt(jnp.mean(ph * ph, axis=-1, keepdims=True) + EPS) * gains_ref[gi:gi + 1, :]
            r = yh * cosh + pltpu.roll(yh, HEAD_DIM // 2, 1) * sinh
            if gi == 3:
                kmean_ref[0, blk, :, hh * HEAD_DIM:(hh + 1) * HEAD_DIM] = jnp.mean(r, axis=0, keepdims=True)
            if gi % 2 == 0:
                r = r * scale
            out_ref[0, rows, hh * HEAD_DIM:(hh + 1) * HEAD_DIM] = r.astype(BF16)

    vt = _dot_nt(wvt_ref[...], h_hi)
    ones = jnp.ones((VT_ROWS - HEAD_DIM, TQ), BF16)
    for gi, out_ref in enumerate((vat_ref, vbt_ref)):
        for hh in range(N_HEADS):
            src = gi * BR_W + hh * HEAD_DIM
            out_ref[0, blk, hh * VT_ROWS:hh * VT_ROWS + HEAD_DIM, :] = vt[src:src + HEAD_DIM, :].astype(BF16)
            out_ref[0, blk, hh * VT_ROWS + HEAD_DIM:(hh + 1) * VT_ROWS, :] = ones

    for gi, out_ref in enumerate((ga_ref, gb_ref)):
        g = _dot(h_hi, wg_ref[:, gi * D_MODEL:(gi + 1) * D_MODEL])
        out_ref[0, rows, :] = jax.nn.sigmoid(g).astype(BF16)

    pi = _dot(h_hi, wi_ref[...])
    cosi, sini = cosi_ref[rows, :], sini_ref[rows, :]
    first = lax.broadcasted_iota(I32, (PROJ_ROWS, LANES), 1) < IDX_DIM
    half = LANES // 2

    def hi_lo(v):
        hi = v.astype(BF16).astype(F32)
        return hi, v - hi

    n_q = IDX_HEADS * IDX_DIM // LANES
    for g4 in range(n_q):
        xg = pi[:, g4 * LANES:(g4 + 1) * LANES]
        hi, lo = hi_lo(xg * cosi + _rope_partner_64(xg) * sini)
        hi_r, lo_r = pltpu.roll(hi, half, 1), pltpu.roll(lo, half, 1)
        for head, parts in ((2 * g4, (jnp.where(first, hi, lo_r), jnp.where(first, hi, 0.0))),
                            (2 * g4 + 1, (jnp.where(first, hi_r, lo), jnp.where(first, hi_r, 0.0)))):
            for pi_, part in enumerate(parts):
                a_ref[0, rows, head * IDX_K + pi_ * LANES:head * IDX_K + (pi_ + 1) * LANES] = part.astype(BF16)
    xg = pi[:, n_q * LANES:(n_q + 1) * LANES]
    roped = xg * cosi + _rope_partner_64(xg) * sini
    hi, lo = hi_lo(roped)
    kc_ref[0, rows, :LANES] = jnp.where(first, hi, pltpu.roll(hi, half, 1)).astype(BF16)
    kc_ref[0, rows, LANES:] = jnp.where(first, lo, 0.0).astype(BF16)
    w_scale = (IDX_HEADS ** -0.5) * (IDX_DIM ** -0.5)
    kw_ref[0, rows, :] = jnp.where(first, roped, xg * w_scale)


def _proj_call(x, mod3, gmix, wqk, wvt, wg, wi, gains, cosh, sinh, cosi, sini):
    bsz, seq, d = x.shape
    tm = TM_PROJ
    nb = seq // MOBA_BLOCK
    const = lambda b, i: (0, 0)
    tok = lambda b, i: (b, i, 0)
    tab = lambda b, i: (i, 0)
    resident = lambda a: pl.BlockSpec(a.shape, const, pipeline_mode=pl.Buffered(1))
    out_shape = (
        [jax.ShapeDtypeStruct((bsz, seq, BR_W), BF16)] * 4
        + [jax.ShapeDtypeStruct((bsz, seq // TQ, N_HEADS * VT_ROWS, TQ), BF16)] * 2
        + [jax.ShapeDtypeStruct((bsz, seq, D_MODEL), BF16)] * 2
        + [jax.ShapeDtypeStruct((bsz, seq, IDX_HEADS * IDX_K), BF16),
           jax.ShapeDtypeStruct((bsz, seq, IDX_K), BF16),
           jax.ShapeDtypeStruct((bsz, seq, LANES), F32),
           jax.ShapeDtypeStruct((bsz, nb, 1, BR_W), F32)]
    )
    out_specs = (
        [pl.BlockSpec((1, tm, BR_W), tok)] * 4
        + [pl.BlockSpec((1, tm // TQ, N_HEADS * VT_ROWS, TQ), lambda b, i: (b, i, 0, 0))] * 2
        + [pl.BlockSpec((1, tm, D_MODEL), tok)] * 2
        + [pl.BlockSpec((1, tm, IDX_HEADS * IDX_K), tok),
           pl.BlockSpec((1, tm, IDX_K), tok),
           pl.BlockSpec((1, tm, LANES), tok),
           pl.BlockSpec((1, tm // MOBA_BLOCK, 1, BR_W), lambda b, i: (b, i, 0, 0))]
    )
    return pl.pallas_call(
        _proj_kernel,
        grid=(bsz, seq // tm),
        in_specs=[
            pl.BlockSpec((1, tm, d), tok),
            pl.BlockSpec((1, N_MOD, d), lambda b, i: (b, 0, 0)),
            pl.BlockSpec((1, d), const),
            resident(wqk), resident(wvt), resident(wg), resident(wi),
            pl.BlockSpec(gains.shape, const),
            pl.BlockSpec((tm, LANES), tab),
            pl.BlockSpec((tm, LANES), tab),
            pl.BlockSpec((tm, LANES), tab),
            pl.BlockSpec((tm, LANES), tab),
        ],
        out_specs=out_specs,
        out_shape=out_shape,
        compiler_params=pltpu.CompilerParams(
            dimension_semantics=("arbitrary", "arbitrary"), vmem_limit_bytes=VMEM_LIMIT),
        name="proj",
    )(x, mod3, gmix, wqk, wvt, wg, wi, gains, cosh, sinh, cosi, sini)


def _tile3(x):
    return x.reshape(x.shape[0] // SUBLANES, SUBLANES, x.shape[1])


def _allreduce_sublanes(x, op):
    for shift in (4, 2, 1):
        x = op(x, pltpu.roll(x, shift, 0))
    return x


def _softmax_tile(s3, vt, m_old, acc_ref, hh):
    rows = slice(hh * VT_ROWS, (hh + 1) * VT_ROWS)
    m_new = jnp.maximum(m_old, _allreduce_sublanes(jnp.max(s3, axis=0), jnp.maximum))
    alpha = jnp.exp2(m_old - m_new)
    p = jnp.exp2(s3 - m_new[None]).reshape(s3.shape[0] * SUBLANES, TQ).astype(BF16)
    acc_ref[rows, :] = (_tile3(acc_ref[rows, :]) * alpha[None]).reshape(VT_ROWS, TQ) + _dot(vt, p)
    return m_new


def _qk_tile(k_ref, q_ref, t, s_ref):
    k0 = pl.multiple_of(t * TQ, TQ)
    for hh, hs in enumerate(_HEAD_SLICES):
        s_ref[hh] = _dot_nt(k_ref[0, pl.ds(k0, TQ), hs], q_ref[0, :, hs])


def _causal_triangle():
    shape = (TQ // SUBLANES, SUBLANES, TQ)
    row = lax.broadcasted_iota(I32, shape, 0) * SUBLANES + lax.broadcasted_iota(I32, shape, 1)
    return row <= lax.broadcasted_iota(I32, shape, 2)


def _edge_mask(j, t_true):
    shape = (TQ // SUBLANES, SUBLANES, TQ)
    row = lax.broadcasted_iota(I32, shape, 0) * SUBLANES + lax.broadcasted_iota(I32, shape, 1)
    return row - lax.broadcasted_iota(I32, shape, 2) <= (j - t_true) * TQ


def _sweep_tiles(j, s_refs, qk, consume, state):
    s0, s1 = s_refs
    last_pair = j // 2
    qk(0, s0)

    def body(p, st):
        t0 = 2 * p
        qk(t0 + 1, s1)
        st = consume(t0, t0, s0, st, False)
        qk(t0 + 2, s0)
        return consume(t0 + 1, t0 + 1, s1, st, False)

    state = lax.fori_loop(0, last_pair, body, state)
    ta = 2 * last_pair
    tb = jnp.minimum(ta + 1, j)
    qk(tb, s1)
    state = consume(ta, ta, s0, state, True)
    return consume(tb, ta + 1, s1, state, True)


def _init_state(acc_ref):
    acc_ref[...] = jnp.zeros(acc_ref.shape, F32)
    return (jnp.full((SUBLANES, TQ), M_INIT, F32),) * N_HEADS


def _write_heads(o_ref, acc_ref):
    for hh, hs in enumerate(_HEAD_SLICES):
        base = hh * VT_ROWS
        inv = 1.0 / acc_ref[base + HEAD_DIM:base + HEAD_DIM + SUBLANES, :]
        o_t = (_tile3(acc_ref[base:base + HEAD_DIM, :]) * inv[None]).reshape(HEAD_DIM, TQ)
        o_ref[0, :, hs] = o_t.T.astype(BF16)


def _sortable_key(score):
    bits = lax.bitcast_convert_type(score, I32)
    key = bits ^ ((bits >> 31) & jnp.int32(0x7FFFFFFF))
    return jnp.where(key == -1, 0, key)


def _dsa_kernel(a_ref, kc_ref, wk_ref, qa_ref, ka_ref, vat_ref, o_ref,
                keys_ref, hi_ref, lo_ref, lo_sel_ref, s0_ref, s1_ref, acc_ref, *, topk, seq_bits):
    j = pl.program_id(1)
    n_tiles = j + 1
    w_rows = wk_ref[0].T[IDX_DIM:IDX_DIM + IDX_HEADS, :]
    w8 = [jnp.broadcast_to(w_rows[hh:hh + 1, :], (SUBLANES, TQ)) for hh in range(IDX_HEADS)]

    def score_tile(c, diagonal):
        k0 = pl.multiple_of(c * TQ, TQ)
        kc = kc_ref[0, pl.ds(k0, TQ), :]
        acc = jnp.zeros((TQ // SUBLANES, SUBLANES, TQ), F32)
        for hh in range(IDX_HEADS):
            lg = _dot_nt(kc, a_ref[0, :, hh * IDX_K:(hh + 1) * IDX_K])
            acc = acc + w8[hh][None] * jnp.maximum(_tile3(lg), 0.0)
        if diagonal:
            acc = jnp.where(_causal_triangle(), acc, NEG)
        key = _sortable_key(acc.reshape(TQ, TQ))
        keys_ref[pl.ds(k0, TQ), :] = key
        hi_ref[pl.ds(k0, TQ), :] = (key >> 16).astype(I16)
        lo_ref[pl.ds(k0, TQ), :] = ((key & 0xFFFF) + I16_MIN).astype(I16)

    def score_pair(p, carry):
        score_tile(2 * p, False)
        score_tile(2 * p + 1, False)
        return carry

    lax.fori_loop(0, j // 2, score_pair, 0)

    @pl.when(j % 2 == 1)
    def _():
        score_tile(j - 1, False)

    score_tile(j, True)

    def count(pred):
        def body(c, cnt):
            for u in range(TQ // SEL_ROWS):
                r0 = pl.multiple_of(c * TQ + u * SEL_ROWS, SEL_ROWS)
                cnt = cnt + jnp.where(pred(keys_ref[pl.ds(r0, SEL_ROWS), :], r0), 1.0, 0.0)
            return cnt
        cnt = lax.fori_loop(0, n_tiles, body, jnp.zeros((SEL_ROWS, TQ), F32))
        return jnp.sum(cnt, axis=0, keepdims=True)

    def packed(v32):
        return jnp.broadcast_to(v32, (PACK_ROWS, TQ)).astype(I16)

    def load16(ref, c, u):
        r0 = pl.multiple_of(c * TQ + u * SEL_ROWS, SEL_ROWS)
        return ref[pl.ds(r0, SEL_ROWS), :].reshape(SEL_ROWS // PACK_ROWS, PACK_ROWS, TQ)

    def reduce16(cnt):
        return jnp.sum(jnp.sum(cnt.astype(F32), axis=0), axis=0, keepdims=True)

    cnt16_zero = jnp.zeros((SEL_ROWS // PACK_ROWS, PACK_ROWS, TQ), I16)
    one16, zero16 = jnp.int16(1), jnp.int16(0)

    def count16(ref, pred):
        def tile(c, cnt):
            for u in range(TQ // SEL_ROWS):
                cnt = cnt + jnp.where(pred(load16(ref, c, u)), one16, zero16)
            return cnt

        cnt = lax.fori_loop(0, n_tiles // 2, lambda p, cnt: tile(2 * p + 1, tile(2 * p, cnt)), cnt16_zero)
        return reduce16(lax.fori_loop(2 * (n_tiles // 2), n_tiles, tile, cnt))

    def radix16(ref, k_needed, count_all):
        c0 = count16(ref, lambda ch: ch >= zero16)
        ok0 = c0 >= k_needed
        start = (jnp.where(ok0, 0, I16_MIN).astype(I32), jnp.where(ok0, c0, count_all), jnp.where(ok0, 0.0, c0))

        def bit_step(i, carry):
            val, cge, cgt = carry
            test = val | (jnp.int32(1) << (14 - i))
            t16 = packed(test)[None]
            c = count16(ref, lambda ch: ch >= t16)
            ok = c >= k_needed
            return jnp.where(ok, test, val), jnp.where(ok, c, cge), jnp.where(ok, cgt, c)

        return lax.fori_loop(0, 15, bit_step, start)

    kf = float(topk)
    hi_k, _, above_hi = radix16(hi_ref, kf, (n_tiles * TQ).astype(F32))
    hi_k16 = packed(hi_k)[None]
    need_lo = kf - above_hi

    def build_lo(c, cnt):
        for u in range(TQ // SEL_ROWS):
            r0 = pl.multiple_of(c * TQ + u * SEL_ROWS, SEL_ROWS)
            tie = load16(hi_ref, c, u) == hi_k16
            lo_sel_ref[pl.ds(r0, SEL_ROWS), :] = jnp.where(tie, load16(lo_ref, c, u), jnp.int16(I16_MIN)).reshape(
                SEL_ROWS, TQ)
            cnt = cnt + jnp.where(tie, one16, zero16)
        return cnt

    n_tie_hi = reduce16(lax.fori_loop(0, n_tiles, build_lo, cnt16_zero))
    lo_k, c_lo, _ = radix16(lo_sel_ref, need_lo, n_tie_hi)
    cand = (hi_k << 16) + (lo_k - I16_MIN)
    cge = kf - need_lo + c_lo

    @pl.when(jnp.max(jnp.abs(cge - kf)) > 0.0)
    def _():
        need = kf - count(lambda kk, r0: kk > cand)
        sub = lax.broadcasted_iota(I32, (SEL_ROWS, TQ), 0)

        def cut_step(i, cut):
            test = cut | (jnp.int32(1) << (seq_bits - 1 - i))
            f = count(lambda kk, r0: (kk == cand) & (r0 + sub < test))
            return jnp.where(f < need, test, cut)

        cut = lax.fori_loop(0, seq_bits, cut_step, jnp.zeros((1, TQ), I32))

        def demote(c, carry):
            r0 = pl.multiple_of(c * SEL_ROWS, SEL_ROWS)
            kk = keys_ref[pl.ds(r0, SEL_ROWS), :]
            keys_ref[pl.ds(r0, SEL_ROWS), :] = jnp.where((kk == cand) & (r0 + sub > cut), kk - 1, kk)
            return carry

        lax.fori_loop(0, n_tiles * (TQ // SEL_ROWS), demote, 0)

    def consume(t, t_true, s_ref, state, edge):
        k0 = pl.multiple_of(t * TQ, TQ)
        sel = _tile3(keys_ref[pl.ds(k0, TQ), :]) >= cand[None]
        if edge:
            sel = sel & _edge_mask(j, t_true)
        bias = jnp.where(sel, 0.0, NEG)
        return tuple(
            _softmax_tile(_tile3(s_ref[hh]) + bias, vat_ref[0, t, hh * VT_ROWS:(hh + 1) * VT_ROWS, :],
                          state[hh], acc_ref, hh)
            for hh in range(N_HEADS))

    qk = functools.partial(_qk_tile, ka_ref, qa_ref)
    _sweep_tiles(j, (s0_ref, s1_ref), qk, consume, _init_state(acc_ref))
    _write_heads(o_ref, acc_ref)


def _dsa_call(a_mat, kc, idx_out, qa, ka, vat):
    bsz, seq, _ = qa.shape
    topk = min(DSA_TOPK_MAX, seq // 4)
    seq_bits = (seq - 1).bit_length()
    kern = functools.partial(_dsa_kernel, topk=topk, seq_bits=seq_bits)
    return pl.pallas_call(
        kern,
        grid=(bsz, seq // TQ),
        in_specs=[
            pl.BlockSpec((1, TQ, IDX_HEADS * IDX_K), lambda b, j: (b, j, 0)),
            pl.BlockSpec((1, seq, IDX_K), lambda b, j: (b, 0, 0)),
            pl.BlockSpec((1, TQ, LANES), lambda b, j: (b, j, 0)),
            pl.BlockSpec((1, TQ, BR_W), lambda b, j: (b, j, 0)),
            pl.BlockSpec((1, seq, BR_W), lambda b, j: (b, 0, 0)),
            pl.BlockSpec((1, seq // TQ, N_HEADS * VT_ROWS, TQ), lambda b, j: (b, 0, 0, 0)),
        ],
        out_specs=pl.BlockSpec((1, TQ, BR_W), lambda b, j: (b, j, 0)),
        out_shape=jax.ShapeDtypeStruct((bsz, seq, BR_W), BF16),
        scratch_shapes=[
            pltpu.VMEM((seq, TQ), I32),
            pltpu.VMEM((seq, TQ), I16),
            pltpu.VMEM((seq, TQ), I16),
            pltpu.VMEM((seq, TQ), I16),
            pltpu.VMEM((N_HEADS, TQ, TQ), F32),
            pltpu.VMEM((N_HEADS, TQ, TQ), F32),
            pltpu.VMEM((N_HEADS * VT_ROWS, TQ), F32),
        ],
        compiler_params=pltpu.CompilerParams(
            dimension_semantics=("arbitrary", "arbitrary"), vmem_limit_bytes=VMEM_LIMIT),
        name="dsa",
    )(a_mat, kc, idx_out, qa, ka, vat)


def _moba_kernel(qb_ref, kb_ref, vbt_ref, kmean_ref, o_ref, bias_ref, s0_ref, s1_ref, acc_ref, *, n_sel):
    j = pl.program_id(1)
    nb = kmean_ref.shape[1]
    blk = lax.broadcasted_iota(I32, (nb, TQ), 0).astype(F32)
    jf = j.astype(F32)

    for hh in range(N_HEADS):
        hs = slice(hh * HEAD_DIM, (hh + 1) * HEAD_DIM)
        km_hi, km_lo = _split_bf16(kmean_ref[0, :, hs])
        q = qb_ref[0, :, hs]
        gate = jnp.where(blk < jf, _dot_nt(km_hi, q) + _dot_nt(km_lo, q), NEG)
        chosen = jnp.zeros((nb, TQ), F32)
        for _ in range(n_sel):
            best = jnp.max(gate, axis=0, keepdims=True)
            first = jnp.min(jnp.where(gate == best, blk, float(nb)), axis=0, keepdims=True)
            hit = blk == first
            chosen = jnp.where(hit, 1.0, chosen)
            gate = jnp.where(hit, -jnp.inf, gate)
        bias = jnp.where((chosen > 0.0) & (blk < jf), 0.0, NEG)
        for n in range(nb):
            bias_ref[hh, n] = jnp.broadcast_to(bias[n:n + 1, :], (SUBLANES, TQ))

    def consume(t, t_true, s_ref, state, edge):
        new = []
        for hh in range(N_HEADS):
            bias = bias_ref[hh, t]
            if edge:
                bias = jnp.where(t_true < j, bias, 0.0)
            s3 = _tile3(s_ref[hh]) + bias[None]
            if edge:
                s3 = jnp.where(_edge_mask(j, t_true), s3, NEG)
            new.append(_softmax_tile(s3, vbt_ref[0, t, hh * VT_ROWS:(hh + 1) * VT_ROWS, :], state[hh], acc_ref, hh))
        return tuple(new)

    qk = functools.partial(_qk_tile, kb_ref, qb_ref)
    _sweep_tiles(j, (s0_ref, s1_ref), qk, consume, _init_state(acc_ref))
    _write_heads(o_ref, acc_ref)


def _moba_call(qb, kb, vbt, kmean):
    bsz, seq, _ = qb.shape
    nb = seq // MOBA_BLOCK
    kern = functools.partial(_moba_kernel, n_sel=min(MOBA_TOPK, nb - 1))
    return pl.pallas_call(
        kern,
        grid=(bsz, seq // TQ),
        in_specs=[
            pl.BlockSpec((1, TQ, BR_W), lambda b, j: (b, j, 0)),
            pl.BlockSpec((1, seq, BR_W), lambda b, j: (b, 0, 0)),
            pl.BlockSpec((1, seq // TQ, N_HEADS * VT_ROWS, TQ), lambda b, j: (b, 0, 0, 0)),
            pl.BlockSpec((1, nb, BR_W), lambda b, j: (b, 0, 0)),
        ],
        out_specs=pl.BlockSpec((1, TQ, BR_W), lambda b, j: (b, j, 0)),
        out_shape=jax.ShapeDtypeStruct((bsz, seq, BR_W), BF16),
        scratch_shapes=[
            pltpu.VMEM((N_HEADS, nb, SUBLANES, TQ), F32),
            pltpu.VMEM((N_HEADS, TQ, TQ), F32),
            pltpu.VMEM((N_HEADS, TQ, TQ), F32),
            pltpu.VMEM((N_HEADS * VT_ROWS, TQ), F32),
        ],
        compiler_params=pltpu.CompilerParams(
            dimension_semantics=("arbitrary", "arbitrary"), vmem_limit_bytes=VMEM_LIMIT),
        name="moba",
    )(qb, kb, vbt, kmean)


def _out_ffn_kernel(x_ref, oa_ref, ob_ref, ga_ref, gb_ref, mod_ref, gffn_ref,
                    wba_ref, wbb_ref, wo_ref, wgu_ref, wd_ref, o_ref, act_ref):
    merged = (ga_ref[0].astype(F32) * _dot(oa_ref[0], wba_ref[...])
              + gb_ref[0].astype(F32) * _dot(ob_ref[0], wbb_ref[...]))
    x1 = x_ref[0] + mod_ref[0, 2:3, :] * _dot(merged.astype(BF16), wo_ref[...])
    y = x1 * lax.rsqrt(jnp.mean(x1 * x1, axis=-1, keepdims=True) + EPS) * gffn_ref[...]
    h = (y * (1.0 + mod_ref[0, 4:5, :]) + mod_ref[0, 3:4, :]).astype(BF16)
    for c0 in range(0, D_FF, FF_CHUNK):
        g = _dot(h, wgu_ref[:, c0:c0 + FF_CHUNK])
        u = _dot(h, wgu_ref[:, D_FF + c0:D_FF + c0 + FF_CHUNK])
        act_ref[:, c0:c0 + FF_CHUNK] = (g * jax.nn.sigmoid(g) * u).astype(BF16)
    o_ref[0] = x1 + mod_ref[0, 5:6, :] * _dot(act_ref[...], wd_ref[...])


def _out_ffn_call(x, oa, ob, ga, gb, mod3, gffn, wba, wbb, wo, wgu, wd):
    bsz, seq, d = x.shape
    tm = TM_FFN
    const = lambda b, i: (0, 0)
    tok = lambda b, i: (b, i, 0)
    resident = lambda a: pl.BlockSpec(a.shape, const, pipeline_mode=pl.Buffered(1))
    return pl.pallas_call(
        _out_ffn_kernel,
        grid=(bsz, seq // tm),
        in_specs=[
            pl.BlockSpec((1, tm, d), tok),
            pl.BlockSpec((1, tm, BR_W), tok),
            pl.BlockSpec((1, tm, BR_W), tok),
            pl.BlockSpec((1, tm, d), tok),
            pl.BlockSpec((1, tm, d), tok),
            pl.BlockSpec((1, N_MOD, d), lambda b, i: (b, 0, 0)),
            pl.BlockSpec((1, d), const),
            resident(wba), resident(wbb), resident(wo), resident(wgu), resident(wd),
        ],
        out_specs=pl.BlockSpec((1, tm, d), tok),
        out_shape=jax.ShapeDtypeStruct((bsz, seq, d), F32),
        scratch_shapes=[pltpu.VMEM((tm, D_FF), BF16)],
        compiler_params=pltpu.CompilerParams(
            dimension_semantics=("arbitrary", "arbitrary"), vmem_limit_bytes=VMEM_LIMIT),
        name="out_ffn",
    )(x, oa, ob, ga, gb, mod3, gffn, wba, wbb, wo, wgu, wd)


def _rope_tables(seq, dim):
    inv = ROPE_THETA ** (-jnp.arange(0, dim, 2, dtype=F32) / dim)
    ang = jnp.arange(seq, dtype=F32)[:, None] * inv[None, :]
    cos, sin = jnp.cos(ang), jnp.sin(ang)
    reps = LANES // dim
    return (jnp.tile(jnp.concatenate([cos, cos], axis=-1), (1, reps)),
            jnp.tile(jnp.concatenate([-sin, sin], axis=-1), (1, reps)))


def _layer(x, mod, g_mix, g_ffn, w_in, g_qa, g_ka, g_qb, g_kb, w_br_a, w_br_b, w_out, w_gu, w_down):
    bsz, seq, d = x.shape
    mod3 = mod.reshape(bsz, N_MOD, d)

    o = 0
    cols = {}
    for name, width in (("qa", BR_W), ("ka", BR_W), ("va", BR_W), ("qi", IDX_HEADS * IDX_DIM),
                        ("ki", IDX_DIM), ("wi", IDX_HEADS), ("qb", BR_W), ("kb", BR_W), ("vb", BR_W),
                        ("ga", D_MODEL), ("gb", D_MODEL)):
        cols[name] = w_in[:, o:o + width]
        o += width
    wqk = jnp.concatenate([cols["qa"], cols["ka"], cols["qb"], cols["kb"]], axis=1).astype(BF16)
    wvt = jnp.concatenate([cols["va"], cols["vb"]], axis=1).T.astype(BF16)
    wg = jnp.concatenate([cols["ga"], cols["gb"]], axis=1).astype(BF16)
    pad = IDX_OUT_W - IDX_HEADS * IDX_DIM - IDX_DIM - IDX_HEADS
    wi = jnp.concatenate([cols["qi"], cols["ki"], cols["wi"], jnp.zeros((d, pad), F32)], axis=1).astype(BF16)
    gains = jnp.stack([g_qa, g_ka, g_qb, g_kb])
    cosh, sinh = _rope_tables(seq, HEAD_DIM)
    cosi, sini = _rope_tables(seq, IDX_DIM)

    qa, ka, qb, kb, vat, vbt, ga, gb, a_mat, kc, kw, kmean = _proj_call(
        x, mod3, g_mix.reshape(1, d), wqk, wvt, wg, wi, gains, cosh, sinh, cosi, sini)

    oa = _dsa_call(a_mat, kc, kw, qa, ka, vat)
    ob = _moba_call(qb, kb, vbt, kmean.reshape(bsz, seq // MOBA_BLOCK, BR_W))

    return _out_ffn_call(x, oa, ob, ga, gb, mod3, g_ffn.reshape(1, d),
                         w_br_a.astype(BF16), w_br_b.astype(BF16), w_out.astype(BF16),
                         w_gu.astype(BF16), w_down.astype(BF16))


def kernel(x, c, w_mod, b_mod, g_mix_norm, g_ffn_norm, w_in, g_q_dsa, g_k_dsa, g_q_moba, g_k_moba,
           w_br_dsa, w_br_moba, w_out, w_gate_up, w_down):
    assert x.shape[1] % TM_PROJ == 0 and x.shape[2] == D_MODEL and w_gate_up.shape[2] == 2 * D_FF
    for l in range(w_mod.shape[0]):
        mod = _mod_call(c, w_mod[l], b_mod[l])
        x = _layer(x, mod, g_mix_norm[l], g_ffn_norm[l], w_in[l], g_q_dsa[l], g_k_dsa[l], g_q_moba[l],
                   g_k_moba[l], w_br_dsa[l], w_br_moba[l], w_out[l], w_gate_up[l], w_down[l])
    return x
```

```python
import functools

import jax
import jax.numpy as jnp
from jax import lax
from jax.experimental import pallas as pl
from jax.experimental.pallas import tpu as pltpu

F32 = jnp.float32
BF16 = jnp.bfloat16
I32 = jnp.int32
I16 = jnp.int16

D_MODEL = 1024
HEAD_DIM = 128
N_HEADS = 4
BR_W = N_HEADS * HEAD_DIM
IDX_HEADS = 8
IDX_DIM = 64
DSA_TOPK_MAX = 256
MOBA_BLOCK = 256
MOBA_TOPK = 3
D_FF = 2816
ROPE_THETA = 10000.0
EPS = 1e-6
NEG = -1e30
LOG2_E = 1.4426950408889634
M_INIT = -1e29
N_MOD = 6
IDX_OUT_W = 640
IDX_K = 256
I16_MIN = -(2 ** 15)
BF16_KEY_NEG_INF = (0xFF80 ^ 0x7FFF) - 0x10000
BF16_KEY_POS_INF = 0x7F80
BISECT_STEPS = 18

LANES = 128
SUBLANES = 8
PACK_ROWS = 16
VT_ROWS = HEAD_DIM + PACK_ROWS
TQ = 256
TM_PROJ = 512
PROJ_ROWS = 256
TM_FFN = 512
FF_CHUNK = 256
SEL_ROWS = 32
SCORE_GROUP = 4
VMEM_LIMIT = 56 * 1024 * 1024

_NT = (((1,), (1,)), ((), ()))
_HEAD_SLICES = tuple(slice(h * HEAD_DIM, (h + 1) * HEAD_DIM) for h in range(N_HEADS))


def _split_bf16(x):
    hi = x.astype(BF16)
    lo = (x - hi.astype(F32)).astype(BF16)
    return hi, lo


def _dot(a, b):
    return jnp.dot(a, b, preferred_element_type=F32)


def _dot_nt(a, b):
    return lax.dot_general(a, b, _NT, preferred_element_type=F32)


def _mod_kernel(c_ref, w_ref, b_ref, o_ref):
    c = c_ref[...]
    a_hi, a_lo = _split_bf16(c * jax.nn.sigmoid(c))
    w_hi, w_lo = _split_bf16(w_ref[...])
    o_ref[...] = _dot(a_hi, w_hi) + _dot(a_lo, w_hi) + _dot(a_hi, w_lo) + b_ref[...]


def _mod_call(c, w_mod, b_mod):
    bsz, d = c.shape
    n = w_mod.shape[1]
    tn = 1024
    return pl.pallas_call(
        _mod_kernel,
        grid=(n // tn,),
        in_specs=[
            pl.BlockSpec((bsz, d), lambda i: (0, 0)),
            pl.BlockSpec((d, tn), lambda i: (0, i)),
            pl.BlockSpec((1, tn), lambda i: (0, i)),
        ],
        out_specs=pl.BlockSpec((bsz, tn), lambda i: (0, i)),
        out_shape=jax.ShapeDtypeStruct((bsz, n), F32),
        compiler_params=pltpu.CompilerParams(vmem_limit_bytes=VMEM_LIMIT),
        name="mod",
    )(c, w_mod, b_mod.reshape(1, n))


def _rope_partner_64(x):
    lane = lax.broadcasted_iota(I32, x.shape, 1)
    return jnp.where((lane % IDX_DIM) < IDX_DIM // 2,
                     pltpu.roll(x, LANES - IDX_DIM // 2, 1), pltpu.roll(x, IDX_DIM // 2, 1))


def _proj_kernel(x_ref, mod_ref, gmix_ref, wqk_ref, wvt_ref, wg_ref, wi_ref, gains_ref,
                 cosh_ref, sinh_ref, cosi_ref, sini_ref,
                 qa_ref, ka_ref, qb_ref, kb_ref, vat_ref, vbt_ref, ga_ref, gb_ref, a_ref, kc_ref, kw_ref, kmean_ref):
    for blk in range(x_ref.shape[1] // PROJ_ROWS):
        _proj_rows(blk, x_ref, mod_ref, gmix_ref, wqk_ref, wvt_ref, wg_ref, wi_ref, gains_ref,
                   cosh_ref, sinh_ref, cosi_ref, sini_ref,
                   qa_ref, ka_ref, qb_ref, kb_ref, vat_ref, vbt_ref, ga_ref, gb_ref, a_ref, kc_ref, kw_ref, kmean_ref)


def _proj_rows(blk, x_ref, mod_ref, gmix_ref, wqk_ref, wvt_ref, wg_ref, wi_ref, gains_ref,
               cosh_ref, sinh_ref, cosi_ref, sini_ref,
               qa_ref, ka_ref, qb_ref, kb_ref, vat_ref, vbt_ref, ga_ref, gb_ref, a_ref, kc_ref, kw_ref, kmean_ref):
    rows = slice(blk * PROJ_ROWS, (blk + 1) * PROJ_ROWS)
    x = x_ref[0, rows, :]
    y = x * lax.rsqrt(jnp.mean(x * x, axis=-1, keepdims=True) + EPS) * gmix_ref[...]
    h = y * (1.0 + mod_ref[0, 1:2, :]) + mod_ref[0, 0:1, :]
    h_hi = h.astype(BF16)
    cosh, sinh = cosh_ref[rows, :], sinh_ref[rows, :]
    scale = HEAD_DIM ** -0.5 * LOG2_E

    for gi, out_ref in enumerate((qa_ref, ka_ref, qb_ref, kb_ref)):
        p = _dot(h_hi, wqk_ref[:, gi * BR_W:(gi + 1) * BR_W])
        for hh in range(N_HEADS):
            ph = p[:, hh * HEAD_DIM:(hh + 1) * HEAD_DIM]
            yh = ph * lax.rsqrt(jnp.mean(ph * ph, axis=-1, keepdims=True) + EPS) * gains_ref[gi:gi + 1, :]
            r = yh * cosh + pltpu.roll(yh, HEAD_DIM // 2, 1) * sinh
            if gi == 3:
                kmean_ref[0, blk, :, hh * HEAD_DIM:(hh + 1) * HEAD_DIM] = jnp.mean(r, axis=0, keepdims=True)
            if gi % 2 == 0:
                r = r * scale
            out_ref[0, rows, hh * HEAD_DIM:(hh + 1) * HEAD_DIM] = r.astype(BF16)

    vt = _dot_nt(wvt_ref[...], h_hi)
    ones = jnp.ones((VT_ROWS - HEAD_DIM, TQ), BF16)
    for gi, out_ref in enumerate((vat_ref, vbt_ref)):
        for hh in range(N_HEADS):
            src = gi * BR_W + hh * HEAD_DIM
            out_ref[0, blk, hh * VT_ROWS:hh * VT_ROWS + HEAD_DIM, :] = vt[src:src + HEAD_DIM, :].astype(BF16)
            out_ref[0, blk, hh * VT_ROWS + HEAD_DIM:(hh + 1) * VT_ROWS, :] = ones

    for gi, out_ref in enumerate((ga_ref, gb_ref)):
        g = _dot(h_hi, wg_ref[:, gi * D_MODEL:(gi + 1) * D_MODEL])
        out_ref[0, rows, :] = jax.nn.sigmoid(g).astype(BF16)

    pi = _dot(h_hi, wi_ref[...])
    cosi, sini = cosi_ref[rows, :], sini_ref[rows, :]
    first = lax.broadcasted_iota(I32, (PROJ_ROWS, LANES), 1) < IDX_DIM
    half = LANES // 2

    def hi_lo(v):
        hi = v.astype(BF16).astype(F32)
        return hi, v - hi

    n_q = IDX_HEADS * IDX_DIM // LANES
    for g4 in range(n_q):
        xg = pi[:, g4 * LANES:(g4 + 1) * LANES]
        hi, lo = hi_lo(xg * cosi + _rope_partner_64(xg) * sini)
        hi_r, lo_r = pltpu.roll(hi, half, 1), pltpu.roll(lo, half, 1)
        for head, parts in ((2 * g4, (jnp.where(first, hi, lo_r), jnp.where(first, hi, 0.0))),
                            (2 * g4 + 1, (jnp.where(first, hi_r, lo), jnp.where(first, hi_r, 0.0)))):
            for pi_, part in enumerate(parts):
                a_ref[0, rows, head * IDX_K + pi_ * LANES:head * IDX_K + (pi_ + 1) * LANES] = part.astype(BF16)
    xg = pi[:, n_q * LANES:(n_q + 1) * LANES]
    roped = xg * cosi + _rope_partner_64(xg) * sini
    hi, lo = hi_lo(roped)
    kc_ref[0, rows, :LANES] = jnp.where(first, hi, pltpu.roll(hi, half, 1)).astype(BF16)
    kc_ref[0, rows, LANES:] = jnp.where(first, lo, 0.0).astype(BF16)
    w_scale = (IDX_HEADS ** -0.5) * (IDX_DIM ** -0.5)
    kw_ref[0, rows, :] = jnp.where(first, roped, xg * w_scale)


def _proj_call(x, mod3, gmix, wqk, wvt, wg, wi, gains, cosh, sinh, cosi, sini):
    bsz, seq, d = x.shape
    tm = TM_PROJ
    nb = seq // MOBA_BLOCK
    const = lambda b, i: (0, 0)
    tok = lambda b, i: (b, i, 0)
    tab = lambda b, i: (i, 0)
    resident = lambda a: pl.BlockSpec(a.shape, const, pipeline_mode=pl.Buffered(1))
    out_shape = (
        [jax.ShapeDtypeStruct((bsz, seq, BR_W), BF16)] * 4
        + [jax.ShapeDtypeStruct((bsz, seq // TQ, N_HEADS * VT_ROWS, TQ), BF16)] * 2
        + [jax.ShapeDtypeStruct((bsz, seq, D_MODEL), BF16)] * 2
        + [jax.ShapeDtypeStruct((bsz, seq, IDX_HEADS * IDX_K), BF16),
           jax.ShapeDtypeStruct((bsz, seq, IDX_K), BF16),
           jax.ShapeDtypeStruct((bsz, seq, LANES), F32),
           jax.ShapeDtypeStruct((bsz, nb, 1, BR_W), F32)]
    )
    out_specs = (
        [pl.BlockSpec((1, tm, BR_W), tok)] * 4
        + [pl.BlockSpec((1, tm // TQ, N_HEADS * VT_ROWS, TQ), lambda b, i: (b, i, 0, 0))] * 2
        + [pl.BlockSpec((1, tm, D_MODEL), tok)] * 2
        + [pl.BlockSpec((1, tm, IDX_HEADS * IDX_K), tok),
           pl.BlockSpec((1, tm, IDX_K), tok),
           pl.BlockSpec((1, tm, LANES), tok),
           pl.BlockSpec((1, tm // MOBA_BLOCK, 1, BR_W), lambda b, i: (b, i, 0, 0))]
    )
    return pl.pallas_call(
        _proj_kernel,
        grid=(bsz, seq // tm),
        in_specs=[
            pl.BlockSpec((1, tm, d), tok),
            pl.BlockSpec((1, N_MOD, d), lambda b, i: (b, 0, 0)),
            pl.BlockSpec((1, d), const),
            resident(wqk), resident(wvt), resident(wg), resident(wi),
            pl.BlockSpec(gains.shape, const),
            pl.BlockSpec((tm, LANES), tab),
            pl.BlockSpec((tm, LANES), tab),
            pl.BlockSpec((tm, LANES), tab),
            pl.BlockSpec((tm, LANES), tab),
        ],
        out_specs=out_specs,
        out_shape=out_shape,
        compiler_params=pltpu.CompilerParams(
            dimension_semantics=("arbitrary", "arbitrary"), vmem_limit_bytes=VMEM_LIMIT),
        name="proj",
    )(x, mod3, gmix, wqk, wvt, wg, wi, gains, cosh, sinh, cosi, sini)


def _tile3(x):
    return x.reshape(x.shape[0] // SUBLANES, SUBLANES, x.shape[1])


def _allreduce_sublanes(x, op):
    for shift in (4, 2, 1):
        x = op(x, pltpu.roll(x, shift, 0))
    return x


def _softmax_tile(s3, vt, m_old, acc_ref, hh):
    rows = slice(hh * VT_ROWS, (hh + 1) * VT_ROWS)
    m_new = jnp.maximum(m_old, _allreduce_sublanes(jnp.max(s3, axis=0), jnp.maximum))
    alpha = jnp.exp2(m_old - m_new)
    p = jnp.exp2(s3 - m_new[None]).reshape(s3.shape[0] * SUBLANES, TQ).astype(BF16)
    acc_ref[rows, :] = (_tile3(acc_ref[rows, :]) * alpha[None]).reshape(VT_ROWS, TQ) + _dot(vt, p)
    return m_new


def _qk_tile(k_ref, q_ref, t, s_ref):
    k0 = pl.multiple_of(t * TQ, TQ)
    for hh, hs in enumerate(_HEAD_SLICES):
        s_ref[hh] = _dot_nt(k_ref[0, pl.ds(k0, TQ), hs], q_ref[0, :, hs])


def _causal_triangle():
    shape = (TQ // SUBLANES, SUBLANES, TQ)
    row = lax.broadcasted_iota(I32, shape, 0) * SUBLANES + lax.broadcasted_iota(I32, shape, 1)
    return row <= lax.broadcasted_iota(I32, shape, 2)


def _edge_mask(j, t_true):
    shape = (TQ // SUBLANES, SUBLANES, TQ)
    row = lax.broadcasted_iota(I32, shape, 0) * SUBLANES + lax.broadcasted_iota(I32, shape, 1)
    return row - lax.broadcasted_iota(I32, shape, 2) <= (j - t_true) * TQ


def _sweep_tiles(j, s_refs, qk, consume, state):
    s0, s1 = s_refs
    last_pair = j // 2
    qk(0, s0)

    def body(p, st):
        t0 = 2 * p
        qk(t0 + 1, s1)
        st = consume(t0, t0, s0, st, False)
        qk(t0 + 2, s0)
        return consume(t0 + 1, t0 + 1, s1, st, False)

    state = lax.fori_loop(0, last_pair // 2, lambda q, st: body(2 * q + 1, body(2 * q, st)), state)
    state = lax.fori_loop(2 * (last_pair // 2), last_pair, body, state)
    ta = 2 * last_pair
    tb = jnp.minimum(ta + 1, j)
    qk(tb, s1)
    state = consume(ta, ta, s0, state, True)
    return consume(tb, ta + 1, s1, state, True)


def _init_state(acc_ref):
    acc_ref[...] = jnp.zeros(acc_ref.shape, F32)
    return (jnp.full((SUBLANES, TQ), M_INIT, F32),) * N_HEADS


def _write_heads(o_ref, acc_ref):
    for hh, hs in enumerate(_HEAD_SLICES):
        base = hh * VT_ROWS
        inv = 1.0 / acc_ref[base + HEAD_DIM:base + HEAD_DIM + SUBLANES, :]
        o_t = (_tile3(acc_ref[base:base + HEAD_DIM, :]) * inv[None]).reshape(HEAD_DIM, TQ)
        o_ref[0, :, hs] = o_t.T.astype(BF16)


def _f32_from_key(key):
    bits = jnp.where(key < 0, key ^ jnp.int32(0x7FFFFFFF), key)
    return lax.bitcast_convert_type(bits, F32)


def _key32_of_key16(key16):
    return (key16 << 16) + jnp.where(key16 < 0, 0xFFFF, 0)


def _bf16_from_key16(key16):
    return _f32_from_key(_key32_of_key16(key16))


def _dsa_kernel(a_ref, kc_ref, wk_ref, qa_ref, ka_ref, vat_ref, o_ref,
                scores_ref, hb_ref, s0_ref, s1_ref, acc_ref, *, topk, seq_bits):
    j = pl.program_id(1)
    n_tiles = j + 1
    w_rows = wk_ref[0].T[IDX_DIM:IDX_DIM + IDX_HEADS, :]
    w8 = [jnp.broadcast_to(w_rows[hh:hh + 1, :], (SUBLANES, TQ)) for hh in range(IDX_HEADS)]

    def score_tile(c, diagonal):
        k0 = pl.multiple_of(c * TQ, TQ)
        kc = kc_ref[0, pl.ds(k0, TQ), :]
        acc = jnp.zeros((TQ // SUBLANES, SUBLANES, TQ), F32)
        for hh in range(IDX_HEADS):
            lg = _dot_nt(kc, a_ref[0, :, hh * IDX_K:(hh + 1) * IDX_K])
            acc = acc + w8[hh][None] * jnp.maximum(_tile3(lg), 0.0)
        if diagonal:
            acc = jnp.where(_causal_triangle(), acc, NEG)
        acc = acc.reshape(TQ, TQ)
        scores_ref[pl.ds(k0, TQ), :] = acc
        hb_ref[pl.ds(k0, TQ), :] = acc.astype(BF16)

    def score_group(g, carry):
        for u in range(SCORE_GROUP):
            score_tile(SCORE_GROUP * g + u, False)
        return carry

    def score_single(c, carry):
        score_tile(c, False)
        return carry

    n_groups = j // SCORE_GROUP
    lax.fori_loop(0, n_groups, score_group, 0)
    lax.fori_loop(SCORE_GROUP * n_groups, j, score_single, 0)
    score_tile(j, True)

    def load_packed(c, u):
        r0 = pl.multiple_of(c * TQ + u * SEL_ROWS, SEL_ROWS)
        return hb_ref[pl.ds(r0, SEL_ROWS), :].reshape(SEL_ROWS // PACK_ROWS, PACK_ROWS, TQ)

    def load_scores(c, u):
        r0 = pl.multiple_of(c * TQ + u * SEL_ROWS, SEL_ROWS)
        return _tile3(scores_ref[pl.ds(r0, SEL_ROWS), :]), r0

    def sweep_count(tile, zero):
        cnt = lax.fori_loop(0, n_tiles // 2, lambda p, cnt: tile(2 * p + 1, tile(2 * p, cnt)), zero)
        return lax.fori_loop(2 * (n_tiles // 2), n_tiles, tile, cnt)

    one16, zero16 = jnp.int16(1), jnp.int16(0)

    def count_bf16(test):
        t = jnp.broadcast_to(test, (PACK_ROWS, TQ)).astype(BF16)[None]

        def tile(c, cnt):
            for u in range(TQ // SEL_ROWS):
                cnt = cnt + jnp.where(load_packed(c, u) >= t, one16, zero16)
            return cnt

        cnt = sweep_count(tile, jnp.zeros((SEL_ROWS // PACK_ROWS, PACK_ROWS, TQ), I16))
        return jnp.sum(jnp.sum(cnt.astype(F32), axis=0), axis=0, keepdims=True)

    def count_f32(pred):
        def tile(c, cnt):
            for u in range(TQ // SEL_ROWS):
                sc, r0 = load_scores(c, u)
                cnt = cnt + jnp.where(pred(sc, r0), 1.0, 0.0)
            return cnt

        cnt = sweep_count(tile, jnp.zeros((SEL_ROWS // SUBLANES, SUBLANES, TQ), F32))
        return jnp.sum(jnp.sum(cnt, axis=0), axis=0, keepdims=True)

    kf = float(topk)

    c0 = count_bf16(jnp.zeros((1, TQ), F32))
    start = jnp.where(c0 >= kf, 0, I16_MIN).astype(I32)

    def bit_step(i, val):
        test = val | (jnp.int32(1) << (14 - i))
        return jnp.where(count_bf16(_bf16_from_key16(test)) >= kf, test, val)

    hb_k = lax.fori_loop(0, 15, bit_step, start)

    lo = _key32_of_key16(jnp.maximum(hb_k - 1, BF16_KEY_NEG_INF))
    hi = _key32_of_key16(jnp.minimum(hb_k + 1, BF16_KEY_POS_INF))

    def bisect_step(i, carry):
        lo, hi, cge = carry
        mid = lo + ((hi - lo) >> 1)
        thr = _f32_from_key(mid)[None]
        c = count_f32(lambda sc, r0: sc >= thr)
        ok = c >= kf
        return jnp.where(ok, mid, lo), jnp.where(ok, hi, mid), jnp.where(ok, c, cge)

    lo, _, cge = lax.fori_loop(0, BISECT_STEPS, bisect_step, (lo, hi, jnp.full((1, TQ), kf, F32)))
    thr = _f32_from_key(lo)

    @pl.when(jnp.max(jnp.abs(cge - kf)) > 0.0)
    def _():
        thr3 = thr[None]
        need = kf - count_f32(lambda sc, r0: sc > thr3)
        sub = (lax.broadcasted_iota(I32, (SEL_ROWS // SUBLANES, SUBLANES, TQ), 0) * SUBLANES
               + lax.broadcasted_iota(I32, (SEL_ROWS // SUBLANES, SUBLANES, TQ), 1))

        def cut_step(i, cut):
            test = cut | (jnp.int32(1) << (seq_bits - 1 - i))
            f = count_f32(lambda sc, r0: (sc == thr3) & (r0 + sub < test[None]))
            return jnp.where(f < need, test, cut)

        cut = lax.fori_loop(0, seq_bits, cut_step, jnp.zeros((1, TQ), I32))

        def demote(c, carry):
            r0 = pl.multiple_of(c * SEL_ROWS, SEL_ROWS)
            sc = _tile3(scores_ref[pl.ds(r0, SEL_ROWS), :])
            scores_ref[pl.ds(r0, SEL_ROWS), :] = jnp.where(
                (sc == thr3) & (r0 + sub > cut[None]), NEG, sc).reshape(SEL_ROWS, TQ)
            return carry

        lax.fori_loop(0, n_tiles * (TQ // SEL_ROWS), demote, 0)

    def consume(t, t_true, s_ref, state, edge):
        k0 = pl.multiple_of(t * TQ, TQ)
        sel = _tile3(scores_ref[pl.ds(k0, TQ), :]) >= thr[None]
        if edge:
            sel = sel & _edge_mask(j, t_true)
        bias = jnp.where(sel, 0.0, NEG)
        return tuple(
            _softmax_tile(_tile3(s_ref[hh]) + bias, vat_ref[0, t, hh * VT_ROWS:(hh + 1) * VT_ROWS, :],
                          state[hh], acc_ref, hh)
            for hh in range(N_HEADS))

    qk = functools.partial(_qk_tile, ka_ref, qa_ref)
    _sweep_tiles(j, (s0_ref, s1_ref), qk, consume, _init_state(acc_ref))
    _write_heads(o_ref, acc_ref)


def _dsa_call(a_mat, kc, idx_out, qa, ka, vat):
    bsz, seq, _ = qa.shape
    topk = min(DSA_TOPK_MAX, seq // 4)
    seq_bits = (seq - 1).bit_length()
    kern = functools.partial(_dsa_kernel, topk=topk, seq_bits=seq_bits)
    return pl.pallas_call(
        kern,
        grid=(bsz, seq // TQ),
        in_specs=[
            pl.BlockSpec((1, TQ, IDX_HEADS * IDX_K), lambda b, j: (b, j, 0)),
            pl.BlockSpec((1, seq, IDX_K), lambda b, j: (b, 0, 0)),
            pl.BlockSpec((1, TQ, LANES), lambda b, j: (b, j, 0)),
            pl.BlockSpec((1, TQ, BR_W), lambda b, j: (b, j, 0)),
            pl.BlockSpec((1, seq, BR_W), lambda b, j: (b, 0, 0)),
            pl.BlockSpec((1, seq // TQ, N_HEADS * VT_ROWS, TQ), lambda b, j: (b, 0, 0, 0)),
        ],
        out_specs=pl.BlockSpec((1, TQ, BR_W), lambda b, j: (b, j, 0)),
        out_shape=jax.ShapeDtypeStruct((bsz, seq, BR_W), BF16),
        scratch_shapes=[
            pltpu.VMEM((seq, TQ), F32),
            pltpu.VMEM((seq, TQ), BF16),
            pltpu.VMEM((N_HEADS, TQ, TQ), F32),
            pltpu.VMEM((N_HEADS, TQ, TQ), F32),
            pltpu.VMEM((N_HEADS * VT_ROWS, TQ), F32),
        ],
        compiler_params=pltpu.CompilerParams(
            dimension_semantics=("arbitrary", "arbitrary"), vmem_limit_bytes=VMEM_LIMIT),
        name="dsa",
    )(a_mat, kc, idx_out, qa, ka, vat)


def _moba_kernel(qb_ref, kb_ref, vbt_ref, kmean_ref, o_ref, bias_ref, s0_ref, s1_ref, acc_ref, *, n_sel):
    j = pl.program_id(1)
    nb = kmean_ref.shape[1]
    blk = lax.broadcasted_iota(I32, (nb, TQ), 0).astype(F32)
    jf = j.astype(F32)

    for hh in range(N_HEADS):
        hs = slice(hh * HEAD_DIM, (hh + 1) * HEAD_DIM)
        km_hi, km_lo = _split_bf16(kmean_ref[0, :, hs])
        q = qb_ref[0, :, hs]
        gate = jnp.where(blk < jf, _dot_nt(km_hi, q) + _dot_nt(km_lo, q), NEG)
        chosen = jnp.zeros((nb, TQ), F32)
        for _ in range(n_sel):
            best = jnp.max(gate, axis=0, keepdims=True)
            first = jnp.min(jnp.where(gate == best, blk, float(nb)), axis=0, keepdims=True)
            hit = blk == first
            chosen = jnp.where(hit, 1.0, chosen)
            gate = jnp.where(hit, -jnp.inf, gate)
        bias = jnp.where((chosen > 0.0) & (blk < jf), 0.0, NEG)
        for n in range(nb):
            bias_ref[hh, n] = jnp.broadcast_to(bias[n:n + 1, :], (SUBLANES, TQ))

    def consume(t, t_true, s_ref, state, edge):
        new = []
        for hh in range(N_HEADS):
            bias = bias_ref[hh, t]
            if edge:
                bias = jnp.where(t_true < j, bias, 0.0)
            s3 = _tile3(s_ref[hh]) + bias[None]
            if edge:
                s3 = jnp.where(_edge_mask(j, t_true), s3, NEG)
            new.append(_softmax_tile(s3, vbt_ref[0, t, hh * VT_ROWS:(hh + 1) * VT_ROWS, :], state[hh], acc_ref, hh))
        return tuple(new)

    qk = functools.partial(_qk_tile, kb_ref, qb_ref)
    _sweep_tiles(j, (s0_ref, s1_ref), qk, consume, _init_state(acc_ref))
    _write_heads(o_ref, acc_ref)


def _moba_call(qb, kb, vbt, kmean):
    bsz, seq, _ = qb.shape
    nb = seq // MOBA_BLOCK
    kern = functools.partial(_moba_kernel, n_sel=min(MOBA_TOPK, nb - 1))
    return pl.pallas_call(
        kern,
        grid=(bsz, seq // TQ),
        in_specs=[
            pl.BlockSpec((1, TQ, BR_W), lambda b, j: (b, j, 0)),
            pl.BlockSpec((1, seq, BR_W), lambda b, j: (b, 0, 0)),
            pl.BlockSpec((1, seq // TQ, N_HEADS * VT_ROWS, TQ), lambda b, j: (b, 0, 0, 0)),
            pl.BlockSpec((1, nb, BR_W), lambda b, j: (b, 0, 0)),
        ],
        out_specs=pl.BlockSpec((1, TQ, BR_W), lambda b, j: (b, j, 0)),
        out_shape=jax.ShapeDtypeStruct((bsz, seq, BR_W), BF16),
        scratch_shapes=[
            pltpu.VMEM((N_HEADS, nb, SUBLANES, TQ), F32),
            pltpu.VMEM((N_HEADS, TQ, TQ), F32),
            pltpu.VMEM((N_HEADS, TQ, TQ), F32),
            pltpu.VMEM((N_HEADS * VT_ROWS, TQ), F32),
        ],
        compiler_params=pltpu.CompilerParams(
            dimension_semantics=("arbitrary", "arbitrary"), vmem_limit_bytes=VMEM_LIMIT),
        name="moba",
    )(qb, kb, vbt, kmean)


def _out_ffn_kernel(x_ref, oa_ref, ob_ref, ga_ref, gb_ref, mod_ref, gffn_ref,
                    wba_ref, wbb_ref, wo_ref, wgu_ref, wd_ref, o_ref, act_ref):
    merged = (ga_ref[0].astype(F32) * _dot(oa_ref[0], wba_ref[...])
              + gb_ref[0].astype(F32) * _dot(ob_ref[0], wbb_ref[...]))
    x1 = x_ref[0] + mod_ref[0, 2:3, :] * _dot(merged.astype(BF16), wo_ref[...])
    y = x1 * lax.rsqrt(jnp.mean(x1 * x1, axis=-1, keepdims=True) + EPS) * gffn_ref[...]
    h = (y * (1.0 + mod_ref[0, 4:5, :]) + mod_ref[0, 3:4, :]).astype(BF16)
    for c0 in range(0, D_FF, FF_CHUNK):
        g = _dot(h, wgu_ref[:, c0:c0 + FF_CHUNK])
        u = _dot(h, wgu_ref[:, D_FF + c0:D_FF + c0 + FF_CHUNK])
        act_ref[:, c0:c0 + FF_CHUNK] = (g * jax.nn.sigmoid(g) * u).astype(BF16)
    o_ref[0] = x1 + mod_ref[0, 5:6, :] * _dot(act_ref[...], wd_ref[...])


def _out_ffn_call(x, oa, ob, ga, gb, mod3, gffn, wba, wbb, wo, wgu, wd):
    bsz, seq, d = x.shape
    tm = TM_FFN
    const = lambda b, i: (0, 0)
    tok = lambda b, i: (b, i, 0)
    resident = lambda a: pl.BlockSpec(a.shape, const, pipeline_mode=pl.Buffered(1))
    return pl.pallas_call(
        _out_ffn_kernel,
        grid=(bsz, seq // tm),
        in_specs=[
            pl.BlockSpec((1, tm, d), tok),
            pl.BlockSpec((1, tm, BR_W), tok),
            pl.BlockSpec((1, tm, BR_W), tok),
            pl.BlockSpec((1, tm, d), tok),
            pl.BlockSpec((1, tm, d), tok),
            pl.BlockSpec((1, N_MOD, d), lambda b, i: (b, 0, 0)),
            pl.BlockSpec((1, d), const),
            resident(wba), resident(wbb), resident(wo), resident(wgu), resident(wd),
        ],
        out_specs=pl.BlockSpec((1, tm, d), tok),
        out_shape=jax.ShapeDtypeStruct((bsz, seq, d), F32),
        scratch_shapes=[pltpu.VMEM((tm, D_FF), BF16)],
        compiler_params=pltpu.CompilerParams(
            dimension_semantics=("arbitrary", "arbitrary"), vmem_limit_bytes=VMEM_LIMIT),
        name="out_ffn",
    )(x, oa, ob, ga, gb, mod3, gffn, wba, wbb, wo, wgu, wd)


def _rope_tables(seq, dim):
    inv = ROPE_THETA ** (-jnp.arange(0, dim, 2, dtype=F32) / dim)
    ang = jnp.arange(seq, dtype=F32)[:, None] * inv[None, :]
    cos, sin = jnp.cos(ang), jnp.sin(ang)
    reps = LANES // dim
    return (jnp.tile(jnp.concatenate([cos, cos], axis=-1), (1, reps)),
            jnp.tile(jnp.concatenate([-sin, sin], axis=-1), (1, reps)))


def _layer(x, mod, g_mix, g_ffn, w_in, g_qa, g_ka, g_qb, g_kb, w_br_a, w_br_b, w_out, w_gu, w_down):
    bsz, seq, d = x.shape
    mod3 = mod.reshape(bsz, N_MOD, d)

    o = 0
    cols = {}
    for name, width in (("qa", BR_W), ("ka", BR_W), ("va", BR_W), ("qi", IDX_HEADS * IDX_DIM),
                        ("ki", IDX_DIM), ("wi", IDX_HEADS), ("qb", BR_W), ("kb", BR_W), ("vb", BR_W),
                        ("ga", D_MODEL), ("gb", D_MODEL)):
        cols[name] = w_in[:, o:o + width]
        o += width
    wqk = jnp.concatenate([cols["qa"], cols["ka"], cols["qb"], cols["kb"]], axis=1).astype(BF16)
    wvt = jnp.concatenate([cols["va"], cols["vb"]], axis=1).T.astype(BF16)
    wg = jnp.concatenate([cols["ga"], cols["gb"]], axis=1).astype(BF16)
    pad = IDX_OUT_W - IDX_HEADS * IDX_DIM - IDX_DIM - IDX_HEADS
    wi = jnp.concatenate([cols["qi"], cols["ki"], cols["wi"], jnp.zeros((d, pad), F32)], axis=1).astype(BF16)
    gains = jnp.stack([g_qa, g_ka, g_qb, g_kb])
    cosh, sinh = _rope_tables(seq, HEAD_DIM)
    cosi, sini = _rope_tables(seq, IDX_DIM)

    qa, ka, qb, kb, vat, vbt, ga, gb, a_mat, kc, kw, kmean = _proj_call(
        x, mod3, g_mix.reshape(1, d), wqk, wvt, wg, wi, gains, cosh, sinh, cosi, sini)

    oa = _dsa_call(a_mat, kc, kw, qa, ka, vat)
    ob = _moba_call(qb, kb, vbt, kmean.reshape(bsz, seq // MOBA_BLOCK, BR_W))

    return _out_ffn_call(x, oa, ob, ga, gb, mod3, g_ffn.reshape(1, d),
                         w_br_a.astype(BF16), w_br_b.astype(BF16), w_out.astype(BF16),
                         w_gu.astype(BF16), w_down.astype(BF16))


def kernel(x, c, w_mod, b_mod, g_mix_norm, g_ffn_norm, w_in, g_q_dsa, g_k_dsa, g_q_moba, g_k_moba,
           w_br_dsa, w_br_moba, w_out, w_gate_up, w_down):
    assert x.shape[1] % TM_PROJ == 0 and x.shape[2] == D_MODEL and w_gate_up.shape[2] == 2 * D_FF
    for l in range(w_mod.shape[0]):
        mod = _mod_call(c, w_mod[l], b_mod[l])
        x = _layer(x, mod, g_mix_norm[l], g_ffn_norm[l], w_in[l], g_q_dsa[l], g_k_dsa[l], g_q_moba[l],
                   g_k_moba[l], w_br_dsa[l], w_br_moba[l], w_out[l], w_gate_up[l], w_down[l])
    return x
```

```python
import functools

import jax
import jax.numpy as jnp
from jax import lax
from jax.experimental import pallas as pl
from jax.experimental.pallas import tpu as pltpu

F32 = jnp.float32
BF16 = jnp.bfloat16
I32 = jnp.int32
I16 = jnp.int16

D_MODEL = 1024
HEAD_DIM = 128
N_HEADS = 4
BR_W = N_HEADS * HEAD_DIM
IDX_HEADS = 8
IDX_DIM = 64
DSA_TOPK_MAX = 256
MOBA_BLOCK = 256
MOBA_TOPK = 3
D_FF = 2816
ROPE_THETA = 10000.0
EPS = 1e-6
NEG = -1e30
LOG2_E = 1.4426950408889634
M_INIT = -1e29
N_MOD = 6
IDX_OUT_W = 640
IDX_K = 256
I16_MIN = -(2 ** 15)
BF16_KEY_NEG_INF = (0xFF80 ^ 0x7FFF) - 0x10000
BF16_KEY_POS_INF = 0x7F80
BISECT_STEPS = 17

LANES = 128
SUBLANES = 8
PACK_ROWS = 16
VT_ROWS = HEAD_DIM + PACK_ROWS
TQ = 256
TM_PROJ = 512
PROJ_ROWS = 256
TM_FFN = 512
FF_CHUNK = 256
SEL_ROWS = 32
SCORE_GROUPS = (4, 2, 1)
VMEM_LIMIT = 56 * 1024 * 1024

_NT = (((1,), (1,)), ((), ()))
_HEAD_SLICES = tuple(slice(h * HEAD_DIM, (h + 1) * HEAD_DIM) for h in range(N_HEADS))


def _split_bf16(x):
    hi = x.astype(BF16)
    lo = (x - hi.astype(F32)).astype(BF16)
    return hi, lo


def _dot(a, b):
    return jnp.dot(a, b, preferred_element_type=F32)


def _dot_nt(a, b):
    return lax.dot_general(a, b, _NT, preferred_element_type=F32)


def _mod_kernel(c_ref, w_ref, b_ref, o_ref):
    c = c_ref[...]
    a_hi, a_lo = _split_bf16(c * jax.nn.sigmoid(c))
    w_hi, w_lo = _split_bf16(w_ref[...])
    o_ref[...] = _dot(a_hi, w_hi) + _dot(a_lo, w_hi) + _dot(a_hi, w_lo) + b_ref[...]


def _mod_call(c, w_mod, b_mod):
    bsz, d = c.shape
    n = w_mod.shape[1]
    tn = 1024
    return pl.pallas_call(
        _mod_kernel,
        grid=(n // tn,),
        in_specs=[
            pl.BlockSpec((bsz, d), lambda i: (0, 0)),
            pl.BlockSpec((d, tn), lambda i: (0, i)),
            pl.BlockSpec((1, tn), lambda i: (0, i)),
        ],
        out_specs=pl.BlockSpec((bsz, tn), lambda i: (0, i)),
        out_shape=jax.ShapeDtypeStruct((bsz, n), F32),
        compiler_params=pltpu.CompilerParams(vmem_limit_bytes=VMEM_LIMIT),
        name="mod",
    )(c, w_mod, b_mod.reshape(1, n))


def _rope_partner_64(x):
    lane = lax.broadcasted_iota(I32, x.shape, 1)
    return jnp.where((lane % IDX_DIM) < IDX_DIM // 2,
                     pltpu.roll(x, LANES - IDX_DIM // 2, 1), pltpu.roll(x, IDX_DIM // 2, 1))


def _proj_kernel(x_ref, mod_ref, gmix_ref, wqk_ref, wvt_ref, wg_ref, wi_ref, gains_ref,
                 cosh_ref, sinh_ref, cosi_ref, sini_ref,
                 qa_ref, ka_ref, qb_ref, kb_ref, vat_ref, vbt_ref, ga_ref, gb_ref, a_ref, kc_ref, kw_ref, kmean_ref):
    for blk in range(x_ref.shape[1] // PROJ_ROWS):
        _proj_rows(blk, x_ref, mod_ref, gmix_ref, wqk_ref, wvt_ref, wg_ref, wi_ref, gains_ref,
                   cosh_ref, sinh_ref, cosi_ref, sini_ref,
                   qa_ref, ka_ref, qb_ref, kb_ref, vat_ref, vbt_ref, ga_ref, gb_ref, a_ref, kc_ref, kw_ref, kmean_ref)


def _proj_rows(blk, x_ref, mod_ref, gmix_ref, wqk_ref, wvt_ref, wg_ref, wi_ref, gains_ref,
               cosh_ref, sinh_ref, cosi_ref, sini_ref,
               qa_ref, ka_ref, qb_ref, kb_ref, vat_ref, vbt_ref, ga_ref, gb_ref, a_ref, kc_ref, kw_ref, kmean_ref):
    rows = slice(blk * PROJ_ROWS, (blk + 1) * PROJ_ROWS)
    x = x_ref[0, rows, :]
    y = x * lax.rsqrt(jnp.mean(x * x, axis=-1, keepdims=True) + EPS) * gmix_ref[...]
    h = y * (1.0 + mod_ref[0, 1:2, :]) + mod_ref[0, 0:1, :]
    h_hi = h.astype(BF16)
    cosh, sinh = cosh_ref[rows, :], sinh_ref[rows, :]
    scale = HEAD_DIM ** -0.5 * LOG2_E

    for gi, out_ref in enumerate((qa_ref, ka_ref, qb_ref, kb_ref)):
        p = _dot(h_hi, wqk_ref[:, gi * BR_W:(gi + 1) * BR_W])
        for hh in range(N_HEADS):
            ph = p[:, hh * HEAD_DIM:(hh + 1) * HEAD_DIM]
            yh = ph * lax.rsqrt(jnp.mean(ph * ph, axis=-1, keepdims=True) + EPS) * gains_ref[gi:gi + 1, :]
            r = yh * cosh + pltpu.roll(yh, HEAD_DIM // 2, 1) * sinh
            if gi == 3:
                kmean_ref[0, blk, :, hh * HEAD_DIM:(hh + 1) * HEAD_DIM] = jnp.mean(r, axis=0, keepdims=True)
            if gi % 2 == 0:
                r = r * scale
            out_ref[0, rows, hh * HEAD_DIM:(hh + 1) * HEAD_DIM] = r.astype(BF16)

    vt = _dot_nt(wvt_ref[...], h_hi)
    ones = jnp.ones((VT_ROWS - HEAD_DIM, TQ), BF16)
    for gi, out_ref in enumerate((vat_ref, vbt_ref)):
        for hh in range(N_HEADS):
            src = gi * BR_W + hh * HEAD_DIM
            out_ref[0, blk, hh * VT_ROWS:hh * VT_ROWS + HEAD_DIM, :] = vt[src:src + HEAD_DIM, :].astype(BF16)
            out_ref[0, blk, hh * VT_ROWS + HEAD_DIM:(hh + 1) * VT_ROWS, :] = ones

    for gi, out_ref in enumerate((ga_ref, gb_ref)):
        g = _dot(h_hi, wg_ref[:, gi * D_MODEL:(gi + 1) * D_MODEL])
        out_ref[0, rows, :] = jax.nn.sigmoid(g).astype(BF16)

    pi = _dot(h_hi, wi_ref[...])
    cosi, sini = cosi_ref[rows, :], sini_ref[rows, :]
    first = lax.broadcasted_iota(I32, (PROJ_ROWS, LANES), 1) < IDX_DIM
    half = LANES // 2

    def hi_lo(v):
        hi = v.astype(BF16).astype(F32)
        return hi, v - hi

    n_q = IDX_HEADS * IDX_DIM // LANES
    for g4 in range(n_q):
        xg = pi[:, g4 * LANES:(g4 + 1) * LANES]
        hi, lo = hi_lo(xg * cosi + _rope_partner_64(xg) * sini)
        hi_r, lo_r = pltpu.roll(hi, half, 1), pltpu.roll(lo, half, 1)
        for head, parts in ((2 * g4, (jnp.where(first, hi, lo_r), jnp.where(first, hi, 0.0))),
                            (2 * g4 + 1, (jnp.where(first, hi_r, lo), jnp.where(first, hi_r, 0.0)))):
            for pi_, part in enumerate(parts):
                a_ref[0, rows, head * IDX_K + pi_ * LANES:head * IDX_K + (pi_ + 1) * LANES] = part.astype(BF16)
    xg = pi[:, n_q * LANES:(n_q + 1) * LANES]
    roped = xg * cosi + _rope_partner_64(xg) * sini
    hi, lo = hi_lo(roped)
    kc_ref[0, rows, :LANES] = jnp.where(first, hi, pltpu.roll(hi, half, 1)).astype(BF16)
    kc_ref[0, rows, LANES:] = jnp.where(first, lo, 0.0).astype(BF16)
    w_scale = (IDX_HEADS ** -0.5) * (IDX_DIM ** -0.5)
    kw_ref[0, rows, :] = jnp.where(first, roped, xg * w_scale)


def _proj_call(x, mod3, gmix, wqk, wvt, wg, wi, gains, cosh, sinh, cosi, sini):
    bsz, seq, d = x.shape
    tm = TM_PROJ
    nb = seq // MOBA_BLOCK
    const = lambda b, i: (0, 0)
    tok = lambda b, i: (b, i, 0)
    tab = lambda b, i: (i, 0)
    resident = lambda a: pl.BlockSpec(a.shape, const, pipeline_mode=pl.Buffered(1))
    out_shape = (
        [jax.ShapeDtypeStruct((bsz, seq, BR_W), BF16)] * 4
        + [jax.ShapeDtypeStruct((bsz, seq // TQ, N_HEADS * VT_ROWS, TQ), BF16)] * 2
        + [jax.ShapeDtypeStruct((bsz, seq, D_MODEL), BF16)] * 2
        + [jax.ShapeDtypeStruct((bsz, seq, IDX_HEADS * IDX_K), BF16),
           jax.ShapeDtypeStruct((bsz, seq, IDX_K), BF16),
           jax.ShapeDtypeStruct((bsz, seq, LANES), F32),
           jax.ShapeDtypeStruct((bsz, nb, 1, BR_W), F32)]
    )
    out_specs = (
        [pl.BlockSpec((1, tm, BR_W), tok)] * 4
        + [pl.BlockSpec((1, tm // TQ, N_HEADS * VT_ROWS, TQ), lambda b, i: (b, i, 0, 0))] * 2
        + [pl.BlockSpec((1, tm, D_MODEL), tok)] * 2
        + [pl.BlockSpec((1, tm, IDX_HEADS * IDX_K), tok),
           pl.BlockSpec((1, tm, IDX_K), tok),
           pl.BlockSpec((1, tm, LANES), tok),
           pl.BlockSpec((1, tm // MOBA_BLOCK, 1, BR_W), lambda b, i: (b, i, 0, 0))]
    )
    return pl.pallas_call(
        _proj_kernel,
        grid=(bsz, seq // tm),
        in_specs=[
            pl.BlockSpec((1, tm, d), tok),
            pl.BlockSpec((1, N_MOD, d), lambda b, i: (b, 0, 0)),
            pl.BlockSpec((1, d), const),
            resident(wqk), resident(wvt), resident(wg), resident(wi),
            pl.BlockSpec(gains.shape, const),
            pl.BlockSpec((tm, LANES), tab),
            pl.BlockSpec((tm, LANES), tab),
            pl.BlockSpec((tm, LANES), tab),
            pl.BlockSpec((tm, LANES), tab),
        ],
        out_specs=out_specs,
        out_shape=out_shape,
        compiler_params=pltpu.CompilerParams(
            dimension_semantics=("arbitrary", "arbitrary"), vmem_limit_bytes=VMEM_LIMIT),
        name="proj",
    )(x, mod3, gmix, wqk, wvt, wg, wi, gains, cosh, sinh, cosi, sini)


def _tile3(x):
    return x.reshape(x.shape[0] // SUBLANES, SUBLANES, x.shape[1])


def _allreduce_sublanes(x, op):
    for shift in (4, 2, 1):
        x = op(x, pltpu.roll(x, shift, 0))
    return x


def _softmax_tile(s3, vt, m_old, acc_ref, hh):
    rows = slice(hh * VT_ROWS, (hh + 1) * VT_ROWS)
    m_new = jnp.maximum(m_old, _allreduce_sublanes(jnp.max(s3, axis=0), jnp.maximum))
    alpha = jnp.exp2(m_old - m_new)
    p = jnp.exp2(s3 - m_new[None]).reshape(s3.shape[0] * SUBLANES, TQ).astype(BF16)
    acc_ref[rows, :] = (_tile3(acc_ref[rows, :]) * alpha[None]).reshape(VT_ROWS, TQ) + _dot(vt, p)
    return m_new


def _qk_tile(k_ref, q_ref, t, s_ref):
    k0 = pl.multiple_of(t * TQ, TQ)
    for hh, hs in enumerate(_HEAD_SLICES):
        s_ref[hh] = _dot_nt(k_ref[0, pl.ds(k0, TQ), hs], q_ref[0, :, hs])


def _causal_triangle():
    shape = (TQ // SUBLANES, SUBLANES, TQ)
    row = lax.broadcasted_iota(I32, shape, 0) * SUBLANES + lax.broadcasted_iota(I32, shape, 1)
    return row <= lax.broadcasted_iota(I32, shape, 2)


def _edge_mask(j, t_true):
    shape = (TQ // SUBLANES, SUBLANES, TQ)
    row = lax.broadcasted_iota(I32, shape, 0) * SUBLANES + lax.broadcasted_iota(I32, shape, 1)
    return row - lax.broadcasted_iota(I32, shape, 2) <= (j - t_true) * TQ


def _sweep_tiles(j, s_refs, qk, consume, state):
    s0, s1 = s_refs
    last_pair = j // 2
    qk(0, s0)

    def body(p, st):
        t0 = 2 * p
        qk(t0 + 1, s1)
        st = consume(t0, t0, s0, st, False)
        qk(t0 + 2, s0)
        return consume(t0 + 1, t0 + 1, s1, st, False)

    state = lax.fori_loop(0, last_pair // 2, lambda q, st: body(2 * q + 1, body(2 * q, st)), state)
    state = lax.fori_loop(2 * (last_pair // 2), last_pair, body, state)
    ta = 2 * last_pair
    tb = jnp.minimum(ta + 1, j)
    qk(tb, s1)
    state = consume(ta, ta, s0, state, True)
    return consume(tb, ta + 1, s1, state, True)


def _init_state(acc_ref):
    acc_ref[...] = jnp.zeros(acc_ref.shape, F32)
    return (jnp.full((SUBLANES, TQ), M_INIT, F32),) * N_HEADS


def _write_heads(o_ref, acc_ref):
    for hh, hs in enumerate(_HEAD_SLICES):
        base = hh * VT_ROWS
        inv = 1.0 / acc_ref[base + HEAD_DIM:base + HEAD_DIM + SUBLANES, :]
        o_t = (_tile3(acc_ref[base:base + HEAD_DIM, :]) * inv[None]).reshape(HEAD_DIM, TQ)
        o_ref[0, :, hs] = o_t.T.astype(BF16)


def _f32_from_key(key):
    bits = jnp.where(key < 0, key ^ jnp.int32(0x7FFFFFFF), key)
    return lax.bitcast_convert_type(bits, F32)


def _key32_of_key16(key16):
    return (key16 << 16) + jnp.where(key16 < 0, 0xFFFF, 0)


def _bf16_from_key16(key16):
    return _f32_from_key(_key32_of_key16(key16))


def _dsa_kernel(a_ref, kc_ref, wk_ref, qa_ref, ka_ref, vat_ref, o_ref,
                scores_ref, hb_ref, s0_ref, s1_ref, acc_ref, *, topk, seq_bits):
    j = pl.program_id(1)
    n_tiles = j + 1
    w_rows = wk_ref[0].T[IDX_DIM:IDX_DIM + IDX_HEADS, :]
    w8 = [jnp.broadcast_to(w_rows[hh:hh + 1, :], (SUBLANES, TQ)) for hh in range(IDX_HEADS)]

    def score_tile(c, diagonal):
        k0 = pl.multiple_of(c * TQ, TQ)
        kc = kc_ref[0, pl.ds(k0, TQ), :]
        acc = jnp.zeros((TQ // SUBLANES, SUBLANES, TQ), F32)
        for hh in range(IDX_HEADS):
            lg = _dot_nt(kc, a_ref[0, :, hh * IDX_K:(hh + 1) * IDX_K])
            acc = acc + w8[hh][None] * jnp.maximum(_tile3(lg), 0.0)
        if diagonal:
            acc = jnp.where(_causal_triangle(), acc, NEG)
        acc = acc.reshape(TQ, TQ)
        scores_ref[pl.ds(k0, TQ), :] = acc
        hb_ref[pl.ds(k0, TQ), :] = acc.astype(BF16)

    def score_run(start, n_trips, width):
        def trip(g, carry):
            for u in range(width):
                score_tile(start + width * g + u, False)
            return carry

        lax.fori_loop(0, n_trips, trip, 0)
        return start + width * n_trips

    done = 0
    for width in SCORE_GROUPS:
        done = score_run(done, (j - done) // width, width)
    score_tile(j, True)

    def load_packed(c, u):
        r0 = pl.multiple_of(c * TQ + u * SEL_ROWS, SEL_ROWS)
        return hb_ref[pl.ds(r0, SEL_ROWS), :].reshape(SEL_ROWS // PACK_ROWS, PACK_ROWS, TQ)

    def load_scores(c, u):
        r0 = pl.multiple_of(c * TQ + u * SEL_ROWS, SEL_ROWS)
        return _tile3(scores_ref[pl.ds(r0, SEL_ROWS), :]), r0

    def sweep_count(tile, zero):
        cnt = lax.fori_loop(0, n_tiles // 2, lambda p, cnt: tile(2 * p + 1, tile(2 * p, cnt)), zero)
        return lax.fori_loop(2 * (n_tiles // 2), n_tiles, tile, cnt)

    one16, zero16 = jnp.int16(1), jnp.int16(0)

    def count_bf16(test):
        t = jnp.broadcast_to(test, (PACK_ROWS, TQ)).astype(BF16)[None]

        def tile(c, cnt):
            for u in range(TQ // SEL_ROWS):
                cnt = cnt + jnp.where(load_packed(c, u) >= t, one16, zero16)
            return cnt

        cnt = sweep_count(tile, jnp.zeros((SEL_ROWS // PACK_ROWS, PACK_ROWS, TQ), I16))
        return jnp.sum(jnp.sum(cnt.astype(F32), axis=0), axis=0, keepdims=True)

    def count_f32(pred):
        def tile(c, cnt):
            for u in range(TQ // SEL_ROWS):
                sc, r0 = load_scores(c, u)
                cnt = cnt + jnp.where(pred(sc, r0), 1.0, 0.0)
            return cnt

        cnt = sweep_count(tile, jnp.zeros((SEL_ROWS // SUBLANES, SUBLANES, TQ), F32))
        return jnp.sum(jnp.sum(cnt, axis=0), axis=0, keepdims=True)

    kf = float(topk)

    c0 = count_bf16(jnp.zeros((1, TQ), F32))
    start = jnp.where(c0 >= kf, 0, I16_MIN).astype(I32)

    def bit_step(i, val):
        test = val | (jnp.int32(1) << (14 - i))
        return jnp.where(count_bf16(_bf16_from_key16(test)) >= kf, test, val)

    hb_k = lax.fori_loop(0, 15, bit_step, start)

    below = _key32_of_key16(jnp.maximum(hb_k - 1, BF16_KEY_NEG_INF))
    lo = below + ((_key32_of_key16(hb_k) - below) >> 1) - 1
    hi = _key32_of_key16(jnp.minimum(hb_k + 1, BF16_KEY_POS_INF))

    def bisect_step(i, carry):
        lo, hi, cge = carry
        mid = lo + ((hi - lo) >> 1)
        thr = _f32_from_key(mid)[None]
        c = count_f32(lambda sc, r0: sc >= thr)
        ok = c >= kf
        return jnp.where(ok, mid, lo), jnp.where(ok, hi, mid), jnp.where(ok, c, cge)

    lo, _, cge = lax.fori_loop(0, BISECT_STEPS, bisect_step, (lo, hi, jnp.full((1, TQ), kf, F32)))
    thr = _f32_from_key(lo)

    @pl.when(jnp.max(jnp.abs(cge - kf)) > 0.0)
    def _():
        thr3 = thr[None]
        need = kf - count_f32(lambda sc, r0: sc > thr3)
        sub = (lax.broadcasted_iota(I32, (SEL_ROWS // SUBLANES, SUBLANES, TQ), 0) * SUBLANES
               + lax.broadcasted_iota(I32, (SEL_ROWS // SUBLANES, SUBLANES, TQ), 1))

        def cut_step(i, cut):
            test = cut | (jnp.int32(1) << (seq_bits - 1 - i))
            f = count_f32(lambda sc, r0: (sc == thr3) & (r0 + sub < test[None]))
            return jnp.where(f < need, test, cut)

        cut = lax.fori_loop(0, seq_bits, cut_step, jnp.zeros((1, TQ), I32))

        def demote(c, carry):
            r0 = pl.multiple_of(c * SEL_ROWS, SEL_ROWS)
            sc = _tile3(scores_ref[pl.ds(r0, SEL_ROWS), :])
            scores_ref[pl.ds(r0, SEL_ROWS), :] = jnp.where(
                (sc == thr3) & (r0 + sub > cut[None]), NEG, sc).reshape(SEL_ROWS, TQ)
            return carry

        lax.fori_loop(0, n_tiles * (TQ // SEL_ROWS), demote, 0)

    def consume(t, t_true, s_ref, state, edge):
        k0 = pl.multiple_of(t * TQ, TQ)
        sel = _tile3(scores_ref[pl.ds(k0, TQ), :]) >= thr[None]
        if edge:
            sel = sel & _edge_mask(j, t_true)
        bias = jnp.where(sel, 0.0, NEG)
        return tuple(
            _softmax_tile(_tile3(s_ref[hh]) + bias, vat_ref[0, t, hh * VT_ROWS:(hh + 1) * VT_ROWS, :],
                          state[hh], acc_ref, hh)
            for hh in range(N_HEADS))

    qk = functools.partial(_qk_tile, ka_ref, qa_ref)
    _sweep_tiles(j, (s0_ref, s1_ref), qk, consume, _init_state(acc_ref))
    _write_heads(o_ref, acc_ref)


def _dsa_call(a_mat, kc, idx_out, qa, ka, vat):
    bsz, seq, _ = qa.shape
    topk = min(DSA_TOPK_MAX, seq // 4)
    seq_bits = (seq - 1).bit_length()
    kern = functools.partial(_dsa_kernel, topk=topk, seq_bits=seq_bits)
    return pl.pallas_call(
        kern,
        grid=(bsz, seq // TQ),
        in_specs=[
            pl.BlockSpec((1, TQ, IDX_HEADS * IDX_K), lambda b, j: (b, j, 0)),
            pl.BlockSpec((1, seq, IDX_K), lambda b, j: (b, 0, 0)),
            pl.BlockSpec((1, TQ, LANES), lambda b, j: (b, j, 0)),
            pl.BlockSpec((1, TQ, BR_W), lambda b, j: (b, j, 0)),
            pl.BlockSpec((1, seq, BR_W), lambda b, j: (b, 0, 0)),
            pl.BlockSpec((1, seq // TQ, N_HEADS * VT_ROWS, TQ), lambda b, j: (b, 0, 0, 0)),
        ],
        out_specs=pl.BlockSpec((1, TQ, BR_W), lambda b, j: (b, j, 0)),
        out_shape=jax.ShapeDtypeStruct((bsz, seq, BR_W), BF16),
        scratch_shapes=[
            pltpu.VMEM((seq, TQ), F32),
            pltpu.VMEM((seq, TQ), BF16),
            pltpu.VMEM((N_HEADS, TQ, TQ), F32),
            pltpu.VMEM((N_HEADS, TQ, TQ), F32),
            pltpu.VMEM((N_HEADS * VT_ROWS, TQ), F32),
        ],
        compiler_params=pltpu.CompilerParams(
            dimension_semantics=("arbitrary", "arbitrary"), vmem_limit_bytes=VMEM_LIMIT),
        name="dsa",
    )(a_mat, kc, idx_out, qa, ka, vat)


def _moba_kernel(qb_ref, kb_ref, vbt_ref, kmean_ref, o_ref, bias_ref, s0_ref, s1_ref, acc_ref, *, n_sel):
    j = pl.program_id(1)
    nb = kmean_ref.shape[1]
    blk = lax.broadcasted_iota(I32, (nb, TQ), 0).astype(F32)
    jf = j.astype(F32)

    for hh in range(N_HEADS):
        hs = slice(hh * HEAD_DIM, (hh + 1) * HEAD_DIM)
        km_hi, km_lo = _split_bf16(kmean_ref[0, :, hs])
        q = qb_ref[0, :, hs]
        gate = jnp.where(blk < jf, _dot_nt(km_hi, q) + _dot_nt(km_lo, q), NEG)
        chosen = jnp.zeros((nb, TQ), F32)
        for _ in range(n_sel):
            best = jnp.max(gate, axis=0, keepdims=True)
            first = jnp.min(jnp.where(gate == best, blk, float(nb)), axis=0, keepdims=True)
            hit = blk == first
            chosen = jnp.where(hit, 1.0, chosen)
            gate = jnp.where(hit, -jnp.inf, gate)
        bias = jnp.where((chosen > 0.0) & (blk < jf), 0.0, NEG)
        for n in range(nb):
            bias_ref[hh, n] = jnp.broadcast_to(bias[n:n + 1, :], (SUBLANES, TQ))

    def consume(t, t_true, s_ref, state, edge):
        new = []
        for hh in range(N_HEADS):
            bias = bias_ref[hh, t]
            if edge:
                bias = jnp.where(t_true < j, bias, 0.0)
            s3 = _tile3(s_ref[hh]) + bias[None]
            if edge:
                s3 = jnp.where(_edge_mask(j, t_true), s3, NEG)
            new.append(_softmax_tile(s3, vbt_ref[0, t, hh * VT_ROWS:(hh + 1) * VT_ROWS, :], state[hh], acc_ref, hh))
        return tuple(new)

    qk = functools.partial(_qk_tile, kb_ref, qb_ref)
    _sweep_tiles(j, (s0_ref, s1_ref), qk, consume, _init_state(acc_ref))
    _write_heads(o_ref, acc_ref)


def _moba_call(qb, kb, vbt, kmean):
    bsz, seq, _ = qb.shape
    nb = seq // MOBA_BLOCK
    kern = functools.partial(_moba_kernel, n_sel=min(MOBA_TOPK, nb - 1))
    return pl.pallas_call(
        kern,
        grid=(bsz, seq // TQ),
        in_specs=[
            pl.BlockSpec((1, TQ, BR_W), lambda b, j: (b, j, 0)),
            pl.BlockSpec((1, seq, BR_W), lambda b, j: (b, 0, 0)),
            pl.BlockSpec((1, seq // TQ, N_HEADS * VT_ROWS, TQ), lambda b, j: (b, 0, 0, 0)),
            pl.BlockSpec((1, nb, BR_W), lambda b, j: (b, 0, 0)),
        ],
        out_specs=pl.BlockSpec((1, TQ, BR_W), lambda b, j: (b, j, 0)),
        out_shape=jax.ShapeDtypeStruct((bsz, seq, BR_W), BF16),
        scratch_shapes=[
            pltpu.VMEM((N_HEADS, nb, SUBLANES, TQ), F32),
            pltpu.VMEM((N_HEADS, TQ, TQ), F32),
            pltpu.VMEM((N_HEADS, TQ, TQ), F32),
            pltpu.VMEM((N_HEADS * VT_ROWS, TQ), F32),
        ],
        compiler_params=pltpu.CompilerParams(
            dimension_semantics=("arbitrary", "arbitrary"), vmem_limit_bytes=VMEM_LIMIT),
        name="moba",
    )(qb, kb, vbt, kmean)


def _out_ffn_kernel(x_ref, oa_ref, ob_ref, ga_ref, gb_ref, mod_ref, gffn_ref,
                    wba_ref, wbb_ref, wo_ref, wgu_ref, wd_ref, o_ref, act_ref):
    merged = (ga_ref[0].astype(F32) * _dot(oa_ref[0], wba_ref[...])
              + gb_ref[0].astype(F32) * _dot(ob_ref[0], wbb_ref[...]))
    x1 = x_ref[0] + mod_ref[0, 2:3, :] * _dot(merged.astype(BF16), wo_ref[...])
    y = x1 * lax.rsqrt(jnp.mean(x1 * x1, axis=-1, keepdims=True) + EPS) * gffn_ref[...]
    h = (y * (1.0 + mod_ref[0, 4:5, :]) + mod_ref[0, 3:4, :]).astype(BF16)
    for c0 in range(0, D_FF, FF_CHUNK):
        g = _dot(h, wgu_ref[:, c0:c0 + FF_CHUNK])
        u = _dot(h, wgu_ref[:, D_FF + c0:D_FF + c0 + FF_CHUNK])
        act_ref[:, c0:c0 + FF_CHUNK] = (g * jax.nn.sigmoid(g) * u).astype(BF16)
    o_ref[0] = x1 + mod_ref[0, 5:6, :] * _dot(act_ref[...], wd_ref[...])


def _out_ffn_call(x, oa, ob, ga, gb, mod3, gffn, wba, wbb, wo, wgu, wd):
    bsz, seq, d = x.shape
    tm = TM_FFN
    const = lambda b, i: (0, 0)
    tok = lambda b, i: (b, i, 0)
    resident = lambda a: pl.BlockSpec(a.shape, const, pipeline_mode=pl.Buffered(1))
    return pl.pallas_call(
        _out_ffn_kernel,
        grid=(bsz, seq // tm),
        in_specs=[
            pl.BlockSpec((1, tm, d), tok),
            pl.BlockSpec((1, tm, BR_W), tok),
            pl.BlockSpec((1, tm, BR_W), tok),
            pl.BlockSpec((1, tm, d), tok),
            pl.BlockSpec((1, tm, d), tok),
            pl.BlockSpec((1, N_MOD, d), lambda b, i: (b, 0, 0)),
            pl.BlockSpec((1, d), const),
            resident(wba), resident(wbb), resident(wo), resident(wgu), resident(wd),
        ],
        out_specs=pl.BlockSpec((1, tm, d), tok),
        out_shape=jax.ShapeDtypeStruct((bsz, seq, d), F32),
        scratch_shapes=[pltpu.VMEM((tm, D_FF), BF16)],
        compiler_params=pltpu.CompilerParams(
            dimension_semantics=("arbitrary", "arbitrary"), vmem_limit_bytes=VMEM_LIMIT),
        name="out_ffn",
    )(x, oa, ob, ga, gb, mod3, gffn, wba, wbb, wo, wgu, wd)


def _rope_tables(seq, dim):
    inv = ROPE_THETA ** (-jnp.arange(0, dim, 2, dtype=F32) / dim)
    ang = jnp.arange(seq, dtype=F32)[:, None] * inv[None, :]
    cos, sin = jnp.cos(ang), jnp.sin(ang)
    reps = LANES // dim
    return (jnp.tile(jnp.concatenate([cos, cos], axis=-1), (1, reps)),
            jnp.tile(jnp.concatenate([-sin, sin], axis=-1), (1, reps)))


def _layer(x, mod, g_mix, g_ffn, w_in, g_qa, g_ka, g_qb, g_kb, w_br_a, w_br_b, w_out, w_gu, w_down):
    bsz, seq, d = x.shape
    mod3 = mod.reshape(bsz, N_MOD, d)

    o = 0
    cols = {}
    for name, width in (("qa", BR_W), ("ka", BR_W), ("va", BR_W), ("qi", IDX_HEADS * IDX_DIM),
                        ("ki", IDX_DIM), ("wi", IDX_HEADS), ("qb", BR_W), ("kb", BR_W), ("vb", BR_W),
                        ("ga", D_MODEL), ("gb", D_MODEL)):
        cols[name] = w_in[:, o:o + width]
        o += width
    wqk = jnp.concatenate([cols["qa"], cols["ka"], cols["qb"], cols["kb"]], axis=1).astype(BF16)
    wvt = jnp.concatenate([cols["va"], cols["vb"]], axis=1).T.astype(BF16)
    wg = jnp.concatenate([cols["ga"], cols["gb"]], axis=1).astype(BF16)
    pad = IDX_OUT_W - IDX_HEADS * IDX_DIM - IDX_DIM - IDX_HEADS
    wi = jnp.concatenate([cols["qi"], cols["ki"], cols["wi"], jnp.zeros((d, pad), F32)], axis=1).astype(BF16)
    gains = jnp.stack([g_qa, g_ka, g_qb, g_kb])
    cosh, sinh = _rope_tables(seq, HEAD_DIM)
    cosi, sini = _rope_tables(seq, IDX_DIM)

    qa, ka, qb, kb, vat, vbt, ga, gb, a_mat, kc, kw, kmean = _proj_call(
        x, mod3, g_mix.reshape(1, d), wqk, wvt, wg, wi, gains, cosh, sinh, cosi, sini)

    oa = _dsa_call(a_mat, kc, kw, qa, ka, vat)
    ob = _moba_call(qb, kb, vbt, kmean.reshape(bsz, seq // MOBA_BLOCK, BR_W))

    return _out_ffn_call(x, oa, ob, ga, gb, mod3, g_ffn.reshape(1, d),
                         w_br_a.astype(BF16), w_br_b.astype(BF16), w_out.astype(BF16),
                         w_gu.astype(BF16), w_down.astype(BF16))


def kernel(x, c, w_mod, b_mod, g_mix_norm, g_ffn_norm, w_in, g_q_dsa, g_k_dsa, g_q_moba, g_k_moba,
           w_br_dsa, w_br_moba, w_out, w_gate_up, w_down):
    assert x.shape[1] % TM_PROJ == 0 and x.shape[2] == D_MODEL and w_gate_up.shape[2] == 2 * D_FF
    for l in range(w_mod.shape[0]):
        mod = _mod_call(c, w_mod[l], b_mod[l])
        x = _layer(x, mod, g_mix_norm[l], g_ffn_norm[l], w_in[l], g_q_dsa[l], g_k_dsa[l], g_q_moba[l],
                   g_k_moba[l], w_br_dsa[l], w_br_moba[l], w_out[l], w_gate_up[l], w_down[l])
    return x
```

```python
import functools

import jax
import jax.numpy as jnp
from jax import lax
from jax.experimental import pallas as pl
from jax.experimental.pallas import tpu as pltpu

F32 = jnp.float32
BF16 = jnp.bfloat16
I32 = jnp.int32
I16 = jnp.int16

D_MODEL = 1024
HEAD_DIM = 128
N_HEADS = 4
BR_W = N_HEADS * HEAD_DIM
IDX_HEADS = 8
IDX_DIM = 64
DSA_TOPK_MAX = 256
MOBA_BLOCK = 256
MOBA_TOPK = 3
D_FF = 2816
ROPE_THETA = 10000.0
EPS = 1e-6
NEG = -1e30
LOG2_E = 1.4426950408889634
M_INIT = -1e29
N_MOD = 6
IDX_OUT_W = 640
IDX_K = 256
I16_MIN = -(2 ** 15)
BF16_KEY_NEG_INF = (0xFF80 ^ 0x7FFF) - 0x10000
BF16_KEY_POS_INF = 0x7F80
BISECT_STEPS = 17

LANES = 128
SUBLANES = 8
PACK_ROWS = 16
VT_ROWS = HEAD_DIM + PACK_ROWS
TQ = 256
TM_PROJ = 512
PROJ_ROWS = 256
TM_FFN = 512
FF_CHUNK = 256
SEL_ROWS = 32
KEEP = 3
SCORE_GROUPS = (4, 2, 1)
VMEM_LIMIT = 56 * 1024 * 1024

_NT = (((1,), (1,)), ((), ()))
_HEAD_SLICES = tuple(slice(h * HEAD_DIM, (h + 1) * HEAD_DIM) for h in range(N_HEADS))


def _split_bf16(x):
    hi = x.astype(BF16)
    lo = (x - hi.astype(F32)).astype(BF16)
    return hi, lo


def _dot(a, b):
    return jnp.dot(a, b, preferred_element_type=F32)


def _dot_nt(a, b):
    return lax.dot_general(a, b, _NT, preferred_element_type=F32)


def _mod_kernel(c_ref, w_ref, b_ref, o_ref):
    c = c_ref[...]
    a_hi, a_lo = _split_bf16(c * jax.nn.sigmoid(c))
    w_hi, w_lo = _split_bf16(w_ref[...])
    o_ref[...] = _dot(a_hi, w_hi) + _dot(a_lo, w_hi) + _dot(a_hi, w_lo) + b_ref[...]


def _mod_call(c, w_mod, b_mod):
    bsz, d = c.shape
    n = w_mod.shape[1]
    tn = 1024
    return pl.pallas_call(
        _mod_kernel,
        grid=(n // tn,),
        in_specs=[
            pl.BlockSpec((bsz, d), lambda i: (0, 0)),
            pl.BlockSpec((d, tn), lambda i: (0, i)),
            pl.BlockSpec((1, tn), lambda i: (0, i)),
        ],
        out_specs=pl.BlockSpec((bsz, tn), lambda i: (0, i)),
        out_shape=jax.ShapeDtypeStruct((bsz, n), F32),
        compiler_params=pltpu.CompilerParams(vmem_limit_bytes=VMEM_LIMIT),
        name="mod",
    )(c, w_mod, b_mod.reshape(1, n))


def _rope_partner_64(x):
    lane = lax.broadcasted_iota(I32, x.shape, 1)
    return jnp.where((lane % IDX_DIM) < IDX_DIM // 2,
                     pltpu.roll(x, LANES - IDX_DIM // 2, 1), pltpu.roll(x, IDX_DIM // 2, 1))


def _proj_kernel(x_ref, mod_ref, gmix_ref, wqk_ref, wvt_ref, wg_ref, wi_ref, gains_ref,
                 cosh_ref, sinh_ref, cosi_ref, sini_ref,
                 qa_ref, ka_ref, qb_ref, kb_ref, vat_ref, vbt_ref, ga_ref, gb_ref, a_ref, kc_ref, kw_ref, kmean_ref):
    for blk in range(x_ref.shape[1] // PROJ_ROWS):
        _proj_rows(blk, x_ref, mod_ref, gmix_ref, wqk_ref, wvt_ref, wg_ref, wi_ref, gains_ref,
                   cosh_ref, sinh_ref, cosi_ref, sini_ref,
                   qa_ref, ka_ref, qb_ref, kb_ref, vat_ref, vbt_ref, ga_ref, gb_ref, a_ref, kc_ref, kw_ref, kmean_ref)


def _proj_rows(blk, x_ref, mod_ref, gmix_ref, wqk_ref, wvt_ref, wg_ref, wi_ref, gains_ref,
               cosh_ref, sinh_ref, cosi_ref, sini_ref,
               qa_ref, ka_ref, qb_ref, kb_ref, vat_ref, vbt_ref, ga_ref, gb_ref, a_ref, kc_ref, kw_ref, kmean_ref):
    rows = slice(blk * PROJ_ROWS, (blk + 1) * PROJ_ROWS)
    x = x_ref[0, rows, :]
    y = x * lax.rsqrt(jnp.mean(x * x, axis=-1, keepdims=True) + EPS) * gmix_ref[...]
    h = y * (1.0 + mod_ref[0, 1:2, :]) + mod_ref[0, 0:1, :]
    h_hi = h.astype(BF16)
    cosh, sinh = cosh_ref[rows, :], sinh_ref[rows, :]
    scale = HEAD_DIM ** -0.5 * LOG2_E

    for gi, out_ref in enumerate((qa_ref, ka_ref, qb_ref, kb_ref)):
        p = _dot(h_hi, wqk_ref[:, gi * BR_W:(gi + 1) * BR_W])
        for hh in range(N_HEADS):
            ph = p[:, hh * HEAD_DIM:(hh + 1) * HEAD_DIM]
            yh = ph * lax.rsqrt(jnp.mean(ph * ph, axis=-1, keepdims=True) + EPS) * gains_ref[gi:gi + 1, :]
            r = yh * cosh + pltpu.roll(yh, HEAD_DIM // 2, 1) * sinh
            if gi == 3:
                kmean_ref[0, blk, :, hh * HEAD_DIM:(hh + 1) * HEAD_DIM] = jnp.mean(r, axis=0, keepdims=True)
            if gi % 2 == 0:
                r = r * scale
            out_ref[0, rows, hh * HEAD_DIM:(hh + 1) * HEAD_DIM] = r.astype(BF16)

    vt = _dot_nt(wvt_ref[...], h_hi)
    ones = jnp.ones((VT_ROWS - HEAD_DIM, TQ), BF16)
    for gi, out_ref in enumerate((vat_ref, vbt_ref)):
        for hh in range(N_HEADS):
            src = gi * BR_W + hh * HEAD_DIM
            out_ref[0, blk, hh * VT_ROWS:hh * VT_ROWS + HEAD_DIM, :] = vt[src:src + HEAD_DIM, :].astype(BF16)
            out_ref[0, blk, hh * VT_ROWS + HEAD_DIM:(hh + 1) * VT_ROWS, :] = ones

    for gi, out_ref in enumerate((ga_ref, gb_ref)):
        g = _dot(h_hi, wg_ref[:, gi * D_MODEL:(gi + 1) * D_MODEL])
        out_ref[0, rows, :] = jax.nn.sigmoid(g).astype(BF16)

    pi = _dot(h_hi, wi_ref[...])
    cosi, sini = cosi_ref[rows, :], sini_ref[rows, :]
    first = lax.broadcasted_iota(I32, (PROJ_ROWS, LANES), 1) < IDX_DIM
    half = LANES // 2

    def hi_lo(v):
        hi = v.astype(BF16).astype(F32)
        return hi, v - hi

    n_q = IDX_HEADS * IDX_DIM // LANES
    for g4 in range(n_q):
        xg = pi[:, g4 * LANES:(g4 + 1) * LANES]
        hi, lo = hi_lo(xg * cosi + _rope_partner_64(xg) * sini)
        hi_r, lo_r = pltpu.roll(hi, half, 1), pltpu.roll(lo, half, 1)
        for head, parts in ((2 * g4, (jnp.where(first, hi, lo_r), jnp.where(first, hi, 0.0))),
                            (2 * g4 + 1, (jnp.where(first, hi_r, lo), jnp.where(first, hi_r, 0.0)))):
            for pi_, part in enumerate(parts):
                a_ref[0, rows, head * IDX_K + pi_ * LANES:head * IDX_K + (pi_ + 1) * LANES] = part.astype(BF16)
    xg = pi[:, n_q * LANES:(n_q + 1) * LANES]
    roped = xg * cosi + _rope_partner_64(xg) * sini
    hi, lo = hi_lo(roped)
    kc_ref[0, rows, :LANES] = jnp.where(first, hi, pltpu.roll(hi, half, 1)).astype(BF16)
    kc_ref[0, rows, LANES:] = jnp.where(first, lo, 0.0).astype(BF16)
    w_scale = (IDX_HEADS ** -0.5) * (IDX_DIM ** -0.5)
    kw_ref[0, rows, :] = jnp.where(first, roped, xg * w_scale)


def _proj_call(x, mod3, gmix, wqk, wvt, wg, wi, gains, cosh, sinh, cosi, sini):
    bsz, seq, d = x.shape
    tm = TM_PROJ
    nb = seq // MOBA_BLOCK
    const = lambda b, i: (0, 0)
    tok = lambda b, i: (b, i, 0)
    tab = lambda b, i: (i, 0)
    resident = lambda a: pl.BlockSpec(a.shape, const, pipeline_mode=pl.Buffered(1))
    out_shape = (
        [jax.ShapeDtypeStruct((bsz, seq, BR_W), BF16)] * 4
        + [jax.ShapeDtypeStruct((bsz, seq // TQ, N_HEADS * VT_ROWS, TQ), BF16)] * 2
        + [jax.ShapeDtypeStruct((bsz, seq, D_MODEL), BF16)] * 2
        + [jax.ShapeDtypeStruct((bsz, seq, IDX_HEADS * IDX_K), BF16),
           jax.ShapeDtypeStruct((bsz, seq, IDX_K), BF16),
           jax.ShapeDtypeStruct((bsz, seq, LANES), F32),
           jax.ShapeDtypeStruct((bsz, nb, 1, BR_W), F32)]
    )
    out_specs = (
        [pl.BlockSpec((1, tm, BR_W), tok)] * 4
        + [pl.BlockSpec((1, tm // TQ, N_HEADS * VT_ROWS, TQ), lambda b, i: (b, i, 0, 0))] * 2
        + [pl.BlockSpec((1, tm, D_MODEL), tok)] * 2
        + [pl.BlockSpec((1, tm, IDX_HEADS * IDX_K), tok),
           pl.BlockSpec((1, tm, IDX_K), tok),
           pl.BlockSpec((1, tm, LANES), tok),
           pl.BlockSpec((1, tm // MOBA_BLOCK, 1, BR_W), lambda b, i: (b, i, 0, 0))]
    )
    return pl.pallas_call(
        _proj_kernel,
        grid=(bsz, seq // tm),
        in_specs=[
            pl.BlockSpec((1, tm, d), tok),
            pl.BlockSpec((1, N_MOD, d), lambda b, i: (b, 0, 0)),
            pl.BlockSpec((1, d), const),
            resident(wqk), resident(wvt), resident(wg), resident(wi),
            pl.BlockSpec(gains.shape, const),
            pl.BlockSpec((tm, LANES), tab),
            pl.BlockSpec((tm, LANES), tab),
            pl.BlockSpec((tm, LANES), tab),
            pl.BlockSpec((tm, LANES), tab),
        ],
        out_specs=out_specs,
        out_shape=out_shape,
        compiler_params=pltpu.CompilerParams(
            dimension_semantics=("arbitrary", "arbitrary"), vmem_limit_bytes=VMEM_LIMIT),
        name="proj",
    )(x, mod3, gmix, wqk, wvt, wg, wi, gains, cosh, sinh, cosi, sini)


def _tile3(x):
    return x.reshape(x.shape[0] // SUBLANES, SUBLANES, x.shape[1])


def _allreduce_sublanes(x, op):
    for shift in (4, 2, 1):
        x = op(x, pltpu.roll(x, shift, 0))
    return x


def _softmax_tile(s3, vt, m_old, acc_ref, hh):
    rows = slice(hh * VT_ROWS, (hh + 1) * VT_ROWS)
    m_new = jnp.maximum(m_old, _allreduce_sublanes(jnp.max(s3, axis=0), jnp.maximum))
    alpha = jnp.exp2(m_old - m_new)
    p = jnp.exp2(s3 - m_new[None]).reshape(s3.shape[0] * SUBLANES, TQ).astype(BF16)
    acc_ref[rows, :] = (_tile3(acc_ref[rows, :]) * alpha[None]).reshape(VT_ROWS, TQ) + _dot(vt, p)
    return m_new


def _qk_tile(k_ref, q_ref, t, s_ref):
    k0 = pl.multiple_of(t * TQ, TQ)
    for hh, hs in enumerate(_HEAD_SLICES):
        s_ref[hh] = _dot_nt(k_ref[0, pl.ds(k0, TQ), hs], q_ref[0, :, hs])


def _causal_triangle():
    shape = (TQ // SUBLANES, SUBLANES, TQ)
    row = lax.broadcasted_iota(I32, shape, 0) * SUBLANES + lax.broadcasted_iota(I32, shape, 1)
    return row <= lax.broadcasted_iota(I32, shape, 2)


def _edge_mask(j, t_true):
    shape = (TQ // SUBLANES, SUBLANES, TQ)
    row = lax.broadcasted_iota(I32, shape, 0) * SUBLANES + lax.broadcasted_iota(I32, shape, 1)
    return row - lax.broadcasted_iota(I32, shape, 2) <= (j - t_true) * TQ


def _sweep_tiles(j, s_refs, qk, consume, state):
    s0, s1 = s_refs
    last_pair = j // 2
    qk(0, s0)

    def body(p, st):
        t0 = 2 * p
        qk(t0 + 1, s1)
        st = consume(t0, t0, s0, st, False)
        qk(t0 + 2, s0)
        return consume(t0 + 1, t0 + 1, s1, st, False)

    state = lax.fori_loop(0, last_pair // 2, lambda q, st: body(2 * q + 1, body(2 * q, st)), state)
    state = lax.fori_loop(2 * (last_pair // 2), last_pair, body, state)
    ta = 2 * last_pair
    tb = jnp.minimum(ta + 1, j)
    qk(tb, s1)
    state = consume(ta, ta, s0, state, True)
    return consume(tb, ta + 1, s1, state, True)


def _init_state(acc_ref):
    acc_ref[...] = jnp.zeros(acc_ref.shape, F32)
    return (jnp.full((SUBLANES, TQ), M_INIT, F32),) * N_HEADS


def _write_heads(o_ref, acc_ref):
    for hh, hs in enumerate(_HEAD_SLICES):
        base = hh * VT_ROWS
        inv = 1.0 / acc_ref[base + HEAD_DIM:base + HEAD_DIM + SUBLANES, :]
        o_t = (_tile3(acc_ref[base:base + HEAD_DIM, :]) * inv[None]).reshape(HEAD_DIM, TQ)
        o_ref[0, :, hs] = o_t.T.astype(BF16)


def _f32_from_key(key):
    bits = jnp.where(key < 0, key ^ jnp.int32(0x7FFFFFFF), key)
    return lax.bitcast_convert_type(bits, F32)


def _key32_of_key16(key16):
    return (key16 << 16) + jnp.where(key16 < 0, 0xFFFF, 0)


def _bf16_from_key16(key16):
    return _f32_from_key(_key32_of_key16(key16))


def _dsa_kernel(a_ref, kc_ref, wk_ref, qa_ref, ka_ref, vat_ref, o_ref,
                scores_ref, hb_ref, red_ref, thr_ref, cge_ref, s0_ref, s1_ref, acc_ref, *, topk, seq_bits):
    j = pl.program_id(1)
    n_tiles = j + 1
    w_rows = wk_ref[0].T[IDX_DIM:IDX_DIM + IDX_HEADS, :]
    w8 = [jnp.broadcast_to(w_rows[hh:hh + 1, :], (SUBLANES, TQ)) for hh in range(IDX_HEADS)]

    def score_tile(c, diagonal):
        k0 = pl.multiple_of(c * TQ, TQ)
        kc = kc_ref[0, pl.ds(k0, TQ), :]
        acc = jnp.zeros((TQ // SUBLANES, SUBLANES, TQ), F32)
        for hh in range(IDX_HEADS):
            lg = _dot_nt(kc, a_ref[0, :, hh * IDX_K:(hh + 1) * IDX_K])
            acc = acc + w8[hh][None] * jnp.maximum(_tile3(lg), 0.0)
        if diagonal:
            acc = jnp.where(_causal_triangle(), acc, NEG)
        acc = acc.reshape(TQ, TQ)
        scores_ref[pl.ds(k0, TQ), :] = acc
        hb_ref[pl.ds(k0, TQ), :] = acc.astype(BF16)

    def score_run(start, n_trips, width):
        def trip(g, carry):
            for u in range(width):
                score_tile(start + width * g + u, False)
            return carry

        lax.fori_loop(0, n_trips, trip, 0)
        return start + width * n_trips

    done = 0
    for width in SCORE_GROUPS:
        done = score_run(done, (j - done) // width, width)
    score_tile(j, True)

    def load_packed(c, u):
        r0 = pl.multiple_of(c * TQ + u * SEL_ROWS, SEL_ROWS)
        return hb_ref[pl.ds(r0, SEL_ROWS), :].reshape(SEL_ROWS // PACK_ROWS, PACK_ROWS, TQ)

    def load_scores(c, u):
        r0 = pl.multiple_of(c * TQ + u * SEL_ROWS, SEL_ROWS)
        return _tile3(scores_ref[pl.ds(r0, SEL_ROWS), :]), r0

    def sweep_count(tile, zero):
        cnt = lax.fori_loop(0, n_tiles // 2, lambda p, cnt: tile(2 * p + 1, tile(2 * p, cnt)), zero)
        return lax.fori_loop(2 * (n_tiles // 2), n_tiles, tile, cnt)

    one16, zero16 = jnp.int16(1), jnp.int16(0)

    def count_bf16(test):
        t = jnp.broadcast_to(test, (PACK_ROWS, TQ)).astype(BF16)[None]

        def tile(c, cnt):
            for u in range(TQ // SEL_ROWS):
                cnt = cnt + jnp.where(load_packed(c, u) >= t, one16, zero16)
            return cnt

        cnt = sweep_count(tile, jnp.zeros((SEL_ROWS // PACK_ROWS, PACK_ROWS, TQ), I16))
        return jnp.sum(jnp.sum(cnt.astype(F32), axis=0), axis=0, keepdims=True)

    def count_f32(pred):
        def tile(c, cnt):
            for u in range(TQ // SEL_ROWS):
                sc, r0 = load_scores(c, u)
                cnt = cnt + jnp.where(pred(sc, r0), 1.0, 0.0)
            return cnt

        cnt = sweep_count(tile, jnp.zeros((SEL_ROWS // SUBLANES, SUBLANES, TQ), F32))
        return jnp.sum(jnp.sum(cnt, axis=0), axis=0, keepdims=True)

    kf = float(topk)

    c0 = count_bf16(jnp.zeros((1, TQ), F32))
    start = jnp.where(c0 >= kf, 0, I16_MIN).astype(I32)

    def bit_step(i, val):
        test = val | (jnp.int32(1) << (14 - i))
        return jnp.where(count_bf16(_bf16_from_key16(test)) >= kf, test, val)

    hb_k = lax.fori_loop(0, 15, bit_step, start)

    below = _key32_of_key16(jnp.maximum(hb_k - 1, BF16_KEY_NEG_INF))
    lo = below + ((_key32_of_key16(hb_k) - below) >> 1) - 1
    hi = _key32_of_key16(jnp.minimum(hb_k + 1, BF16_KEY_POS_INF))

    def bisect(count_ge):
        def step(i, carry):
            lo, hi, cge = carry
            mid = lo + ((hi - lo) >> 1)
            c = count_ge(_f32_from_key(mid)[None])
            ok = c >= kf
            return jnp.where(ok, mid, lo), jnp.where(ok, hi, mid), jnp.where(ok, c, cge)

        lo_k, _, cge = lax.fori_loop(0, BISECT_STEPS, step, (lo, hi, jnp.full((1, TQ), kf, F32)))
        thr_ref[...] = jnp.broadcast_to(_f32_from_key(lo_k), (SUBLANES, TQ))
        cge_ref[...] = jnp.broadcast_to(cge, (SUBLANES, TQ))

    lo_f, hi_f = _f32_from_key(lo), _f32_from_key(hi)
    ninf = jnp.full((SUBLANES, TQ), -jnp.inf, F32)

    def reduce_tile(c, carry):
        above, spill = carry
        k0 = pl.multiple_of(c * TQ, TQ)
        x3 = _tile3(scores_ref[pl.ds(k0, TQ), :])
        top = [ninf] * KEEP
        for g in range(TQ // SUBLANES):
            x = x3[g]
            ge_hi = x >= hi_f
            above = above + jnp.where(ge_hi, 1.0, 0.0)
            t = jnp.where((x >= lo_f) & ~ge_hi, x, -jnp.inf)
            for r in range(KEEP):
                top[r], t = jnp.maximum(top[r], t), jnp.minimum(top[r], t)
            spill = jnp.maximum(spill, t)
        for r in range(KEEP):
            red_ref[c, r] = top[r]
        return above, spill

    above, spill = lax.fori_loop(0, n_tiles, reduce_tile, (jnp.zeros((SUBLANES, TQ), F32), ninf))
    above = jnp.sum(above, axis=0, keepdims=True)
    reduced_ok = jnp.max(spill) == -jnp.inf

    @pl.when(reduced_ok)
    def _():
        def count_ge(t3):
            def tile(c, cnt):
                for r in range(KEEP):
                    cnt = cnt + jnp.where(red_ref[c, r] >= t3[0], 1.0, 0.0)
                return cnt
            cnt = sweep_count(tile, jnp.zeros((SUBLANES, TQ), F32))
            return above + jnp.sum(cnt, axis=0, keepdims=True)
        bisect(count_ge)

    @pl.when(jnp.logical_not(reduced_ok))
    def _():
        bisect(lambda t3: count_f32(lambda sc, r0: sc >= t3))

    thr = thr_ref[0:1, :]
    cge = cge_ref[0:1, :]

    @pl.when(jnp.max(jnp.abs(cge - kf)) > 0.0)
    def _():
        thr3 = thr[None]
        need = kf - count_f32(lambda sc, r0: sc > thr3)
        sub = (lax.broadcasted_iota(I32, (SEL_ROWS // SUBLANES, SUBLANES, TQ), 0) * SUBLANES
               + lax.broadcasted_iota(I32, (SEL_ROWS // SUBLANES, SUBLANES, TQ), 1))

        def cut_step(i, cut):
            test = cut | (jnp.int32(1) << (seq_bits - 1 - i))
            f = count_f32(lambda sc, r0: (sc == thr3) & (r0 + sub < test[None]))
            return jnp.where(f < need, test, cut)

        cut = lax.fori_loop(0, seq_bits, cut_step, jnp.zeros((1, TQ), I32))

        def demote(c, carry):
            r0 = pl.multiple_of(c * SEL_ROWS, SEL_ROWS)
            sc = _tile3(scores_ref[pl.ds(r0, SEL_ROWS), :])
            scores_ref[pl.ds(r0, SEL_ROWS), :] = jnp.where(
                (sc == thr3) & (r0 + sub > cut[None]), NEG, sc).reshape(SEL_ROWS, TQ)
            return carry

        lax.fori_loop(0, n_tiles * (TQ // SEL_ROWS), demote, 0)

    def consume(t, t_true, s_ref, state, edge):
        k0 = pl.multiple_of(t * TQ, TQ)
        sel = _tile3(scores_ref[pl.ds(k0, TQ), :]) >= thr[None]
        if edge:
            sel = sel & _edge_mask(j, t_true)
        bias = jnp.where(sel, 0.0, NEG)
        return tuple(
            _softmax_tile(_tile3(s_ref[hh]) + bias, vat_ref[0, t, hh * VT_ROWS:(hh + 1) * VT_ROWS, :],
                          state[hh], acc_ref, hh)
            for hh in range(N_HEADS))

    qk = functools.partial(_qk_tile, ka_ref, qa_ref)
    _sweep_tiles(j, (s0_ref, s1_ref), qk, consume, _init_state(acc_ref))
    _write_heads(o_ref, acc_ref)


def _dsa_call(a_mat, kc, idx_out, qa, ka, vat):
    bsz, seq, _ = qa.shape
    topk = min(DSA_TOPK_MAX, seq // 4)
    seq_bits = (seq - 1).bit_length()
    kern = functools.partial(_dsa_kernel, topk=topk, seq_bits=seq_bits)
    return pl.pallas_call(
        kern,
        grid=(bsz, seq // TQ),
        in_specs=[
            pl.BlockSpec((1, TQ, IDX_HEADS * IDX_K), lambda b, j: (b, j, 0)),
            pl.BlockSpec((1, seq, IDX_K), lambda b, j: (b, 0, 0)),
            pl.BlockSpec((1, TQ, LANES), lambda b, j: (b, j, 0)),
            pl.BlockSpec((1, TQ, BR_W), lambda b, j: (b, j, 0)),
            pl.BlockSpec((1, seq, BR_W), lambda b, j: (b, 0, 0)),
            pl.BlockSpec((1, seq // TQ, N_HEADS * VT_ROWS, TQ), lambda b, j: (b, 0, 0, 0)),
        ],
        out_specs=pl.BlockSpec((1, TQ, BR_W), lambda b, j: (b, j, 0)),
        out_shape=jax.ShapeDtypeStruct((bsz, seq, BR_W), BF16),
        scratch_shapes=[
            pltpu.VMEM((seq, TQ), F32),
            pltpu.VMEM((seq, TQ), BF16),
            pltpu.VMEM((seq // TQ, KEEP, SUBLANES, TQ), F32),
            pltpu.VMEM((SUBLANES, TQ), F32),
            pltpu.VMEM((SUBLANES, TQ), F32),
            pltpu.VMEM((N_HEADS, TQ, TQ), F32),
            pltpu.VMEM((N_HEADS, TQ, TQ), F32),
            pltpu.VMEM((N_HEADS * VT_ROWS, TQ), F32),
        ],
        compiler_params=pltpu.CompilerParams(
            dimension_semantics=("arbitrary", "arbitrary"), vmem_limit_bytes=VMEM_LIMIT),
        name="dsa",
    )(a_mat, kc, idx_out, qa, ka, vat)


def _moba_kernel(qb_ref, kb_ref, vbt_ref, kmean_ref, o_ref, bias_ref, s0_ref, s1_ref, acc_ref, *, n_sel):
    j = pl.program_id(1)
    nb = kmean_ref.shape[1]
    blk = lax.broadcasted_iota(I32, (nb, TQ), 0).astype(F32)
    jf = j.astype(F32)

    for hh in range(N_HEADS):
        hs = slice(hh * HEAD_DIM, (hh + 1) * HEAD_DIM)
        km_hi, km_lo = _split_bf16(kmean_ref[0, :, hs])
        q = qb_ref[0, :, hs]
        gate = jnp.where(blk < jf, _dot_nt(km_hi, q) + _dot_nt(km_lo, q), NEG)
        chosen = jnp.zeros((nb, TQ), F32)
        for _ in range(n_sel):
            best = jnp.max(gate, axis=0, keepdims=True)
            first = jnp.min(jnp.where(gate == best, blk, float(nb)), axis=0, keepdims=True)
            hit = blk == first
            chosen = jnp.where(hit, 1.0, chosen)
            gate = jnp.where(hit, -jnp.inf, gate)
        bias = jnp.where((chosen > 0.0) & (blk < jf), 0.0, NEG)
        for n in range(nb):
            bias_ref[hh, n] = jnp.broadcast_to(bias[n:n + 1, :], (SUBLANES, TQ))

    def consume(t, t_true, s_ref, state, edge):
        new = []
        for hh in range(N_HEADS):
            bias = bias_ref[hh, t]
            if edge:
                bias = jnp.where(t_true < j, bias, 0.0)
            s3 = _tile3(s_ref[hh]) + bias[None]
            if edge:
                s3 = jnp.where(_edge_mask(j, t_true), s3, NEG)
            new.append(_softmax_tile(s3, vbt_ref[0, t, hh * VT_ROWS:(hh + 1) * VT_ROWS, :], state[hh], acc_ref, hh))
        return tuple(new)

    qk = functools.partial(_qk_tile, kb_ref, qb_ref)
    _sweep_tiles(j, (s0_ref, s1_ref), qk, consume, _init_state(acc_ref))
    _write_heads(o_ref, acc_ref)


def _moba_call(qb, kb, vbt, kmean):
    bsz, seq, _ = qb.shape
    nb = seq // MOBA_BLOCK
    kern = functools.partial(_moba_kernel, n_sel=min(MOBA_TOPK, nb - 1))
    return pl.pallas_call(
        kern,
        grid=(bsz, seq // TQ),
        in_specs=[
            pl.BlockSpec((1, TQ, BR_W), lambda b, j: (b, j, 0)),
            pl.BlockSpec((1, seq, BR_W), lambda b, j: (b, 0, 0)),
            pl.BlockSpec((1, seq // TQ, N_HEADS * VT_ROWS, TQ), lambda b, j: (b, 0, 0, 0)),
            pl.BlockSpec((1, nb, BR_W), lambda b, j: (b, 0, 0)),
        ],
        out_specs=pl.BlockSpec((1, TQ, BR_W), lambda b, j: (b, j, 0)),
        out_shape=jax.ShapeDtypeStruct((bsz, seq, BR_W), BF16),
        scratch_shapes=[
            pltpu.VMEM((N_HEADS, nb, SUBLANES, TQ), F32),
            pltpu.VMEM((N_HEADS, TQ, TQ), F32),
            pltpu.VMEM((N_HEADS, TQ, TQ), F32),
            pltpu.VMEM((N_HEADS * VT_ROWS, TQ), F32),
        ],
        compiler_params=pltpu.CompilerParams(
            dimension_semantics=("arbitrary", "arbitrary"), vmem_limit_bytes=VMEM_LIMIT),
        name="moba",
    )(qb, kb, vbt, kmean)


def _out_ffn_kernel(x_ref, oa_ref, ob_ref, ga_ref, gb_ref, mod_ref, gffn_ref,
                    wba_ref, wbb_ref, wo_ref, wgu_ref, wd_ref, o_ref, act_ref):
    merged = (ga_ref[0].astype(F32) * _dot(oa_ref[0], wba_ref[...])
              + gb_ref[0].astype(F32) * _dot(ob_ref[0], wbb_ref[...]))
    x1 = x_ref[0] + mod_ref[0, 2:3, :] * _dot(merged.astype(BF16), wo_ref[...])
    y = x1 * lax.rsqrt(jnp.mean(x1 * x1, axis=-1, keepdims=True) + EPS) * gffn_ref[...]
    h = (y * (1.0 + mod_ref[0, 4:5, :]) + mod_ref[0, 3:4, :]).astype(BF16)
    for c0 in range(0, D_FF, FF_CHUNK):
        g = _dot(h, wgu_ref[:, c0:c0 + FF_CHUNK])
        u = _dot(h, wgu_ref[:, D_FF + c0:D_FF + c0 + FF_CHUNK])
        act_ref[:, c0:c0 + FF_CHUNK] = (g * jax.nn.sigmoid(g) * u).astype(BF16)
    o_ref[0] = x1 + mod_ref[0, 5:6, :] * _dot(act_ref[...], wd_ref[...])


def _out_ffn_call(x, oa, ob, ga, gb, mod3, gffn, wba, wbb, wo, wgu, wd):
    bsz, seq, d = x.shape
    tm = TM_FFN
    const = lambda b, i: (0, 0)
    tok = lambda b, i: (b, i, 0)
    resident = lambda a: pl.BlockSpec(a.shape, const, pipeline_mode=pl.Buffered(1))
    return pl.pallas_call(
        _out_ffn_kernel,
        grid=(bsz, seq // tm),
        in_specs=[
            pl.BlockSpec((1, tm, d), tok),
            pl.BlockSpec((1, tm, BR_W), tok),
            pl.BlockSpec((1, tm, BR_W), tok),
            pl.BlockSpec((1, tm, d), tok),
            pl.BlockSpec((1, tm, d), tok),
            pl.BlockSpec((1, N_MOD, d), lambda b, i: (b, 0, 0)),
            pl.BlockSpec((1, d), const),
            resident(wba), resident(wbb), resident(wo), resident(wgu), resident(wd),
        ],
        out_specs=pl.BlockSpec((1, tm, d), tok),
        out_shape=jax.ShapeDtypeStruct((bsz, seq, d), F32),
        scratch_shapes=[pltpu.VMEM((tm, D_FF), BF16)],
        compiler_params=pltpu.CompilerParams(
            dimension_semantics=("arbitrary", "arbitrary"), vmem_limit_bytes=VMEM_LIMIT),
        name="out_ffn",
    )(x, oa, ob, ga, gb, mod3, gffn, wba, wbb, wo, wgu, wd)


def _rope_tables(seq, dim):
    inv = ROPE_THETA ** (-jnp.arange(0, dim, 2, dtype=F32) / dim)
    ang = jnp.arange(seq, dtype=F32)[:, None] * inv[None, :]
    cos, sin = jnp.cos(ang), jnp.sin(ang)
    reps = LANES // dim
    return (jnp.tile(jnp.concatenate([cos, cos], axis=-1), (1, reps)),
            jnp.tile(jnp.concatenate([-sin, sin], axis=-1), (1, reps)))


def _layer(x, mod, g_mix, g_ffn, w_in, g_qa, g_ka, g_qb, g_kb, w_br_a, w_br_b, w_out, w_gu, w_down):
    bsz, seq, d = x.shape
    mod3 = mod.reshape(bsz, N_MOD, d)

    o = 0
    cols = {}
    for name, width in (("qa", BR_W), ("ka", BR_W), ("va", BR_W), ("qi", IDX_HEADS * IDX_DIM),
                        ("ki", IDX_DIM), ("wi", IDX_HEADS), ("qb", BR_W), ("kb", BR_W), ("vb", BR_W),
                        ("ga", D_MODEL), ("gb", D_MODEL)):
        cols[name] = w_in[:, o:o + width]
        o += width
    wqk = jnp.concatenate([cols["qa"], cols["ka"], cols["qb"], cols["kb"]], axis=1).astype(BF16)
    wvt = jnp.concatenate([cols["va"], cols["vb"]], axis=1).T.astype(BF16)
    wg = jnp.concatenate([cols["ga"], cols["gb"]], axis=1).astype(BF16)
    pad = IDX_OUT_W - IDX_HEADS * IDX_DIM - IDX_DIM - IDX_HEADS
    wi = jnp.concatenate([cols["qi"], cols["ki"], cols["wi"], jnp.zeros((d, pad), F32)], axis=1).astype(BF16)
    gains = jnp.stack([g_qa, g_ka, g_qb, g_kb])
    cosh, sinh = _rope_tables(seq, HEAD_DIM)
    cosi, sini = _rope_tables(seq, IDX_DIM)

    qa, ka, qb, kb, vat, vbt, ga, gb, a_mat, kc, kw, kmean = _proj_call(
        x, mod3, g_mix.reshape(1, d), wqk, wvt, wg, wi, gains, cosh, sinh, cosi, sini)

    oa = _dsa_call(a_mat, kc, kw, qa, ka, vat)
    ob = _moba_call(qb, kb, vbt, kmean.reshape(bsz, seq // MOBA_BLOCK, BR_W))

    return _out_ffn_call(x, oa, ob, ga, gb, mod3, g_ffn.reshape(1, d),
                         w_br_a.astype(BF16), w_br_b.astype(BF16), w_out.astype(BF16),
                         w_gu.astype(BF16), w_down.astype(BF16))


def kernel(x, c, w_mod, b_mod, g_mix_norm, g_ffn_norm, w_in, g_q_dsa, g_k_dsa, g_q_moba, g_k_moba,
           w_br_dsa, w_br_moba, w_out, w_gate_up, w_down):
    assert x.shape[1] % TM_PROJ == 0 and x.shape[2] == D_MODEL and w_gate_up.shape[2] == 2 * D_FF
    for l in range(w_mod.shape[0]):
        mod = _mod_call(c, w_mod[l], b_mod[l])
        x = _layer(x, mod, g_mix_norm[l], g_ffn_norm[l], w_in[l], g_q_dsa[l], g_k_dsa[l], g_q_moba[l],
                   g_k_moba[l], w_br_dsa[l], w_br_moba[l], w_out[l], w_gate_up[l], w_down[l])
    return x
```

```python
import functools

import jax
import jax.numpy as jnp
from jax import lax
from jax.experimental import pallas as pl
from jax.experimental.pallas import tpu as pltpu

F32 = jnp.float32
BF16 = jnp.bfloat16
I32 = jnp.int32
I16 = jnp.int16

D_MODEL = 1024
HEAD_DIM = 128
N_HEADS = 4
BR_W = N_HEADS * HEAD_DIM
IDX_HEADS = 8
IDX_DIM = 64
DSA_TOPK_MAX = 256
MOBA_BLOCK = 256
MOBA_TOPK = 3
D_FF = 2816
ROPE_THETA = 10000.0
EPS = 1e-6
NEG = -1e30
LOG2_E = 1.4426950408889634
M_INIT = -1e29
N_MOD = 6
IDX_OUT_W = 640
IDX_K = 256
I16_MIN = -(2 ** 15)
BF16_KEY_NEG_INF = (0xFF80 ^ 0x7FFF) - 0x10000
BF16_KEY_POS_INF = 0x7F80
BISECT_STEPS = 17

LANES = 128
SUBLANES = 8
PACK_ROWS = 16
VT_ROWS = HEAD_DIM + PACK_ROWS
TQ = 256
TM_PROJ = 512
PROJ_ROWS = 256
TM_FFN = 512
FF_CHUNK = 256
SEL_ROWS = 32
KEEP = 3
SCORE_GROUPS = (4, 2, 1)
VMEM_LIMIT = 56 * 1024 * 1024

_NT = (((1,), (1,)), ((), ()))
_TN = (((0,), (1,)), ((), ()))
_HEAD_SLICES = tuple(slice(h * HEAD_DIM, (h + 1) * HEAD_DIM) for h in range(N_HEADS))


def _split_bf16(x):
    hi = x.astype(BF16)
    lo = (x - hi.astype(F32)).astype(BF16)
    return hi, lo


def _dot(a, b):
    return jnp.dot(a, b, preferred_element_type=F32)


def _dot_nt(a, b):
    return lax.dot_general(a, b, _NT, preferred_element_type=F32)


def _mod_kernel(c_ref, w_ref, b_ref, o_ref):
    c = c_ref[...]
    a_hi, a_lo = _split_bf16(c * jax.nn.sigmoid(c))
    w_hi, w_lo = _split_bf16(w_ref[...])
    o_ref[...] = _dot(a_hi, w_hi) + _dot(a_lo, w_hi) + _dot(a_hi, w_lo) + b_ref[...]


def _mod_call(c, w_mod, b_mod):
    bsz, d = c.shape
    n = w_mod.shape[1]
    tn = 1024
    return pl.pallas_call(
        _mod_kernel,
        grid=(n // tn,),
        in_specs=[
            pl.BlockSpec((bsz, d), lambda i: (0, 0)),
            pl.BlockSpec((d, tn), lambda i: (0, i)),
            pl.BlockSpec((1, tn), lambda i: (0, i)),
        ],
        out_specs=pl.BlockSpec((bsz, tn), lambda i: (0, i)),
        out_shape=jax.ShapeDtypeStruct((bsz, n), F32),
        compiler_params=pltpu.CompilerParams(vmem_limit_bytes=VMEM_LIMIT),
        name="mod",
    )(c, w_mod, b_mod.reshape(1, n))


def _rope_partner_64(x):
    lane = lax.broadcasted_iota(I32, x.shape, 1)
    return jnp.where((lane % IDX_DIM) < IDX_DIM // 2,
                     pltpu.roll(x, LANES - IDX_DIM // 2, 1), pltpu.roll(x, IDX_DIM // 2, 1))


def _proj_kernel(x_ref, mod_ref, gmix_ref, wqk_ref, wv_ref, wg_ref, wi_ref, gains_ref,
                 cosh_ref, sinh_ref, cosi_ref, sini_ref,
                 qa_ref, ka_ref, qb_ref, kb_ref, vat_ref, vbt_ref, ga_ref, gb_ref, a_ref, kc_ref, kw_ref, kmean_ref):
    for blk in range(x_ref.shape[1] // PROJ_ROWS):
        _proj_rows(blk, x_ref, mod_ref, gmix_ref, wqk_ref, wv_ref, wg_ref, wi_ref, gains_ref,
                   cosh_ref, sinh_ref, cosi_ref, sini_ref,
                   qa_ref, ka_ref, qb_ref, kb_ref, vat_ref, vbt_ref, ga_ref, gb_ref, a_ref, kc_ref, kw_ref, kmean_ref)


def _proj_rows(blk, x_ref, mod_ref, gmix_ref, wqk_ref, wv_ref, wg_ref, wi_ref, gains_ref,
               cosh_ref, sinh_ref, cosi_ref, sini_ref,
               qa_ref, ka_ref, qb_ref, kb_ref, vat_ref, vbt_ref, ga_ref, gb_ref, a_ref, kc_ref, kw_ref, kmean_ref):
    rows = slice(blk * PROJ_ROWS, (blk + 1) * PROJ_ROWS)
    x = x_ref[0, rows, :]
    y = x * lax.rsqrt(jnp.mean(x * x, axis=-1, keepdims=True) + EPS) * gmix_ref[...]
    h = y * (1.0 + mod_ref[0, 1:2, :]) + mod_ref[0, 0:1, :]
    h_hi = h.astype(BF16)
    cosh, sinh = cosh_ref[rows, :], sinh_ref[rows, :]
    scale = HEAD_DIM ** -0.5 * LOG2_E

    for gi, out_ref in enumerate((qa_ref, ka_ref, qb_ref, kb_ref)):
        p = _dot(h_hi, wqk_ref[:, gi * BR_W:(gi + 1) * BR_W])
        for hh in range(N_HEADS):
            ph = p[:, hh * HEAD_DIM:(hh + 1) * HEAD_DIM]
            yh = ph * lax.rsqrt(jnp.mean(ph * ph, axis=-1, keepdims=True) + EPS) * gains_ref[gi:gi + 1, :]
            r = yh * cosh + pltpu.roll(yh, HEAD_DIM // 2, 1) * sinh
            if gi == 3:
                kmean_ref[0, blk, :, hh * HEAD_DIM:(hh + 1) * HEAD_DIM] = jnp.mean(r, axis=0, keepdims=True)
            if gi % 2 == 0:
                r = r * scale
            out_ref[0, rows, hh * HEAD_DIM:(hh + 1) * HEAD_DIM] = r.astype(BF16)

    vt = lax.dot_general(wv_ref[...], h_hi, _TN, preferred_element_type=F32)
    ones = jnp.ones((VT_ROWS - HEAD_DIM, TQ), BF16)
    for gi, out_ref in enumerate((vat_ref, vbt_ref)):
        for hh in range(N_HEADS):
            src = gi * BR_W + hh * HEAD_DIM
            out_ref[0, blk, hh * VT_ROWS:hh * VT_ROWS + HEAD_DIM, :] = vt[src:src + HEAD_DIM, :].astype(BF16)
            out_ref[0, blk, hh * VT_ROWS + HEAD_DIM:(hh + 1) * VT_ROWS, :] = ones

    for gi, out_ref in enumerate((ga_ref, gb_ref)):
        g = _dot(h_hi, wg_ref[:, gi * D_MODEL:(gi + 1) * D_MODEL])
        out_ref[0, rows, :] = jax.nn.sigmoid(g).astype(BF16)

    pi = _dot(h_hi, wi_ref[...])
    cosi, sini = cosi_ref[rows, :], sini_ref[rows, :]
    first = lax.broadcasted_iota(I32, (PROJ_ROWS, LANES), 1) < IDX_DIM
    half = LANES // 2

    def hi_lo(v):
        hi = v.astype(BF16).astype(F32)
        return hi, v - hi

    n_q = IDX_HEADS * IDX_DIM // LANES
    for g4 in range(n_q):
        xg = pi[:, g4 * LANES:(g4 + 1) * LANES]
        hi, lo = hi_lo(xg * cosi + _rope_partner_64(xg) * sini)
        hi_r, lo_r = pltpu.roll(hi, half, 1), pltpu.roll(lo, half, 1)
        for head, parts in ((2 * g4, (jnp.where(first, hi, lo_r), jnp.where(first, hi, 0.0))),
                            (2 * g4 + 1, (jnp.where(first, hi_r, lo), jnp.where(first, hi_r, 0.0)))):
            for pi_, part in enumerate(parts):
                a_ref[0, rows, head * IDX_K + pi_ * LANES:head * IDX_K + (pi_ + 1) * LANES] = part.astype(BF16)
    xg = pi[:, n_q * LANES:(n_q + 1) * LANES]
    roped = xg * cosi + _rope_partner_64(xg) * sini
    hi, lo = hi_lo(roped)
    kc_ref[0, rows, :LANES] = jnp.where(first, hi, pltpu.roll(hi, half, 1)).astype(BF16)
    kc_ref[0, rows, LANES:] = jnp.where(first, lo, 0.0).astype(BF16)
    w_scale = (IDX_HEADS ** -0.5) * (IDX_DIM ** -0.5)
    kw_ref[0, rows, :] = jnp.where(first, roped, xg * w_scale)


def _proj_call(x, mod3, gmix, wqk, wv, wg, wi, gains, cosh, sinh, cosi, sini):
    bsz, seq, d = x.shape
    tm = TM_PROJ
    nb = seq // MOBA_BLOCK
    const = lambda b, i: (0, 0)
    tok = lambda b, i: (b, i, 0)
    tab = lambda b, i: (i, 0)
    resident = lambda a: pl.BlockSpec(a.shape, const, pipeline_mode=pl.Buffered(1))
    out_shape = (
        [jax.ShapeDtypeStruct((bsz, seq, BR_W), BF16)] * 4
        + [jax.ShapeDtypeStruct((bsz, seq // TQ, N_HEADS * VT_ROWS, TQ), BF16)] * 2
        + [jax.ShapeDtypeStruct((bsz, seq, D_MODEL), BF16)] * 2
        + [jax.ShapeDtypeStruct((bsz, seq, IDX_HEADS * IDX_K), BF16),
           jax.ShapeDtypeStruct((bsz, seq, IDX_K), BF16),
           jax.ShapeDtypeStruct((bsz, seq, LANES), F32),
           jax.ShapeDtypeStruct((bsz, nb, 1, BR_W), F32)]
    )
    out_specs = (
        [pl.BlockSpec((1, tm, BR_W), tok)] * 4
        + [pl.BlockSpec((1, tm // TQ, N_HEADS * VT_ROWS, TQ), lambda b, i: (b, i, 0, 0))] * 2
        + [pl.BlockSpec((1, tm, D_MODEL), tok)] * 2
        + [pl.BlockSpec((1, tm, IDX_HEADS * IDX_K), tok),
           pl.BlockSpec((1, tm, IDX_K), tok),
           pl.BlockSpec((1, tm, LANES), tok),
           pl.BlockSpec((1, tm // MOBA_BLOCK, 1, BR_W), lambda b, i: (b, i, 0, 0))]
    )
    return pl.pallas_call(
        _proj_kernel,
        grid=(bsz, seq // tm),
        in_specs=[
            pl.BlockSpec((1, tm, d), tok),
            pl.BlockSpec((1, N_MOD, d), lambda b, i: (b, 0, 0)),
            pl.BlockSpec((1, d), const),
            resident(wqk), resident(wv), resident(wg), resident(wi),
            pl.BlockSpec(gains.shape, const),
            pl.BlockSpec((tm, LANES), tab),
            pl.BlockSpec((tm, LANES), tab),
            pl.BlockSpec((tm, LANES), tab),
            pl.BlockSpec((tm, LANES), tab),
        ],
        out_specs=out_specs,
        out_shape=out_shape,
        compiler_params=pltpu.CompilerParams(
            dimension_semantics=("arbitrary", "arbitrary"), vmem_limit_bytes=VMEM_LIMIT),
        name="proj",
    )(x, mod3, gmix, wqk, wv, wg, wi, gains, cosh, sinh, cosi, sini)


def _tile3(x):
    return x.reshape(x.shape[0] // SUBLANES, SUBLANES, x.shape[1])


def _allreduce_sublanes(x, op):
    for shift in (4, 2, 1):
        x = op(x, pltpu.roll(x, shift, 0))
    return x


def _softmax_tile(s3, vt, m_old, acc_ref, hh):
    rows = slice(hh * VT_ROWS, (hh + 1) * VT_ROWS)
    m_new = jnp.maximum(m_old, _allreduce_sublanes(jnp.max(s3, axis=0), jnp.maximum))
    alpha = jnp.exp2(m_old - m_new)
    p = jnp.exp2(s3 - m_new[None]).reshape(s3.shape[0] * SUBLANES, TQ).astype(BF16)
    acc_ref[rows, :] = (_tile3(acc_ref[rows, :]) * alpha[None]).reshape(VT_ROWS, TQ) + _dot(vt, p)
    return m_new


def _qk_tile(k_ref, q_ref, t, s_ref):
    k0 = pl.multiple_of(t * TQ, TQ)
    for hh, hs in enumerate(_HEAD_SLICES):
        s_ref[hh] = _dot_nt(k_ref[0, pl.ds(k0, TQ), hs], q_ref[0, :, hs])


def _edge_mask(j, t_true):
    shape = (TQ // SUBLANES, SUBLANES, TQ)
    row = lax.broadcasted_iota(I32, shape, 0) * SUBLANES + lax.broadcasted_iota(I32, shape, 1)
    return row - lax.broadcasted_iota(I32, shape, 2) <= (j - t_true) * TQ


def _sweep_tiles(j, s_refs, qk, consume, state):
    s0, s1 = s_refs
    last_pair = j // 2
    qk(0, s0)

    def body(p, st):
        t0 = 2 * p
        qk(t0 + 1, s1)
        st = consume(t0, t0, s0, st, False)
        qk(t0 + 2, s0)
        return consume(t0 + 1, t0 + 1, s1, st, False)

    state = lax.fori_loop(0, last_pair // 2, lambda q, st: body(2 * q + 1, body(2 * q, st)), state)
    state = lax.fori_loop(2 * (last_pair // 2), last_pair, body, state)
    ta = 2 * last_pair
    tb = jnp.minimum(ta + 1, j)
    qk(tb, s1)
    state = consume(ta, ta, s0, state, True)
    return consume(tb, ta + 1, s1, state, True)


def _init_state(acc_ref):
    acc_ref[...] = jnp.zeros(acc_ref.shape, F32)
    return (jnp.full((SUBLANES, TQ), M_INIT, F32),) * N_HEADS


def _write_heads(o_ref, acc_ref):
    for hh, hs in enumerate(_HEAD_SLICES):
        base = hh * VT_ROWS
        inv = 1.0 / acc_ref[base + HEAD_DIM:base + HEAD_DIM + SUBLANES, :]
        o_t = (_tile3(acc_ref[base:base + HEAD_DIM, :]) * inv[None]).reshape(HEAD_DIM, TQ)
        o_ref[0, :, hs] = o_t.T.astype(BF16)


def _f32_from_key(key):
    bits = jnp.where(key < 0, key ^ jnp.int32(0x7FFFFFFF), key)
    return lax.bitcast_convert_type(bits, F32)


def _key32_of_key16(key16):
    return (key16 << 16) + jnp.where(key16 < 0, 0xFFFF, 0)


def _bf16_from_key16(key16):
    return _f32_from_key(_key32_of_key16(key16))


def _dsa_kernel(a_ref, kc_ref, wk_ref, qa_ref, ka_ref, vat_ref, o_ref,
                scores_ref, hb_ref, red_ref, thr_ref, cge_ref, s0_ref, s1_ref, acc_ref, *, topk, seq_bits):
    j = pl.program_id(1)
    n_tiles = j + 1
    w_rows = wk_ref[0].T[IDX_DIM:IDX_DIM + IDX_HEADS, :]
    w8 = [jnp.broadcast_to(w_rows[hh:hh + 1, :], (SUBLANES, TQ)) for hh in range(IDX_HEADS)]

    def score_tile(c, diagonal):
        k0 = pl.multiple_of(c * TQ, TQ)
        kc = kc_ref[0, pl.ds(k0, TQ), :]
        acc = jnp.zeros((TQ // SUBLANES, SUBLANES, TQ), F32)
        for hh in range(IDX_HEADS):
            lg = _dot_nt(kc, a_ref[0, :, hh * IDX_K:(hh + 1) * IDX_K])
            acc = acc + w8[hh][None] * jnp.maximum(_tile3(lg), 0.0)
        if diagonal:
            acc = jnp.where(_edge_mask(j, j), acc, NEG)
        acc = acc.reshape(TQ, TQ)
        scores_ref[pl.ds(k0, TQ), :] = acc
        hb_ref[pl.ds(k0, TQ), :] = acc.astype(BF16)

    def score_run(start, n_trips, width):
        def trip(g, carry):
            for u in range(width):
                score_tile(start + width * g + u, False)
            return carry

        lax.fori_loop(0, n_trips, trip, 0)
        return start + width * n_trips

    done = 0
    for width in SCORE_GROUPS:
        done = score_run(done, (j - done) // width, width)
    score_tile(j, True)

    def load_packed(c, u):
        r0 = pl.multiple_of(c * TQ + u * SEL_ROWS, SEL_ROWS)
        return hb_ref[pl.ds(r0, SEL_ROWS), :].reshape(SEL_ROWS // PACK_ROWS, PACK_ROWS, TQ)

    def load_scores(c, u):
        r0 = pl.multiple_of(c * TQ + u * SEL_ROWS, SEL_ROWS)
        return _tile3(scores_ref[pl.ds(r0, SEL_ROWS), :]), r0

    def sweep_count(tile, zero):
        cnt = lax.fori_loop(0, n_tiles // 2, lambda p, cnt: tile(2 * p + 1, tile(2 * p, cnt)), zero)
        return lax.fori_loop(2 * (n_tiles // 2), n_tiles, tile, cnt)

    one16, zero16 = jnp.int16(1), jnp.int16(0)

    def count_bf16(test):
        t = jnp.broadcast_to(test, (PACK_ROWS, TQ)).astype(BF16)[None]

        def tile(c, cnt):
            for u in range(TQ // SEL_ROWS):
                cnt = cnt + jnp.where(load_packed(c, u) >= t, one16, zero16)
            return cnt

        cnt = sweep_count(tile, jnp.zeros((SEL_ROWS // PACK_ROWS, PACK_ROWS, TQ), I16))
        return jnp.sum(jnp.sum(cnt.astype(F32), axis=0), axis=0, keepdims=True)

    def count_f32(pred):
        def tile(c, cnt):
            for u in range(TQ // SEL_ROWS):
                sc, r0 = load_scores(c, u)
                cnt = cnt + jnp.where(pred(sc, r0), 1.0, 0.0)
            return cnt

        cnt = sweep_count(tile, jnp.zeros((SEL_ROWS // SUBLANES, SUBLANES, TQ), F32))
        return jnp.sum(jnp.sum(cnt, axis=0), axis=0, keepdims=True)

    kf = float(topk)

    c0 = count_bf16(jnp.zeros((1, TQ), F32))
    start = jnp.where(c0 >= kf, 0, I16_MIN).astype(I32)

    def bit_step(i, val):
        test = val | (jnp.int32(1) << (14 - i))
        return jnp.where(count_bf16(_bf16_from_key16(test)) >= kf, test, val)

    hb_k = lax.fori_loop(0, 15, bit_step, start)

    below = _key32_of_key16(jnp.maximum(hb_k - 1, BF16_KEY_NEG_INF))
    lo = below + ((_key32_of_key16(hb_k) - below) >> 1) - 1
    hi = _key32_of_key16(jnp.minimum(hb_k + 1, BF16_KEY_POS_INF))

    def bisect(count_ge):
        def step(i, carry):
            lo, hi, cge = carry
            mid = lo + ((hi - lo) >> 1)
            c = count_ge(_f32_from_key(mid)[None])
            ok = c >= kf
            return jnp.where(ok, mid, lo), jnp.where(ok, hi, mid), jnp.where(ok, c, cge)

        lo_k, _, cge = lax.fori_loop(0, BISECT_STEPS, step, (lo, hi, jnp.full((1, TQ), kf, F32)))
        thr_ref[...] = jnp.broadcast_to(_f32_from_key(lo_k), (SUBLANES, TQ))
        cge_ref[...] = jnp.broadcast_to(cge, (SUBLANES, TQ))

    lo_f, hi_f = _f32_from_key(lo), _f32_from_key(hi)
    ninf = jnp.full((SUBLANES, TQ), -jnp.inf, F32)

    def reduce_tile(c, carry):
        above, spill = carry
        k0 = pl.multiple_of(c * TQ, TQ)
        x3 = _tile3(scores_ref[pl.ds(k0, TQ), :])
        top = [ninf] * KEEP
        for g in range(TQ // SUBLANES):
            x = x3[g]
            ge_hi = x >= hi_f
            above = above + jnp.where(ge_hi, 1.0, 0.0)
            t = jnp.where((x >= lo_f) & ~ge_hi, x, -jnp.inf)
            for r in range(KEEP):
                top[r], t = jnp.maximum(top[r], t), jnp.minimum(top[r], t)
            spill = jnp.maximum(spill, t)
        for r in range(KEEP):
            red_ref[c, r] = top[r]
        return above, spill

    above, spill = lax.fori_loop(0, n_tiles, reduce_tile, (jnp.zeros((SUBLANES, TQ), F32), ninf))
    above = jnp.sum(above, axis=0, keepdims=True)
    reduced_ok = jnp.max(spill) == -jnp.inf

    @pl.when(reduced_ok)
    def _():
        def count_ge(t3):
            def tile(c, cnt):
                for r in range(KEEP):
                    cnt = cnt + jnp.where(red_ref[c, r] >= t3[0], 1.0, 0.0)
                return cnt
            cnt = sweep_count(tile, jnp.zeros((SUBLANES, TQ), F32))
            return above + jnp.sum(cnt, axis=0, keepdims=True)
        bisect(count_ge)

    @pl.when(jnp.logical_not(reduced_ok))
    def _():
        bisect(lambda t3: count_f32(lambda sc, r0: sc >= t3))

    thr = thr_ref[0:1, :]
    cge = cge_ref[0:1, :]

    @pl.when(jnp.max(jnp.abs(cge - kf)) > 0.0)
    def _():
        thr3 = thr[None]
        need = kf - count_f32(lambda sc, r0: sc > thr3)
        sub = (lax.broadcasted_iota(I32, (SEL_ROWS // SUBLANES, SUBLANES, TQ), 0) * SUBLANES
               + lax.broadcasted_iota(I32, (SEL_ROWS // SUBLANES, SUBLANES, TQ), 1))

        def cut_step(i, cut):
            test = cut | (jnp.int32(1) << (seq_bits - 1 - i))
            f = count_f32(lambda sc, r0: (sc == thr3) & (r0 + sub < test[None]))
            return jnp.where(f < need, test, cut)

        cut = lax.fori_loop(0, seq_bits, cut_step, jnp.zeros((1, TQ), I32))

        def demote(c, carry):
            r0 = pl.multiple_of(c * SEL_ROWS, SEL_ROWS)
            sc = _tile3(scores_ref[pl.ds(r0, SEL_ROWS), :])
            scores_ref[pl.ds(r0, SEL_ROWS), :] = jnp.where(
                (sc == thr3) & (r0 + sub > cut[None]), NEG, sc).reshape(SEL_ROWS, TQ)
            return carry

        lax.fori_loop(0, n_tiles * (TQ // SEL_ROWS), demote, 0)

    def consume(t, t_true, s_ref, state, edge):
        k0 = pl.multiple_of(t * TQ, TQ)
        sel = _tile3(scores_ref[pl.ds(k0, TQ), :]) >= thr[None]
        if edge:
            sel = sel & _edge_mask(j, t_true)
        bias = jnp.where(sel, 0.0, NEG)
        return tuple(
            _softmax_tile(_tile3(s_ref[hh]) + bias, vat_ref[0, t, hh * VT_ROWS:(hh + 1) * VT_ROWS, :],
                          state[hh], acc_ref, hh)
            for hh in range(N_HEADS))

    qk = functools.partial(_qk_tile, ka_ref, qa_ref)
    _sweep_tiles(j, (s0_ref, s1_ref), qk, consume, _init_state(acc_ref))
    _write_heads(o_ref, acc_ref)


def _dsa_call(a_mat, kc, idx_out, qa, ka, vat):
    bsz, seq, _ = qa.shape
    topk = min(DSA_TOPK_MAX, seq // 4)
    seq_bits = (seq - 1).bit_length()
    kern = functools.partial(_dsa_kernel, topk=topk, seq_bits=seq_bits)
    return pl.pallas_call(
        kern,
        grid=(bsz, seq // TQ),
        in_specs=[
            pl.BlockSpec((1, TQ, IDX_HEADS * IDX_K), lambda b, j: (b, j, 0)),
            pl.BlockSpec((1, seq, IDX_K), lambda b, j: (b, 0, 0)),
            pl.BlockSpec((1, TQ, LANES), lambda b, j: (b, j, 0)),
            pl.BlockSpec((1, TQ, BR_W), lambda b, j: (b, j, 0)),
            pl.BlockSpec((1, seq, BR_W), lambda b, j: (b, 0, 0)),
            pl.BlockSpec((1, seq // TQ, N_HEADS * VT_ROWS, TQ), lambda b, j: (b, 0, 0, 0)),
        ],
        out_specs=pl.BlockSpec((1, TQ, BR_W), lambda b, j: (b, j, 0)),
        out_shape=jax.ShapeDtypeStruct((bsz, seq, BR_W), BF16),
        scratch_shapes=[
            pltpu.VMEM((seq, TQ), F32),
            pltpu.VMEM((seq, TQ), BF16),
            pltpu.VMEM((seq // TQ, KEEP, SUBLANES, TQ), F32),
            pltpu.VMEM((SUBLANES, TQ), F32),
            pltpu.VMEM((SUBLANES, TQ), F32),
            pltpu.VMEM((N_HEADS, TQ, TQ), F32),
            pltpu.VMEM((N_HEADS, TQ, TQ), F32),
            pltpu.VMEM((N_HEADS * VT_ROWS, TQ), F32),
        ],
        compiler_params=pltpu.CompilerParams(
            dimension_semantics=("arbitrary", "arbitrary"), vmem_limit_bytes=VMEM_LIMIT),
        name="dsa",
    )(a_mat, kc, idx_out, qa, ka, vat)


def _moba_kernel(qb_ref, kb_ref, vbt_ref, kmean_ref, o_ref, bias_ref, s0_ref, s1_ref, acc_ref, *, n_sel):
    j = pl.program_id(1)
    nb = kmean_ref.shape[1]
    blk = lax.broadcasted_iota(I32, (nb, TQ), 0).astype(F32)
    jf = j.astype(F32)

    for hh in range(N_HEADS):
        hs = slice(hh * HEAD_DIM, (hh + 1) * HEAD_DIM)
        km_hi, km_lo = _split_bf16(kmean_ref[0, :, hs])
        q = qb_ref[0, :, hs]
        gate = jnp.where(blk < jf, _dot_nt(km_hi, q) + _dot_nt(km_lo, q), NEG)
        chosen = jnp.zeros((nb, TQ), F32)
        for _ in range(n_sel):
            best = jnp.max(gate, axis=0, keepdims=True)
            first = jnp.min(jnp.where(gate == best, blk, float(nb)), axis=0, keepdims=True)
            hit = blk == first
            chosen = jnp.where(hit, 1.0, chosen)
            gate = jnp.where(hit, -jnp.inf, gate)
        bias = jnp.where((chosen > 0.0) & (blk < jf), 0.0, NEG)
        for n in range(nb):
            bias_ref[hh, n] = jnp.broadcast_to(bias[n:n + 1, :], (SUBLANES, TQ))

    def consume(t, t_true, s_ref, state, edge):
        new = []
        for hh in range(N_HEADS):
            bias = bias_ref[hh, t]
            if edge:
                bias = jnp.where(t_true < j, bias, 0.0)
            s3 = _tile3(s_ref[hh]) + bias[None]
            if edge:
                s3 = jnp.where(_edge_mask(j, t_true), s3, NEG)
            new.append(_softmax_tile(s3, vbt_ref[0, t, hh * VT_ROWS:(hh + 1) * VT_ROWS, :], state[hh], acc_ref, hh))
        return tuple(new)

    qk = functools.partial(_qk_tile, kb_ref, qb_ref)
    _sweep_tiles(j, (s0_ref, s1_ref), qk, consume, _init_state(acc_ref))
    _write_heads(o_ref, acc_ref)


def _moba_call(qb, kb, vbt, kmean):
    bsz, seq, _ = qb.shape
    nb = seq // MOBA_BLOCK
    kern = functools.partial(_moba_kernel, n_sel=min(MOBA_TOPK, nb - 1))
    return pl.pallas_call(
        kern,
        grid=(bsz, seq // TQ),
        in_specs=[
            pl.BlockSpec((1, TQ, BR_W), lambda b, j: (b, j, 0)),
            pl.BlockSpec((1, seq, BR_W), lambda b, j: (b, 0, 0)),
            pl.BlockSpec((1, seq // TQ, N_HEADS * VT_ROWS, TQ), lambda b, j: (b, 0, 0, 0)),
            pl.BlockSpec((1, nb, BR_W), lambda b, j: (b, 0, 0)),
        ],
        out_specs=pl.BlockSpec((1, TQ, BR_W), lambda b, j: (b, j, 0)),
        out_shape=jax.ShapeDtypeStruct((bsz, seq, BR_W), BF16),
        scratch_shapes=[
            pltpu.VMEM((N_HEADS, nb, SUBLANES, TQ), F32),
            pltpu.VMEM((N_HEADS, TQ, TQ), F32),
            pltpu.VMEM((N_HEADS, TQ, TQ), F32),
            pltpu.VMEM((N_HEADS * VT_ROWS, TQ), F32),
        ],
        compiler_params=pltpu.CompilerParams(
            dimension_semantics=("arbitrary", "arbitrary"), vmem_limit_bytes=VMEM_LIMIT),
        name="moba",
    )(qb, kb, vbt, kmean)


def _out_ffn_kernel(x_ref, oa_ref, ob_ref, ga_ref, gb_ref, mod_ref, gffn_ref,
                    wba_ref, wbb_ref, wo_ref, wgu_ref, wd_ref, o_ref, act_ref):
    merged = (ga_ref[0].astype(F32) * _dot(oa_ref[0], wba_ref[...])
              + gb_ref[0].astype(F32) * _dot(ob_ref[0], wbb_ref[...]))
    x1 = x_ref[0] + mod_ref[0, 2:3, :] * _dot(merged.astype(BF16), wo_ref[...])
    y = x1 * lax.rsqrt(jnp.mean(x1 * x1, axis=-1, keepdims=True) + EPS) * gffn_ref[...]
    h = (y * (1.0 + mod_ref[0, 4:5, :]) + mod_ref[0, 3:4, :]).astype(BF16)
    for c0 in range(0, D_FF, FF_CHUNK):
        g = _dot(h, wgu_ref[:, c0:c0 + FF_CHUNK])
        u = _dot(h, wgu_ref[:, D_FF + c0:D_FF + c0 + FF_CHUNK])
        act_ref[:, c0:c0 + FF_CHUNK] = (g * jax.nn.sigmoid(g) * u).astype(BF16)
    o_ref[0] = x1 + mod_ref[0, 5:6, :] * _dot(act_ref[...], wd_ref[...])


def _out_ffn_call(x, oa, ob, ga, gb, mod3, gffn, wba, wbb, wo, wgu, wd):
    bsz, seq, d = x.shape
    tm = TM_FFN
    const = lambda b, i: (0, 0)
    tok = lambda b, i: (b, i, 0)
    resident = lambda a: pl.BlockSpec(a.shape, const, pipeline_mode=pl.Buffered(1))
    return pl.pallas_call(
        _out_ffn_kernel,
        grid=(bsz, seq // tm),
        in_specs=[
            pl.BlockSpec((1, tm, d), tok),
            pl.BlockSpec((1, tm, BR_W), tok),
            pl.BlockSpec((1, tm, BR_W), tok),
            pl.BlockSpec((1, tm, d), tok),
            pl.BlockSpec((1, tm, d), tok),
            pl.BlockSpec((1, N_MOD, d), lambda b, i: (b, 0, 0)),
            pl.BlockSpec((1, d), const),
            resident(wba), resident(wbb), resident(wo), resident(wgu), resident(wd),
        ],
        out_specs=pl.BlockSpec((1, tm, d), tok),
        out_shape=jax.ShapeDtypeStruct((bsz, seq, d), F32),
        scratch_shapes=[pltpu.VMEM((tm, D_FF), BF16)],
        compiler_params=pltpu.CompilerParams(
            dimension_semantics=("arbitrary", "arbitrary"), vmem_limit_bytes=VMEM_LIMIT),
        name="out_ffn",
    )(x, oa, ob, ga, gb, mod3, gffn, wba, wbb, wo, wgu, wd)


def _rope_tables(seq, dim):
    inv = ROPE_THETA ** (-jnp.arange(0, dim, 2, dtype=F32) / dim)
    ang = jnp.arange(seq, dtype=F32)[:, None] * inv[None, :]
    cos, sin = jnp.cos(ang), jnp.sin(ang)
    reps = LANES // dim
    return (jnp.tile(jnp.concatenate([cos, cos], axis=-1), (1, reps)),
            jnp.tile(jnp.concatenate([-sin, sin], axis=-1), (1, reps)))


def _layer(x, mod, g_mix, g_ffn, w_in, g_qa, g_ka, g_qb, g_kb, w_br_a, w_br_b, w_out, w_gu, w_down):
    bsz, seq, d = x.shape
    mod3 = mod.reshape(bsz, N_MOD, d)

    o = 0
    cols = {}
    for name, width in (("qa", BR_W), ("ka", BR_W), ("va", BR_W), ("qi", IDX_HEADS * IDX_DIM),
                        ("ki", IDX_DIM), ("wi", IDX_HEADS), ("qb", BR_W), ("kb", BR_W), ("vb", BR_W),
                        ("ga", D_MODEL), ("gb", D_MODEL)):
        cols[name] = w_in[:, o:o + width]
        o += width
    wqk = jnp.concatenate([cols["qa"], cols["ka"], cols["qb"], cols["kb"]], axis=1).astype(BF16)
    wv = jnp.concatenate([cols["va"], cols["vb"]], axis=1).astype(BF16)
    wg = jnp.concatenate([cols["ga"], cols["gb"]], axis=1).astype(BF16)
    pad = IDX_OUT_W - IDX_HEADS * IDX_DIM - IDX_DIM - IDX_HEADS
    wi = jnp.concatenate([cols["qi"], cols["ki"], cols["wi"], jnp.zeros((d, pad), F32)], axis=1).astype(BF16)
    gains = jnp.stack([g_qa, g_ka, g_qb, g_kb])
    cosh, sinh = _rope_tables(seq, HEAD_DIM)
    cosi, sini = _rope_tables(seq, IDX_DIM)

    qa, ka, qb, kb, vat, vbt, ga, gb, a_mat, kc, kw, kmean = _proj_call(
        x, mod3, g_mix.reshape(1, d), wqk, wv, wg, wi, gains, cosh, sinh, cosi, sini)

    oa = _dsa_call(a_mat, kc, kw, qa, ka, vat)
    ob = _moba_call(qb, kb, vbt, kmean.reshape(bsz, seq // MOBA_BLOCK, BR_W))

    return _out_ffn_call(x, oa, ob, ga, gb, mod3, g_ffn.reshape(1, d),
                         w_br_a.astype(BF16), w_br_b.astype(BF16), w_out.astype(BF16),
                         w_gu.astype(BF16), w_down.astype(BF16))


def kernel(x, c, w_mod, b_mod, g_mix_norm, g_ffn_norm, w_in, g_q_dsa, g_k_dsa, g_q_moba, g_k_moba,
           w_br_dsa, w_br_moba, w_out, w_gate_up, w_down):
    assert x.shape[1] % TM_PROJ == 0 and x.shape[2] == D_MODEL and w_gate_up.shape[2] == 2 * D_FF
    for l in range(w_mod.shape[0]):
        mod = _mod_call(c, w_mod[l], b_mod[l])
        x = _layer(x, mod, g_mix_norm[l], g_ffn_norm[l], w_in[l], g_q_dsa[l], g_k_dsa[l], g_q_moba[l],
                   g_k_moba[l], w_br_dsa[l], w_br_moba[l], w_out[l], w_gate_up[l], w_down[l])
    return x
```

```python
import functools

import jax
import jax.numpy as jnp
from jax import lax
from jax.experimental import pallas as pl
from jax.experimental.pallas import tpu as pltpu

F32 = jnp.float32
BF16 = jnp.bfloat16
I32 = jnp.int32
I16 = jnp.int16

D_MODEL = 1024
HEAD_DIM = 128
N_HEADS = 4
BR_W = N_HEADS * HEAD_DIM
IDX_HEADS = 8
IDX_DIM = 64
DSA_TOPK_MAX = 256
MOBA_BLOCK = 256
MOBA_TOPK = 3
D_FF = 2816
ROPE_THETA = 10000.0
EPS = 1e-6
NEG = -1e30
LOG2_E = 1.4426950408889634
M_INIT = -1e29
N_MOD = 6
IDX_OUT_W = 640
IDX_K = 256
I16_MIN = -(2 ** 15)
BF16_KEY_NEG_INF = (0xFF80 ^ 0x7FFF) - 0x10000
BF16_KEY_POS_INF = 0x7F80
BISECT_STEPS = 17

LANES = 128
SUBLANES = 8
PACK_ROWS = 16
VT_ROWS = HEAD_DIM + PACK_ROWS
TQ = 256
TM_PROJ = 512
PROJ_ROWS = 256
TM_FFN = 512
FF_CHUNK = 256
SEL_ROWS = 32
KEEP = 3
SCORE_GROUPS = (4, 2, 1)
VMEM_LIMIT = 56 * 1024 * 1024

_NT = (((1,), (1,)), ((), ()))
_HEAD_SLICES = tuple(slice(h * HEAD_DIM, (h + 1) * HEAD_DIM) for h in range(N_HEADS))


def _split_bf16(x):
    hi = x.astype(BF16)
    lo = (x - hi.astype(F32)).astype(BF16)
    return hi, lo


def _dot(a, b):
    return jnp.dot(a, b, preferred_element_type=F32)


def _dot_nt(a, b):
    return lax.dot_general(a, b, _NT, preferred_element_type=F32)


def _mod_kernel(c_ref, w_ref, b_ref, o_ref):
    c = c_ref[...]
    a_hi, a_lo = _split_bf16(c * jax.nn.sigmoid(c))
    w_hi, w_lo = _split_bf16(w_ref[...])
    o_ref[...] = _dot(a_hi, w_hi) + _dot(a_lo, w_hi) + _dot(a_hi, w_lo) + b_ref[...]


def _mod_call(c, w_mod, b_mod):
    bsz, d = c.shape
    n = w_mod.shape[1]
    tn = 1024
    return pl.pallas_call(
        _mod_kernel,
        grid=(n // tn,),
        in_specs=[
            pl.BlockSpec((bsz, d), lambda i: (0, 0)),
            pl.BlockSpec((d, tn), lambda i: (0, i)),
            pl.BlockSpec((1, tn), lambda i: (0, i)),
        ],
        out_specs=pl.BlockSpec((bsz, tn), lambda i: (0, i)),
        out_shape=jax.ShapeDtypeStruct((bsz, n), F32),
        compiler_params=pltpu.CompilerParams(vmem_limit_bytes=VMEM_LIMIT),
        name="mod",
    )(c, w_mod, b_mod.reshape(1, n))


def _rope_partner_64(x):
    lane = lax.broadcasted_iota(I32, x.shape, 1)
    return jnp.where((lane % IDX_DIM) < IDX_DIM // 2,
                     pltpu.roll(x, LANES - IDX_DIM // 2, 1), pltpu.roll(x, IDX_DIM // 2, 1))


def _proj_kernel(x_ref, mod_ref, gmix_ref, wqk_ref, wvt_ref, wg_ref, wi_ref, gains_ref,
                 cosh_ref, sinh_ref, cosi_ref, sini_ref,
                 qa_ref, ka_ref, qb_ref, kb_ref, vat_ref, vbt_ref, ga_ref, gb_ref, a_ref, kc_ref, kw_ref, kmean_ref):
    for blk in range(x_ref.shape[1] // PROJ_ROWS):
        _proj_rows(blk, x_ref, mod_ref, gmix_ref, wqk_ref, wvt_ref, wg_ref, wi_ref, gains_ref,
                   cosh_ref, sinh_ref, cosi_ref, sini_ref,
                   qa_ref, ka_ref, qb_ref, kb_ref, vat_ref, vbt_ref, ga_ref, gb_ref, a_ref, kc_ref, kw_ref, kmean_ref)


def _proj_rows(blk, x_ref, mod_ref, gmix_ref, wqk_ref, wvt_ref, wg_ref, wi_ref, gains_ref,
               cosh_ref, sinh_ref, cosi_ref, sini_ref,
               qa_ref, ka_ref, qb_ref, kb_ref, vat_ref, vbt_ref, ga_ref, gb_ref, a_ref, kc_ref, kw_ref, kmean_ref):
    rows = slice(blk * PROJ_ROWS, (blk + 1) * PROJ_ROWS)
    x = x_ref[0, rows, :]
    y = x * lax.rsqrt(jnp.mean(x * x, axis=-1, keepdims=True) + EPS) * gmix_ref[...]
    h = y * (1.0 + mod_ref[0, 1:2, :]) + mod_ref[0, 0:1, :]
    h_hi = h.astype(BF16)
    cosh, sinh = cosh_ref[rows, :], sinh_ref[rows, :]
    scale = HEAD_DIM ** -0.5 * LOG2_E

    for gi, out_ref in enumerate((qa_ref, ka_ref, qb_ref, kb_ref)):
        p = _dot(h_hi, wqk_ref[:, gi * BR_W:(gi + 1) * BR_W])
        for hh in range(N_HEADS):
            ph = p[:, hh * HEAD_DIM:(hh + 1) * HEAD_DIM]
            yh = ph * lax.rsqrt(jnp.mean(ph * ph, axis=-1, keepdims=True) + EPS) * gains_ref[gi:gi + 1, :]
            r = yh * cosh + pltpu.roll(yh, HEAD_DIM // 2, 1) * sinh
            if gi == 3:
                kmean_ref[0, blk, :, hh * HEAD_DIM:(hh + 1) * HEAD_DIM] = jnp.mean(r, axis=0, keepdims=True)
            if gi % 2 == 0:
                r = r * scale
            out_ref[0, rows, hh * HEAD_DIM:(hh + 1) * HEAD_DIM] = r.astype(BF16)

    vt = _dot_nt(wvt_ref[...], h_hi)
    ones = jnp.ones((VT_ROWS - HEAD_DIM, TQ), BF16)
    for gi, out_ref in enumerate((vat_ref, vbt_ref)):
        for hh in range(N_HEADS):
            src = gi * BR_W + hh * HEAD_DIM
            out_ref[0, blk, hh * VT_ROWS:hh * VT_ROWS + HEAD_DIM, :] = vt[src:src + HEAD_DIM, :].astype(BF16)
            out_ref[0, blk, hh * VT_ROWS + HEAD_DIM:(hh + 1) * VT_ROWS, :] = ones

    for gi, out_ref in enumerate((ga_ref, gb_ref)):
        g = _dot(h_hi, wg_ref[:, gi * D_MODEL:(gi + 1) * D_MODEL])
        out_ref[0, rows, :] = jax.nn.sigmoid(g).astype(BF16)

    pi = _dot(h_hi, wi_ref[...])
    cosi, sini = cosi_ref[rows, :], sini_ref[rows, :]
    first = lax.broadcasted_iota(I32, (PROJ_ROWS, LANES), 1) < IDX_DIM
    half = LANES // 2

    def hi_lo(v):
        hi = v.astype(BF16).astype(F32)
        return hi, v - hi

    n_q = IDX_HEADS * IDX_DIM // LANES
    for g4 in range(n_q):
        xg = pi[:, g4 * LANES:(g4 + 1) * LANES]
        hi, lo = hi_lo(xg * cosi + _rope_partner_64(xg) * sini)
        hi_r, lo_r = pltpu.roll(hi, half, 1), pltpu.roll(lo, half, 1)
        for head, parts in ((2 * g4, (jnp.where(first, hi, lo_r), jnp.where(first, hi, 0.0))),
                            (2 * g4 + 1, (jnp.where(first, hi_r, lo), jnp.where(first, hi_r, 0.0)))):
            for pi_, part in enumerate(parts):
                a_ref[0, rows, head * IDX_K + pi_ * LANES:head * IDX_K + (pi_ + 1) * LANES] = part.astype(BF16)
    xg = pi[:, n_q * LANES:(n_q + 1) * LANES]
    roped = xg * cosi + _rope_partner_64(xg) * sini
    hi, lo = hi_lo(roped)
    kc_ref[0, rows, :LANES] = jnp.where(first, hi, pltpu.roll(hi, half, 1)).astype(BF16)
    kc_ref[0, rows, LANES:] = jnp.where(first, lo, 0.0).astype(BF16)
    w_scale = (IDX_HEADS ** -0.5) * (IDX_DIM ** -0.5)
    kw_ref[0, rows, :] = jnp.where(first, roped, xg * w_scale)


def _proj_call(x, mod3, gmix, wqk, wvt, wg, wi, gains, cosh, sinh, cosi, sini):
    bsz, seq, d = x.shape
    tm = TM_PROJ
    nb = seq // MOBA_BLOCK
    const = lambda b, i: (0, 0)
    tok = lambda b, i: (b, i, 0)
    tab = lambda b, i: (i, 0)
    resident = lambda a: pl.BlockSpec(a.shape, const, pipeline_mode=pl.Buffered(1))
    out_shape = (
        [jax.ShapeDtypeStruct((bsz, seq, BR_W), BF16)] * 4
        + [jax.ShapeDtypeStruct((bsz, seq // TQ, N_HEADS * VT_ROWS, TQ), BF16)] * 2
        + [jax.ShapeDtypeStruct((bsz, seq, D_MODEL), BF16)] * 2
        + [jax.ShapeDtypeStruct((bsz, seq, IDX_HEADS * IDX_K), BF16),
           jax.ShapeDtypeStruct((bsz, seq, IDX_K), BF16),
           jax.ShapeDtypeStruct((bsz, seq, LANES), F32),
           jax.ShapeDtypeStruct((bsz, nb, 1, BR_W), F32)]
    )
    out_specs = (
        [pl.BlockSpec((1, tm, BR_W), tok)] * 4
        + [pl.BlockSpec((1, tm // TQ, N_HEADS * VT_ROWS, TQ), lambda b, i: (b, i, 0, 0))] * 2
        + [pl.BlockSpec((1, tm, D_MODEL), tok)] * 2
        + [pl.BlockSpec((1, tm, IDX_HEADS * IDX_K), tok),
           pl.BlockSpec((1, tm, IDX_K), tok),
           pl.BlockSpec((1, tm, LANES), tok),
           pl.BlockSpec((1, tm // MOBA_BLOCK, 1, BR_W), lambda b, i: (b, i, 0, 0))]
    )
    return pl.pallas_call(
        _proj_kernel,
        grid=(bsz, seq // tm),
        in_specs=[
            pl.BlockSpec((1, tm, d), tok),
            pl.BlockSpec((1, N_MOD, d), lambda b, i: (b, 0, 0)),
            pl.BlockSpec((1, d), const),
            resident(wqk), resident(wvt), resident(wg), resident(wi),
            pl.BlockSpec(gains.shape, const),
            pl.BlockSpec((tm, LANES), tab),
            pl.BlockSpec((tm, LANES), tab),
            pl.BlockSpec((tm, LANES), tab),
            pl.BlockSpec((tm, LANES), tab),
        ],
        out_specs=out_specs,
        out_shape=out_shape,
        compiler_params=pltpu.CompilerParams(
            dimension_semantics=("arbitrary", "arbitrary"), vmem_limit_bytes=VMEM_LIMIT),
        name="proj",
    )(x, mod3, gmix, wqk, wvt, wg, wi, gains, cosh, sinh, cosi, sini)


def _tile3(x):
    return x.reshape(x.shape[0] // SUBLANES, SUBLANES, x.shape[1])


def _allreduce_sublanes(x, op):
    for shift in (4, 2, 1):
        x = op(x, pltpu.roll(x, shift, 0))
    return x


def _softmax_tile(s3, vt, m_old, acc_ref, hh):
    rows = slice(hh * VT_ROWS, (hh + 1) * VT_ROWS)
    m_new = jnp.maximum(m_old, _allreduce_sublanes(jnp.max(s3, axis=0), jnp.maximum))
    alpha = jnp.exp2(m_old - m_new)
    p = jnp.exp2(s3 - m_new[None]).reshape(s3.shape[0] * SUBLANES, TQ).astype(BF16)
    acc_ref[rows, :] = (_tile3(acc_ref[rows, :]) * alpha[None]).reshape(VT_ROWS, TQ) + _dot(vt, p)
    return m_new


def _qk_tile(k_ref, q_ref, t, s_ref):
    k0 = pl.multiple_of(t * TQ, TQ)
    for hh, hs in enumerate(_HEAD_SLICES):
        s_ref[hh] = _dot_nt(k_ref[0, pl.ds(k0, TQ), hs], q_ref[0, :, hs])


def _causal_triangle():
    shape = (TQ // SUBLANES, SUBLANES, TQ)
    row = lax.broadcasted_iota(I32, shape, 0) * SUBLANES + lax.broadcasted_iota(I32, shape, 1)
    return row <= lax.broadcasted_iota(I32, shape, 2)


def _edge_mask(j, t_true):
    shape = (TQ // SUBLANES, SUBLANES, TQ)
    row = lax.broadcasted_iota(I32, shape, 0) * SUBLANES + lax.broadcasted_iota(I32, shape, 1)
    return row - lax.broadcasted_iota(I32, shape, 2) <= (j - t_true) * TQ


def _sweep_tiles(j, s_refs, qk, consume, state):
    s0, s1 = s_refs
    last_pair = j // 2
    qk(0, s0)

    def body(p, st):
        t0 = 2 * p
        qk(t0 + 1, s1)
        st = consume(t0, t0, s0, st, False)
        qk(t0 + 2, s0)
        return consume(t0 + 1, t0 + 1, s1, st, False)

    state = lax.fori_loop(0, last_pair // 2, lambda q, st: body(2 * q + 1, body(2 * q, st)), state)
    state = lax.fori_loop(2 * (last_pair // 2), last_pair, body, state)
    ta = 2 * last_pair

    @pl.when(ta == j)
    def _():
        consume(j, j, s0, state, True)

    @pl.when(ta != j)
    def _():
        qk(j, s1)
        consume(j, j, s1, consume(ta, ta, s0, state, False), True)


def _init_state(acc_ref):
    acc_ref[...] = jnp.zeros(acc_ref.shape, F32)
    return (jnp.full((SUBLANES, TQ), M_INIT, F32),) * N_HEADS


def _write_heads(o_ref, acc_ref):
    for hh, hs in enumerate(_HEAD_SLICES):
        base = hh * VT_ROWS
        inv = 1.0 / acc_ref[base + HEAD_DIM:base + HEAD_DIM + SUBLANES, :]
        o_t = (_tile3(acc_ref[base:base + HEAD_DIM, :]) * inv[None]).reshape(HEAD_DIM, TQ)
        o_ref[0, :, hs] = o_t.T.astype(BF16)


def _f32_from_key(key):
    bits = jnp.where(key < 0, key ^ jnp.int32(0x7FFFFFFF), key)
    return lax.bitcast_convert_type(bits, F32)


def _key32_of_key16(key16):
    return (key16 << 16) + jnp.where(key16 < 0, 0xFFFF, 0)


def _bf16_from_key16(key16):
    return _f32_from_key(_key32_of_key16(key16))


def _dsa_kernel(a_ref, kc_ref, wk_ref, qa_ref, ka_ref, vat_ref, o_ref,
                scores_ref, hb_ref, red_ref, thr_ref, cge_ref, s0_ref, s1_ref, acc_ref, *, topk, seq_bits):
    j = pl.program_id(1)
    n_tiles = j + 1
    w_rows = wk_ref[0].T[IDX_DIM:IDX_DIM + IDX_HEADS, :]
    w8 = [jnp.broadcast_to(w_rows[hh:hh + 1, :], (SUBLANES, TQ)) for hh in range(IDX_HEADS)]

    def score_tile(c, diagonal):
        k0 = pl.multiple_of(c * TQ, TQ)
        kc = kc_ref[0, pl.ds(k0, TQ), :]
        acc = jnp.zeros((TQ // SUBLANES, SUBLANES, TQ), F32)
        for hh in range(IDX_HEADS):
            lg = _dot_nt(kc, a_ref[0, :, hh * IDX_K:(hh + 1) * IDX_K])
            acc = acc + w8[hh][None] * jnp.maximum(_tile3(lg), 0.0)
        if diagonal:
            acc = jnp.where(_causal_triangle(), acc, NEG)
        acc = acc.reshape(TQ, TQ)
        scores_ref[pl.ds(k0, TQ), :] = acc
        hb_ref[pl.ds(k0, TQ), :] = acc.astype(BF16)

    def score_run(start, n_trips, width):
        def trip(g, carry):
            for u in range(width):
                score_tile(start + width * g + u, False)
            return carry

        lax.fori_loop(0, n_trips, trip, 0)
        return start + width * n_trips

    done = 0
    for width in SCORE_GROUPS:
        done = score_run(done, (j - done) // width, width)
    score_tile(j, True)

    def load_packed(c, u):
        r0 = pl.multiple_of(c * TQ + u * SEL_ROWS, SEL_ROWS)
        return hb_ref[pl.ds(r0, SEL_ROWS), :].reshape(SEL_ROWS // PACK_ROWS, PACK_ROWS, TQ)

    def load_scores(c, u):
        r0 = pl.multiple_of(c * TQ + u * SEL_ROWS, SEL_ROWS)
        return _tile3(scores_ref[pl.ds(r0, SEL_ROWS), :]), r0

    def sweep_count(tile, zero):
        cnt = lax.fori_loop(0, n_tiles // 2, lambda p, cnt: tile(2 * p + 1, tile(2 * p, cnt)), zero)
        return lax.fori_loop(2 * (n_tiles // 2), n_tiles, tile, cnt)

    one16, zero16 = jnp.int16(1), jnp.int16(0)

    def count_bf16(test):
        t = jnp.broadcast_to(test, (PACK_ROWS, TQ)).astype(BF16)[None]

        def tile(c, cnt):
            for u in range(TQ // SEL_ROWS):
                cnt = cnt + jnp.where(load_packed(c, u) >= t, one16, zero16)
            return cnt

        cnt = sweep_count(tile, jnp.zeros((SEL_ROWS // PACK_ROWS, PACK_ROWS, TQ), I16))
        return jnp.sum(jnp.sum(cnt.astype(F32), axis=0), axis=0, keepdims=True)

    def count_f32(pred):
        def tile(c, cnt):
            for u in range(TQ // SEL_ROWS):
                sc, r0 = load_scores(c, u)
                cnt = cnt + jnp.where(pred(sc, r0), 1.0, 0.0)
            return cnt

        cnt = sweep_count(tile, jnp.zeros((SEL_ROWS // SUBLANES, SUBLANES, TQ), F32))
        return jnp.sum(jnp.sum(cnt, axis=0), axis=0, keepdims=True)

    kf = float(topk)

    c0 = count_bf16(jnp.zeros((1, TQ), F32))
    start = jnp.where(c0 >= kf, 0, I16_MIN).astype(I32)

    def bit_step(i, val):
        test = val | (jnp.int32(1) << (14 - i))
        return jnp.where(count_bf16(_bf16_from_key16(test)) >= kf, test, val)

    hb_k = lax.fori_loop(0, 15, bit_step, start)

    below = _key32_of_key16(jnp.maximum(hb_k - 1, BF16_KEY_NEG_INF))
    lo = below + ((_key32_of_key16(hb_k) - below) >> 1) - 1
    hi = _key32_of_key16(jnp.minimum(hb_k + 1, BF16_KEY_POS_INF))

    def bisect(count_ge):
        def step(i, carry):
            lo, hi, cge = carry
            mid = lo + ((hi - lo) >> 1)
            c = count_ge(_f32_from_key(mid)[None])
            ok = c >= kf
            return jnp.where(ok, mid, lo), jnp.where(ok, hi, mid), jnp.where(ok, c, cge)

        lo_k, _, cge = lax.fori_loop(0, BISECT_STEPS, step, (lo, hi, jnp.full((1, TQ), kf, F32)))
        thr_ref[...] = jnp.broadcast_to(_f32_from_key(lo_k), (SUBLANES, TQ))
        cge_ref[...] = jnp.broadcast_to(cge, (SUBLANES, TQ))

    lo_f, hi_f = _f32_from_key(lo), _f32_from_key(hi)
    ninf = jnp.full((SUBLANES, TQ), -jnp.inf, F32)

    def reduce_tile(c, carry):
        above, spill = carry
        k0 = pl.multiple_of(c * TQ, TQ)
        x3 = _tile3(scores_ref[pl.ds(k0, TQ), :])
        top = [ninf] * KEEP
        for g in range(TQ // SUBLANES):
            x = x3[g]
            ge_hi = x >= hi_f
            above = above + jnp.where(ge_hi, 1.0, 0.0)
            t = jnp.where((x >= lo_f) & ~ge_hi, x, -jnp.inf)
            for r in range(KEEP):
                top[r], t = jnp.maximum(top[r], t), jnp.minimum(top[r], t)
            spill = jnp.maximum(spill, t)
        for r in range(KEEP):
            red_ref[c, r] = top[r]
        return above, spill

    above, spill = lax.fori_loop(0, n_tiles, reduce_tile, (jnp.zeros((SUBLANES, TQ), F32), ninf))
    above = jnp.sum(above, axis=0, keepdims=True)
    reduced_ok = jnp.max(spill) == -jnp.inf

    @pl.when(reduced_ok)
    def _():
        def count_ge(t3):
            def tile(c, cnt):
                for r in range(KEEP):
                    cnt = cnt + jnp.where(red_ref[c, r] >= t3[0], 1.0, 0.0)
                return cnt
            cnt = sweep_count(tile, jnp.zeros((SUBLANES, TQ), F32))
            return above + jnp.sum(cnt, axis=0, keepdims=True)
        bisect(count_ge)

    @pl.when(jnp.logical_not(reduced_ok))
    def _():
        bisect(lambda t3: count_f32(lambda sc, r0: sc >= t3))

    thr = thr_ref[0:1, :]
    cge = cge_ref[0:1, :]

    @pl.when(jnp.max(jnp.abs(cge - kf)) > 0.0)
    def _():
        thr3 = thr[None]
        need = kf - count_f32(lambda sc, r0: sc > thr3)
        sub = (lax.broadcasted_iota(I32, (SEL_ROWS // SUBLANES, SUBLANES, TQ), 0) * SUBLANES
               + lax.broadcasted_iota(I32, (SEL_ROWS // SUBLANES, SUBLANES, TQ), 1))

        def cut_step(i, cut):
            test = cut | (jnp.int32(1) << (seq_bits - 1 - i))
            f = count_f32(lambda sc, r0: (sc == thr3) & (r0 + sub < test[None]))
            return jnp.where(f < need, test, cut)

        cut = lax.fori_loop(0, seq_bits, cut_step, jnp.zeros((1, TQ), I32))

        def demote(c, carry):
            r0 = pl.multiple_of(c * SEL_ROWS, SEL_ROWS)
            sc = _tile3(scores_ref[pl.ds(r0, SEL_ROWS), :])
            scores_ref[pl.ds(r0, SEL_ROWS), :] = jnp.where(
                (sc == thr3) & (r0 + sub > cut[None]), NEG, sc).reshape(SEL_ROWS, TQ)
            return carry

        lax.fori_loop(0, n_tiles * (TQ // SEL_ROWS), demote, 0)

    def consume(t, t_true, s_ref, state, edge):
        k0 = pl.multiple_of(t * TQ, TQ)
        sel = _tile3(scores_ref[pl.ds(k0, TQ), :]) >= thr[None]
        if edge:
            sel = sel & _edge_mask(j, t_true)
        bias = jnp.where(sel, 0.0, NEG)
        return tuple(
            _softmax_tile(_tile3(s_ref[hh]) + bias, vat_ref[0, t, hh * VT_ROWS:(hh + 1) * VT_ROWS, :],
                          state[hh], acc_ref, hh)
            for hh in range(N_HEADS))

    qk = functools.partial(_qk_tile, ka_ref, qa_ref)
    _sweep_tiles(j, (s0_ref, s1_ref), qk, consume, _init_state(acc_ref))
    _write_heads(o_ref, acc_ref)


def _dsa_call(a_mat, kc, idx_out, qa, ka, vat):
    bsz, seq, _ = qa.shape
    topk = min(DSA_TOPK_MAX, seq // 4)
    seq_bits = (seq - 1).bit_length()
    kern = functools.partial(_dsa_kernel, topk=topk, seq_bits=seq_bits)
    return pl.pallas_call(
        kern,
        grid=(bsz, seq // TQ),
        in_specs=[
            pl.BlockSpec((1, TQ, IDX_HEADS * IDX_K), lambda b, j: (b, j, 0)),
            pl.BlockSpec((1, seq, IDX_K), lambda b, j: (b, 0, 0)),
            pl.BlockSpec((1, TQ, LANES), lambda b, j: (b, j, 0)),
            pl.BlockSpec((1, TQ, BR_W), lambda b, j: (b, j, 0)),
            pl.BlockSpec((1, seq, BR_W), lambda b, j: (b, 0, 0)),
            pl.BlockSpec((1, seq // TQ, N_HEADS * VT_ROWS, TQ), lambda b, j: (b, 0, 0, 0)),
        ],
        out_specs=pl.BlockSpec((1, TQ, BR_W), lambda b, j: (b, j, 0)),
        out_shape=jax.ShapeDtypeStruct((bsz, seq, BR_W), BF16),
        scratch_shapes=[
            pltpu.VMEM((seq, TQ), F32),
            pltpu.VMEM((seq, TQ), BF16),
            pltpu.VMEM((seq // TQ, KEEP, SUBLANES, TQ), F32),
            pltpu.VMEM((SUBLANES, TQ), F32),
            pltpu.VMEM((SUBLANES, TQ), F32),
            pltpu.VMEM((N_HEADS, TQ, TQ), F32),
            pltpu.VMEM((N_HEADS, TQ, TQ), F32),
            pltpu.VMEM((N_HEADS * VT_ROWS, TQ), F32),
        ],
        compiler_params=pltpu.CompilerParams(
            dimension_semantics=("arbitrary", "arbitrary"), vmem_limit_bytes=VMEM_LIMIT),
        name="dsa",
    )(a_mat, kc, idx_out, qa, ka, vat)


def _moba_kernel(qb_ref, kb_ref, vbt_ref, kmean_ref, o_ref, bias_ref, s0_ref, s1_ref, acc_ref, *, n_sel):
    j = pl.program_id(1)
    nb = kmean_ref.shape[1]
    blk = lax.broadcasted_iota(I32, (nb, TQ), 0).astype(F32)
    jf = j.astype(F32)

    for hh in range(N_HEADS):
        hs = slice(hh * HEAD_DIM, (hh + 1) * HEAD_DIM)
        km_hi, km_lo = _split_bf16(kmean_ref[0, :, hs])
        q = qb_ref[0, :, hs]
        gate = jnp.where(blk < jf, _dot_nt(km_hi, q) + _dot_nt(km_lo, q), NEG)
        chosen = jnp.zeros((nb, TQ), F32)
        for _ in range(n_sel):
            best = jnp.max(gate, axis=0, keepdims=True)
            first = jnp.min(jnp.where(gate == best, blk, float(nb)), axis=0, keepdims=True)
            hit = blk == first
            chosen = jnp.where(hit, 1.0, chosen)
            gate = jnp.where(hit, -jnp.inf, gate)
        bias = jnp.where((chosen > 0.0) & (blk < jf), 0.0, NEG)
        for n in range(nb):
            bias_ref[hh, n] = jnp.broadcast_to(bias[n:n + 1, :], (SUBLANES, TQ))

    def consume(t, t_true, s_ref, state, edge):
        new = []
        for hh in range(N_HEADS):
            if edge:
                s3 = jnp.where(_edge_mask(j, t_true), _tile3(s_ref[hh]), NEG)
            else:
                s3 = _tile3(s_ref[hh]) + bias_ref[hh, t][None]
            new.append(_softmax_tile(s3, vbt_ref[0, t, hh * VT_ROWS:(hh + 1) * VT_ROWS, :], state[hh], acc_ref, hh))
        return tuple(new)

    qk = functools.partial(_qk_tile, kb_ref, qb_ref)
    _sweep_tiles(j, (s0_ref, s1_ref), qk, consume, _init_state(acc_ref))
    _write_heads(o_ref, acc_ref)


def _moba_call(qb, kb, vbt, kmean):
    bsz, seq, _ = qb.shape
    nb = seq // MOBA_BLOCK
    kern = functools.partial(_moba_kernel, n_sel=min(MOBA_TOPK, nb - 1))
    return pl.pallas_call(
        kern,
        grid=(bsz, seq // TQ),
        in_specs=[
            pl.BlockSpec((1, TQ, BR_W), lambda b, j: (b, j, 0)),
            pl.BlockSpec((1, seq, BR_W), lambda b, j: (b, 0, 0)),
            pl.BlockSpec((1, seq // TQ, N_HEADS * VT_ROWS, TQ), lambda b, j: (b, 0, 0, 0)),
            pl.BlockSpec((1, nb, BR_W), lambda b, j: (b, 0, 0)),
        ],
        out_specs=pl.BlockSpec((1, TQ, BR_W), lambda b, j: (b, j, 0)),
        out_shape=jax.ShapeDtypeStruct((bsz, seq, BR_W), BF16),
        scratch_shapes=[
            pltpu.VMEM((N_HEADS, nb, SUBLANES, TQ), F32),
            pltpu.VMEM((N_HEADS, TQ, TQ), F32),
            pltpu.VMEM((N_HEADS, TQ, TQ), F32),
            pltpu.VMEM((N_HEADS * VT_ROWS, TQ), F32),
        ],
        compiler_params=pltpu.CompilerParams(
            dimension_semantics=("arbitrary", "arbitrary"), vmem_limit_bytes=VMEM_LIMIT),
        name="moba",
    )(qb, kb, vbt, kmean)


def _out_ffn_kernel(x_ref, oa_ref, ob_ref, ga_ref, gb_ref, mod_ref, gffn_ref,
                    wba_ref, wbb_ref, wo_ref, wgu_ref, wd_ref, o_ref, act_ref):
    merged = (ga_ref[0].astype(F32) * _dot(oa_ref[0], wba_ref[...])
              + gb_ref[0].astype(F32) * _dot(ob_ref[0], wbb_ref[...]))
    x1 = x_ref[0] + mod_ref[0, 2:3, :] * _dot(merged.astype(BF16), wo_ref[...])
    y = x1 * lax.rsqrt(jnp.mean(x1 * x1, axis=-1, keepdims=True) + EPS) * gffn_ref[...]
    h = (y * (1.0 + mod_ref[0, 4:5, :]) + mod_ref[0, 3:4, :]).astype(BF16)
    for c0 in range(0, D_FF, FF_CHUNK):
        g = _dot(h, wgu_ref[:, c0:c0 + FF_CHUNK])
        u = _dot(h, wgu_ref[:, D_FF + c0:D_FF + c0 + FF_CHUNK])
        act_ref[:, c0:c0 + FF_CHUNK] = (g * jax.nn.sigmoid(g) * u).astype(BF16)
    o_ref[0] = x1 + mod_ref[0, 5:6, :] * _dot(act_ref[...], wd_ref[...])


def _out_ffn_call(x, oa, ob, ga, gb, mod3, gffn, wba, wbb, wo, wgu, wd):
    bsz, seq, d = x.shape
    tm = TM_FFN
    const = lambda b, i: (0, 0)
    tok = lambda b, i: (b, i, 0)
    resident = lambda a: pl.BlockSpec(a.shape, const, pipeline_mode=pl.Buffered(1))
    return pl.pallas_call(
        _out_ffn_kernel,
        grid=(bsz, seq // tm),
        in_specs=[
            pl.BlockSpec((1, tm, d), tok),
            pl.BlockSpec((1, tm, BR_W), tok),
            pl.BlockSpec((1, tm, BR_W), tok),
            pl.BlockSpec((1, tm, d), tok),
            pl.BlockSpec((1, tm, d), tok),
            pl.BlockSpec((1, N_MOD, d), lambda b, i: (b, 0, 0)),
            pl.BlockSpec((1, d), const),
            resident(wba), resident(wbb), resident(wo), resident(wgu), resident(wd),
        ],
        out_specs=pl.BlockSpec((1, tm, d), tok),
        out_shape=jax.ShapeDtypeStruct((bsz, seq, d), F32),
        scratch_shapes=[pltpu.VMEM((tm, D_FF), BF16)],
        compiler_params=pltpu.CompilerParams(
            dimension_semantics=("arbitrary", "arbitrary"), vmem_limit_bytes=VMEM_LIMIT),
        name="out_ffn",
    )(x, oa, ob, ga, gb, mod3, gffn, wba, wbb, wo, wgu, wd)


def _rope_tables(seq, dim):
    inv = ROPE_THETA ** (-jnp.arange(0, dim, 2, dtype=F32) / dim)
    ang = jnp.arange(seq, dtype=F32)[:, None] * inv[None, :]
    cos, sin = jnp.cos(ang), jnp.sin(ang)
    reps = LANES // dim
    return (jnp.tile(jnp.concatenate([cos, cos], axis=-1), (1, reps)),
            jnp.tile(jnp.concatenate([-sin, sin], axis=-1), (1, reps)))


def _layer(x, mod, g_mix, g_ffn, w_in, g_qa, g_ka, g_qb, g_kb, w_br_a, w_br_b, w_out, w_gu, w_down):
    bsz, seq, d = x.shape
    mod3 = mod.reshape(bsz, N_MOD, d)

    o = 0
    cols = {}
    for name, width in (("qa", BR_W), ("ka", BR_W), ("va", BR_W), ("qi", IDX_HEADS * IDX_DIM),
                        ("ki", IDX_DIM), ("wi", IDX_HEADS), ("qb", BR_W), ("kb", BR_W), ("vb", BR_W),
                        ("ga", D_MODEL), ("gb", D_MODEL)):
        cols[name] = w_in[:, o:o + width]
        o += width
    wqk = jnp.concatenate([cols["qa"], cols["ka"], cols["qb"], cols["kb"]], axis=1).astype(BF16)
    wvt = jnp.concatenate([cols["va"], cols["vb"]], axis=1).T.astype(BF16)
    wg = jnp.concatenate([cols["ga"], cols["gb"]], axis=1).astype(BF16)
    pad = IDX_OUT_W - IDX_HEADS * IDX_DIM - IDX_DIM - IDX_HEADS
    wi = jnp.concatenate([cols["qi"], cols["ki"], cols["wi"], jnp.zeros((d, pad), F32)], axis=1).astype(BF16)
    gains = jnp.stack([g_qa, g_ka, g_qb, g_kb])
    cosh, sinh = _rope_tables(seq, HEAD_DIM)
    cosi, sini = _rope_tables(seq, IDX_DIM)

    qa, ka, qb, kb, vat, vbt, ga, gb, a_mat, kc, kw, kmean = _proj_call(
        x, mod3, g_mix.reshape(1, d), wqk, wvt, wg, wi, gains, cosh, sinh, cosi, sini)

    oa = _dsa_call(a_mat, kc, kw, qa, ka, vat)
    ob = _moba_call(qb, kb, vbt, kmean.reshape(bsz, seq // MOBA_BLOCK, BR_W))

    return _out_ffn_call(x, oa, ob, ga, gb, mod3, g_ffn.reshape(1, d),
                         w_br_a.astype(BF16), w_br_b.astype(BF16), w_out.astype(BF16),
                         w_gu.astype(BF16), w_down.astype(BF16))


def kernel(x, c, w_mod, b_mod, g_mix_norm, g_ffn_norm, w_in, g_q_dsa, g_k_dsa, g_q_moba, g_k_moba,
           w_br_dsa, w_br_moba, w_out, w_gate_up, w_down):
    assert x.shape[1] % TM_PROJ == 0 and x.shape[2] == D_MODEL and w_gate_up.shape[2] == 2 * D_FF
    for l in range(w_mod.shape[0]):
        mod = _mod_call(c, w_mod[l], b_mod[l])
        x = _layer(x, mod, g_mix_norm[l], g_ffn_norm[l], w_in[l], g_q_dsa[l], g_k_dsa[l], g_q_moba[l],
                   g_k_moba[l], w_br_dsa[l], w_br_moba[l], w_out[l], w_gate_up[l], w_down[l])
    return x
```

```python
import functools

import jax
import jax.numpy as jnp
from jax import lax
from jax.experimental import pallas as pl
from jax.experimental.pallas import tpu as pltpu

F32 = jnp.float32
BF16 = jnp.bfloat16
I32 = jnp.int32
I16 = jnp.int16

D_MODEL = 1024
HEAD_DIM = 128
N_HEADS = 4
BR_W = N_HEADS * HEAD_DIM
IDX_HEADS = 8
IDX_DIM = 64
DSA_TOPK_MAX = 256
MOBA_BLOCK = 256
MOBA_TOPK = 3
D_FF = 2816
ROPE_THETA = 10000.0
EPS = 1e-6
NEG = -1e30
LOG2_E = 1.4426950408889634
M_INIT = -1e29
N_MOD = 6
IDX_OUT_W = 640
IDX_K = 256
I16_MIN = -(2 ** 15)
BF16_KEY_NEG_INF = (0xFF80 ^ 0x7FFF) - 0x10000
BF16_KEY_POS_INF = 0x7F80
BISECT_STEPS = 17

LANES = 128
SUBLANES = 8
PACK_ROWS = 16
VT_ROWS = HEAD_DIM + PACK_ROWS
TQ = 256
TM_PROJ = 512
PROJ_ROWS = 256
TM_FFN = 512
FF_CHUNK = 256
SEL_ROWS = 32
KEEP = 3
SCORE_GROUPS = (4, 2, 1)
VMEM_LIMIT = 56 * 1024 * 1024

_NT = (((1,), (1,)), ((), ()))
_HEAD_SLICES = tuple(slice(h * HEAD_DIM, (h + 1) * HEAD_DIM) for h in range(N_HEADS))


def _split_bf16(x):
    hi = x.astype(BF16)
    lo = (x - hi.astype(F32)).astype(BF16)
    return hi, lo


def _dot(a, b):
    return jnp.dot(a, b, preferred_element_type=F32)


def _dot_nt(a, b):
    return lax.dot_general(a, b, _NT, preferred_element_type=F32)


def _mod_kernel(c_ref, w_ref, b_ref, o_ref):
    c = c_ref[...]
    a_hi, a_lo = _split_bf16(c * jax.nn.sigmoid(c))
    w_hi, w_lo = _split_bf16(w_ref[...])
    o_ref[...] = _dot(a_hi, w_hi) + _dot(a_lo, w_hi) + _dot(a_hi, w_lo) + b_ref[...]


def _mod_call(c, w_mod, b_mod):
    bsz, d = c.shape
    n = w_mod.shape[1]
    tn = 2048
    return pl.pallas_call(
        _mod_kernel,
        grid=(n // tn,),
        in_specs=[
            pl.BlockSpec((bsz, d), lambda i: (0, 0)),
            pl.BlockSpec((d, tn), lambda i: (0, i)),
            pl.BlockSpec((1, tn), lambda i: (0, i)),
        ],
        out_specs=pl.BlockSpec((bsz, tn), lambda i: (0, i)),
        out_shape=jax.ShapeDtypeStruct((bsz, n), F32),
        compiler_params=pltpu.CompilerParams(vmem_limit_bytes=VMEM_LIMIT),
        name="mod",
    )(c, w_mod, b_mod.reshape(1, n))


def _rope_partner_64(x):
    lane = lax.broadcasted_iota(I32, x.shape, 1)
    return jnp.where((lane % IDX_DIM) < IDX_DIM // 2,
                     pltpu.roll(x, LANES - IDX_DIM // 2, 1), pltpu.roll(x, IDX_DIM // 2, 1))


def _proj_kernel(x_ref, mod_ref, gmix_ref, wqk_ref, wvt_ref, wg_ref, wi_ref, gains_ref,
                 cosh_ref, sinh_ref, cosi_ref, sini_ref,
                 qa_ref, ka_ref, qb_ref, kb_ref, vat_ref, vbt_ref, ga_ref, gb_ref, a_ref, kc_ref, kw_ref, kmean_ref):
    for blk in range(x_ref.shape[1] // PROJ_ROWS):
        _proj_rows(blk, x_ref, mod_ref, gmix_ref, wqk_ref, wvt_ref, wg_ref, wi_ref, gains_ref,
                   cosh_ref, sinh_ref, cosi_ref, sini_ref,
                   qa_ref, ka_ref, qb_ref, kb_ref, vat_ref, vbt_ref, ga_ref, gb_ref, a_ref, kc_ref, kw_ref, kmean_ref)


def _proj_rows(blk, x_ref, mod_ref, gmix_ref, wqk_ref, wvt_ref, wg_ref, wi_ref, gains_ref,
               cosh_ref, sinh_ref, cosi_ref, sini_ref,
               qa_ref, ka_ref, qb_ref, kb_ref, vat_ref, vbt_ref, ga_ref, gb_ref, a_ref, kc_ref, kw_ref, kmean_ref):
    rows = slice(blk * PROJ_ROWS, (blk + 1) * PROJ_ROWS)
    x = x_ref[0, rows, :]
    y = x * lax.rsqrt(jnp.mean(x * x, axis=-1, keepdims=True) + EPS) * gmix_ref[...]
    h = y * (1.0 + mod_ref[0, 1:2, :]) + mod_ref[0, 0:1, :]
    h_hi = h.astype(BF16)
    cosh, sinh = cosh_ref[rows, :], sinh_ref[rows, :]
    scale = HEAD_DIM ** -0.5 * LOG2_E

    for gi, out_ref in enumerate((qa_ref, ka_ref, qb_ref, kb_ref)):
        p = _dot(h_hi, wqk_ref[:, gi * BR_W:(gi + 1) * BR_W])
        for hh in range(N_HEADS):
            ph = p[:, hh * HEAD_DIM:(hh + 1) * HEAD_DIM]
            yh = ph * lax.rsqrt(jnp.mean(ph * ph, axis=-1, keepdims=True) + EPS) * gains_ref[gi:gi + 1, :]
            r = yh * cosh + pltpu.roll(yh, HEAD_DIM // 2, 1) * sinh
            if gi == 3:
                kmean_ref[0, blk, :, hh * HEAD_DIM:(hh + 1) * HEAD_DIM] = jnp.mean(r, axis=0, keepdims=True)
            if gi % 2 == 0:
                r = r * scale
            out_ref[0, rows, hh * HEAD_DIM:(hh + 1) * HEAD_DIM] = r.astype(BF16)

    vt = _dot_nt(wvt_ref[...], h_hi)
    ones = jnp.ones((VT_ROWS - HEAD_DIM, TQ), BF16)
    for gi, out_ref in enumerate((vat_ref, vbt_ref)):
        for hh in range(N_HEADS):
            src = gi * BR_W + hh * HEAD_DIM
            out_ref[0, blk, hh * VT_ROWS:hh * VT_ROWS + HEAD_DIM, :] = vt[src:src + HEAD_DIM, :].astype(BF16)
            out_ref[0, blk, hh * VT_ROWS + HEAD_DIM:(hh + 1) * VT_ROWS, :] = ones

    for gi, out_ref in enumerate((ga_ref, gb_ref)):
        g = _dot(h_hi, wg_ref[:, gi * D_MODEL:(gi + 1) * D_MODEL])
        out_ref[0, rows, :] = jax.nn.sigmoid(g).astype(BF16)

    pi = _dot(h_hi, wi_ref[...])
    cosi, sini = cosi_ref[rows, :], sini_ref[rows, :]
    first = lax.broadcasted_iota(I32, (PROJ_ROWS, LANES), 1) < IDX_DIM
    half = LANES // 2

    def hi_lo(v):
        hi = v.astype(BF16).astype(F32)
        return hi, v - hi

    n_q = IDX_HEADS * IDX_DIM // LANES
    for g4 in range(n_q):
        xg = pi[:, g4 * LANES:(g4 + 1) * LANES]
        hi, lo = hi_lo(xg * cosi + _rope_partner_64(xg) * sini)
        hi_r, lo_r = pltpu.roll(hi, half, 1), pltpu.roll(lo, half, 1)
        for head, parts in ((2 * g4, (jnp.where(first, hi, lo_r), jnp.where(first, hi, 0.0))),
                            (2 * g4 + 1, (jnp.where(first, hi_r, lo), jnp.where(first, hi_r, 0.0)))):
            for pi_, part in enumerate(parts):
                a_ref[0, rows, head * IDX_K + pi_ * LANES:head * IDX_K + (pi_ + 1) * LANES] = part.astype(BF16)
    xg = pi[:, n_q * LANES:(n_q + 1) * LANES]
    roped = xg * cosi + _rope_partner_64(xg) * sini
    hi, lo = hi_lo(roped)
    kc_ref[0, rows, :LANES] = jnp.where(first, hi, pltpu.roll(hi, half, 1)).astype(BF16)
    kc_ref[0, rows, LANES:] = jnp.where(first, lo, 0.0).astype(BF16)
    w_scale = (IDX_HEADS ** -0.5) * (IDX_DIM ** -0.5)
    kw_ref[0, rows, :] = jnp.where(first, roped, xg * w_scale)


def _proj_call(x, mod3, gmix, wqk, wvt, wg, wi, gains, cosh, sinh, cosi, sini):
    bsz, seq, d = x.shape
    tm = TM_PROJ
    nb = seq // MOBA_BLOCK
    const = lambda b, i: (0, 0)
    tok = lambda b, i: (b, i, 0)
    tab = lambda b, i: (i, 0)
    resident = lambda a: pl.BlockSpec(a.shape, const, pipeline_mode=pl.Buffered(1))
    out_shape = (
        [jax.ShapeDtypeStruct((bsz, seq, BR_W), BF16)] * 4
        + [jax.ShapeDtypeStruct((bsz, seq // TQ, N_HEADS * VT_ROWS, TQ), BF16)] * 2
        + [jax.ShapeDtypeStruct((bsz, seq, D_MODEL), BF16)] * 2
        + [jax.ShapeDtypeStruct((bsz, seq, IDX_HEADS * IDX_K), BF16),
           jax.ShapeDtypeStruct((bsz, seq, IDX_K), BF16),
           jax.ShapeDtypeStruct((bsz, seq, LANES), F32),
           jax.ShapeDtypeStruct((bsz, nb, 1, BR_W), F32)]
    )
    out_specs = (
        [pl.BlockSpec((1, tm, BR_W), tok)] * 4
        + [pl.BlockSpec((1, tm // TQ, N_HEADS * VT_ROWS, TQ), lambda b, i: (b, i, 0, 0))] * 2
        + [pl.BlockSpec((1, tm, D_MODEL), tok)] * 2
        + [pl.BlockSpec((1, tm, IDX_HEADS * IDX_K), tok),
           pl.BlockSpec((1, tm, IDX_K), tok),
           pl.BlockSpec((1, tm, LANES), tok),
           pl.BlockSpec((1, tm // MOBA_BLOCK, 1, BR_W), lambda b, i: (b, i, 0, 0))]
    )
    return pl.pallas_call(
        _proj_kernel,
        grid=(bsz, seq // tm),
        in_specs=[
            pl.BlockSpec((1, tm, d), tok),
            pl.BlockSpec((1, N_MOD, d), lambda b, i: (b, 0, 0)),
            pl.BlockSpec((1, d), const),
            resident(wqk), resident(wvt), resident(wg), resident(wi),
            pl.BlockSpec(gains.shape, const),
            pl.BlockSpec((tm, LANES), tab),
            pl.BlockSpec((tm, LANES), tab),
            pl.BlockSpec((tm, LANES), tab),
            pl.BlockSpec((tm, LANES), tab),
        ],
        out_specs=out_specs,
        out_shape=out_shape,
        compiler_params=pltpu.CompilerParams(
            dimension_semantics=("arbitrary", "arbitrary"), vmem_limit_bytes=VMEM_LIMIT),
        name="proj",
    )(x, mod3, gmix, wqk, wvt, wg, wi, gains, cosh, sinh, cosi, sini)


def _tile3(x):
    return x.reshape(x.shape[0] // SUBLANES, SUBLANES, x.shape[1])


def _allreduce_sublanes(x, op):
    for shift in (4, 2, 1):
        x = op(x, pltpu.roll(x, shift, 0))
    return x


def _softmax_tile(s3, vt, m_old, acc_ref, hh):
    rows = slice(hh * VT_ROWS, (hh + 1) * VT_ROWS)
    m_new = jnp.maximum(m_old, _allreduce_sublanes(jnp.max(s3, axis=0), jnp.maximum))
    alpha = jnp.exp2(m_old - m_new)
    p = jnp.exp2(s3 - m_new[None]).reshape(s3.shape[0] * SUBLANES, TQ).astype(BF16)
    acc_ref[rows, :] = (_tile3(acc_ref[rows, :]) * alpha[None]).reshape(VT_ROWS, TQ) + _dot(vt, p)
    return m_new


def _qk_tile(k_ref, q_ref, t, s_ref):
    k0 = pl.multiple_of(t * TQ, TQ)
    for hh, hs in enumerate(_HEAD_SLICES):
        s_ref[hh] = _dot_nt(k_ref[0, pl.ds(k0, TQ), hs], q_ref[0, :, hs])


def _edge_mask(j, t_true):
    shape = (TQ // SUBLANES, SUBLANES, TQ)
    row = lax.broadcasted_iota(I32, shape, 0) * SUBLANES + lax.broadcasted_iota(I32, shape, 1)
    return row - lax.broadcasted_iota(I32, shape, 2) <= (j - t_true) * TQ


def _sweep_tiles(j, s_refs, qk, consume, state):
    s0, s1 = s_refs
    last_pair = j // 2
    qk(0, s0)

    def body(p, st):
        t0 = 2 * p
        qk(t0 + 1, s1)
        st = consume(t0, t0, s0, st, False)
        qk(t0 + 2, s0)
        return consume(t0 + 1, t0 + 1, s1, st, False)

    state = lax.fori_loop(0, last_pair // 2, lambda q, st: body(2 * q + 1, body(2 * q, st)), state)
    state = lax.fori_loop(2 * (last_pair // 2), last_pair, body, state)
    ta = 2 * last_pair

    @pl.when(ta == j)
    def _():
        consume(j, j, s0, state, True)

    @pl.when(ta != j)
    def _():
        qk(j, s1)
        consume(j, j, s1, consume(ta, ta, s0, state, False), True)


def _init_state(acc_ref):
    acc_ref[...] = jnp.zeros(acc_ref.shape, F32)
    return (jnp.full((SUBLANES, TQ), M_INIT, F32),) * N_HEADS


def _write_heads(o_ref, acc_ref):
    for hh, hs in enumerate(_HEAD_SLICES):
        base = hh * VT_ROWS
        inv = 1.0 / acc_ref[base + HEAD_DIM:base + HEAD_DIM + SUBLANES, :]
        o_t = (_tile3(acc_ref[base:base + HEAD_DIM, :]) * inv[None]).reshape(HEAD_DIM, TQ)
        o_ref[0, :, hs] = o_t.T.astype(BF16)


def _f32_from_key(key):
    bits = jnp.where(key < 0, key ^ jnp.int32(0x7FFFFFFF), key)
    return lax.bitcast_convert_type(bits, F32)


def _key32_of_key16(key16):
    return (key16 << 16) + jnp.where(key16 < 0, 0xFFFF, 0)


def _bf16_from_key16(key16):
    return _f32_from_key(_key32_of_key16(key16))


def _dsa_kernel(a_ref, kc_ref, wk_ref, qa_ref, ka_ref, vat_ref, o_ref,
                scores_ref, hb_ref, red_ref, thr_ref, cge_ref, s0_ref, s1_ref, acc_ref, *, topk, seq_bits):
    j = pl.program_id(1)
    n_tiles = j + 1
    w_rows = wk_ref[0].T[IDX_DIM:IDX_DIM + IDX_HEADS, :]
    w8 = [jnp.broadcast_to(w_rows[hh:hh + 1, :], (SUBLANES, TQ)) for hh in range(IDX_HEADS)]

    def score_tile(c, maybe_diagonal):
        k0 = pl.multiple_of(c * TQ, TQ)
        kc = kc_ref[0, pl.ds(k0, TQ), :]
        acc = jnp.zeros((TQ // SUBLANES, SUBLANES, TQ), F32)
        for hh in range(IDX_HEADS):
            lg = _dot_nt(kc, a_ref[0, :, hh * IDX_K:(hh + 1) * IDX_K])
            acc = acc + w8[hh][None] * jnp.maximum(_tile3(lg), 0.0)
        if maybe_diagonal:
            acc = jnp.where(_edge_mask(j, c), acc, NEG)
        acc = acc.reshape(TQ, TQ)
        scores_ref[pl.ds(k0, TQ), :] = acc
        hb_ref[pl.ds(k0, TQ), :] = acc.astype(BF16)

    def score_run(start, n_trips, width):
        def trip(g, carry):
            for u in range(width):
                score_tile(start + width * g + u, u == width - 1)
            return carry

        lax.fori_loop(0, n_trips, trip, 0)
        return start + width * n_trips

    done = 0
    for width in SCORE_GROUPS:
        done = score_run(done, (n_tiles - done) // width, width)

    def load_packed(c, u):
        r0 = pl.multiple_of(c * TQ + u * SEL_ROWS, SEL_ROWS)
        return hb_ref[pl.ds(r0, SEL_ROWS), :].reshape(SEL_ROWS // PACK_ROWS, PACK_ROWS, TQ)

    def load_scores(c, u):
        r0 = pl.multiple_of(c * TQ + u * SEL_ROWS, SEL_ROWS)
        return _tile3(scores_ref[pl.ds(r0, SEL_ROWS), :]), r0

    def sweep_count(tile, zero):
        cnt = lax.fori_loop(0, n_tiles // 2, lambda p, cnt: tile(2 * p + 1, tile(2 * p, cnt)), zero)
        return lax.fori_loop(2 * (n_tiles // 2), n_tiles, tile, cnt)

    one16, zero16 = jnp.int16(1), jnp.int16(0)

    def count_bf16(test):
        t = jnp.broadcast_to(test, (PACK_ROWS, TQ)).astype(BF16)[None]

        def tile(c, cnt):
            for u in range(TQ // SEL_ROWS):
                cnt = cnt + jnp.where(load_packed(c, u) >= t, one16, zero16)
            return cnt

        cnt = sweep_count(tile, jnp.zeros((SEL_ROWS // PACK_ROWS, PACK_ROWS, TQ), I16))
        return jnp.sum(jnp.sum(cnt.astype(F32), axis=0), axis=0, keepdims=True)

    def count_f32(pred):
        def tile(c, cnt):
            for u in range(TQ // SEL_ROWS):
                sc, r0 = load_scores(c, u)
                cnt = cnt + jnp.where(pred(sc, r0), 1.0, 0.0)
            return cnt

        cnt = sweep_count(tile, jnp.zeros((SEL_ROWS // SUBLANES, SUBLANES, TQ), F32))
        return jnp.sum(jnp.sum(cnt, axis=0), axis=0, keepdims=True)

    kf = float(topk)

    c0 = count_bf16(jnp.zeros((1, TQ), F32))
    start = jnp.where(c0 >= kf, 0, I16_MIN).astype(I32)

    def bit_step(i, val):
        test = val | (jnp.int32(1) << (14 - i))
        return jnp.where(count_bf16(_bf16_from_key16(test)) >= kf, test, val)

    hb_k = lax.fori_loop(0, 15, bit_step, start)

    below = _key32_of_key16(jnp.maximum(hb_k - 1, BF16_KEY_NEG_INF))
    lo = below + ((_key32_of_key16(hb_k) - below) >> 1) - 1
    hi = _key32_of_key16(jnp.minimum(hb_k + 1, BF16_KEY_POS_INF))

    def bisect(count_ge):
        def step(i, carry):
            lo, hi, cge = carry
            mid = lo + ((hi - lo) >> 1)
            c = count_ge(_f32_from_key(mid)[None])
            ok = c >= kf
            return jnp.where(ok, mid, lo), jnp.where(ok, hi, mid), jnp.where(ok, c, cge)

        lo_k, _, cge = lax.fori_loop(0, BISECT_STEPS, step, (lo, hi, jnp.full((1, TQ), kf, F32)))
        thr_ref[...] = jnp.broadcast_to(_f32_from_key(lo_k), (SUBLANES, TQ))
        cge_ref[...] = jnp.broadcast_to(cge, (SUBLANES, TQ))

    lo_f, hi_f = _f32_from_key(lo), _f32_from_key(hi)
    ninf = jnp.full((SUBLANES, TQ), -jnp.inf, F32)

    def reduce_tile(c, carry):
        above, spill = carry
        k0 = pl.multiple_of(c * TQ, TQ)
        x3 = _tile3(scores_ref[pl.ds(k0, TQ), :])
        top = [ninf] * KEEP
        for g in range(TQ // SUBLANES):
            x = x3[g]
            ge_hi = x >= hi_f
            above = above + jnp.where(ge_hi, 1.0, 0.0)
            t = jnp.where((x >= lo_f) & ~ge_hi, x, -jnp.inf)
            for r in range(KEEP):
                top[r], t = jnp.maximum(top[r], t), jnp.minimum(top[r], t)
            spill = jnp.maximum(spill, t)
        for r in range(KEEP):
            red_ref[c, r] = top[r]
        return above, spill

    above, spill = lax.fori_loop(0, n_tiles, reduce_tile, (jnp.zeros((SUBLANES, TQ), F32), ninf))
    above = jnp.sum(above, axis=0, keepdims=True)
    reduced_ok = jnp.max(spill) == -jnp.inf

    @pl.when(reduced_ok)
    def _():
        def count_ge(t3):
            def tile(c, cnt):
                for r in range(KEEP):
                    cnt = cnt + jnp.where(red_ref[c, r] >= t3[0], 1.0, 0.0)
                return cnt
            cnt = sweep_count(tile, jnp.zeros((SUBLANES, TQ), F32))
            return above + jnp.sum(cnt, axis=0, keepdims=True)
        bisect(count_ge)

    @pl.when(jnp.logical_not(reduced_ok))
    def _():
        bisect(lambda t3: count_f32(lambda sc, r0: sc >= t3))

    thr = thr_ref[0:1, :]
    cge = cge_ref[0:1, :]

    @pl.when(jnp.max(jnp.abs(cge - kf)) > 0.0)
    def _():
        thr3 = thr[None]
        need = kf - count_f32(lambda sc, r0: sc > thr3)
        sub = (lax.broadcasted_iota(I32, (SEL_ROWS // SUBLANES, SUBLANES, TQ), 0) * SUBLANES
               + lax.broadcasted_iota(I32, (SEL_ROWS // SUBLANES, SUBLANES, TQ), 1))

        def cut_step(i, cut):
            test = cut | (jnp.int32(1) << (seq_bits - 1 - i))
            f = count_f32(lambda sc, r0: (sc == thr3) & (r0 + sub < test[None]))
            return jnp.where(f < need, test, cut)

        cut = lax.fori_loop(0, seq_bits, cut_step, jnp.zeros((1, TQ), I32))

        def demote(c, carry):
            r0 = pl.multiple_of(c * SEL_ROWS, SEL_ROWS)
            sc = _tile3(scores_ref[pl.ds(r0, SEL_ROWS), :])
            scores_ref[pl.ds(r0, SEL_ROWS), :] = jnp.where(
                (sc == thr3) & (r0 + sub > cut[None]), NEG, sc).reshape(SEL_ROWS, TQ)
            return carry

        lax.fori_loop(0, n_tiles * (TQ // SEL_ROWS), demote, 0)

    def consume(t, t_true, s_ref, state, edge):
        k0 = pl.multiple_of(t * TQ, TQ)
        sel = _tile3(scores_ref[pl.ds(k0, TQ), :]) >= thr[None]
        if edge:
            sel = sel & _edge_mask(j, t_true)
        bias = jnp.where(sel, 0.0, NEG)
        return tuple(
            _softmax_tile(_tile3(s_ref[hh]) + bias, vat_ref[0, t, hh * VT_ROWS:(hh + 1) * VT_ROWS, :],
                          state[hh], acc_ref, hh)
            for hh in range(N_HEADS))

    qk = functools.partial(_qk_tile, ka_ref, qa_ref)
    _sweep_tiles(j, (s0_ref, s1_ref), qk, consume, _init_state(acc_ref))
    _write_heads(o_ref, acc_ref)


def _dsa_call(a_mat, kc, idx_out, qa, ka, vat):
    bsz, seq, _ = qa.shape
    topk = min(DSA_TOPK_MAX, seq // 4)
    seq_bits = (seq - 1).bit_length()
    kern = functools.partial(_dsa_kernel, topk=topk, seq_bits=seq_bits)
    return pl.pallas_call(
        kern,
        grid=(bsz, seq // TQ),
        in_specs=[
            pl.BlockSpec((1, TQ, IDX_HEADS * IDX_K), lambda b, j: (b, j, 0)),
            pl.BlockSpec((1, seq, IDX_K), lambda b, j: (b, 0, 0)),
            pl.BlockSpec((1, TQ, LANES), lambda b, j: (b, j, 0)),
            pl.BlockSpec((1, TQ, BR_W), lambda b, j: (b, j, 0)),
            pl.BlockSpec((1, seq, BR_W), lambda b, j: (b, 0, 0)),
            pl.BlockSpec((1, seq // TQ, N_HEADS * VT_ROWS, TQ), lambda b, j: (b, 0, 0, 0)),
        ],
        out_specs=pl.BlockSpec((1, TQ, BR_W), lambda b, j: (b, j, 0)),
        out_shape=jax.ShapeDtypeStruct((bsz, seq, BR_W), BF16),
        scratch_shapes=[
            pltpu.VMEM((seq, TQ), F32),
            pltpu.VMEM((seq, TQ), BF16),
            pltpu.VMEM((seq // TQ, KEEP, SUBLANES, TQ), F32),
            pltpu.VMEM((SUBLANES, TQ), F32),
            pltpu.VMEM((SUBLANES, TQ), F32),
            pltpu.VMEM((N_HEADS, TQ, TQ), F32),
            pltpu.VMEM((N_HEADS, TQ, TQ), F32),
            pltpu.VMEM((N_HEADS * VT_ROWS, TQ), F32),
        ],
        compiler_params=pltpu.CompilerParams(
            dimension_semantics=("arbitrary", "arbitrary"), vmem_limit_bytes=VMEM_LIMIT),
        name="dsa",
    )(a_mat, kc, idx_out, qa, ka, vat)


def _moba_kernel(qb_ref, kb_ref, vbt_ref, kmean_ref, o_ref, bias_ref, s0_ref, s1_ref, acc_ref, *, n_sel):
    j = pl.program_id(1)
    nb = kmean_ref.shape[1]
    blk = lax.broadcasted_iota(I32, (nb, TQ), 0).astype(F32)
    jf = j.astype(F32)

    for hh in range(N_HEADS):
        hs = slice(hh * HEAD_DIM, (hh + 1) * HEAD_DIM)
        km_hi, km_lo = _split_bf16(kmean_ref[0, :, hs])
        q = qb_ref[0, :, hs]
        gate = jnp.where(blk < jf, _dot_nt(km_hi, q) + _dot_nt(km_lo, q), NEG)
        chosen = jnp.zeros((nb, TQ), F32)
        for _ in range(n_sel):
            best = jnp.max(gate, axis=0, keepdims=True)
            first = jnp.min(jnp.where(gate == best, blk, float(nb)), axis=0, keepdims=True)
            hit = blk == first
            chosen = jnp.where(hit, 1.0, chosen)
            gate = jnp.where(hit, -jnp.inf, gate)
        bias = jnp.where((chosen > 0.0) & (blk < jf), 0.0, NEG)
        for n in range(nb):
            bias_ref[hh, n] = jnp.broadcast_to(bias[n:n + 1, :], (SUBLANES, TQ))

    def consume(t, t_true, s_ref, state, edge):
        new = []
        for hh in range(N_HEADS):
            if edge:
                s3 = jnp.where(_edge_mask(j, t_true), _tile3(s_ref[hh]), NEG)
            else:
                s3 = _tile3(s_ref[hh]) + bias_ref[hh, t][None]
            new.append(_softmax_tile(s3, vbt_ref[0, t, hh * VT_ROWS:(hh + 1) * VT_ROWS, :], state[hh], acc_ref, hh))
        return tuple(new)

    qk = functools.partial(_qk_tile, kb_ref, qb_ref)
    _sweep_tiles(j, (s0_ref, s1_ref), qk, consume, _init_state(acc_ref))
    _write_heads(o_ref, acc_ref)


def _moba_call(qb, kb, vbt, kmean):
    bsz, seq, _ = qb.shape
    nb = seq // MOBA_BLOCK
    kern = functools.partial(_moba_kernel, n_sel=min(MOBA_TOPK, nb - 1))
    return pl.pallas_call(
        kern,
        grid=(bsz, seq // TQ),
        in_specs=[
            pl.BlockSpec((1, TQ, BR_W), lambda b, j: (b, j, 0)),
            pl.BlockSpec((1, seq, BR_W), lambda b, j: (b, 0, 0)),
            pl.BlockSpec((1, seq // TQ, N_HEADS * VT_ROWS, TQ), lambda b, j: (b, 0, 0, 0)),
            pl.BlockSpec((1, nb, BR_W), lambda b, j: (b, 0, 0)),
        ],
        out_specs=pl.BlockSpec((1, TQ, BR_W), lambda b, j: (b, j, 0)),
        out_shape=jax.ShapeDtypeStruct((bsz, seq, BR_W), BF16),
        scratch_shapes=[
            pltpu.VMEM((N_HEADS, nb, SUBLANES, TQ), F32),
            pltpu.VMEM((N_HEADS, TQ, TQ), F32),
            pltpu.VMEM((N_HEADS, TQ, TQ), F32),
            pltpu.VMEM((N_HEADS * VT_ROWS, TQ), F32),
        ],
        compiler_params=pltpu.CompilerParams(
            dimension_semantics=("arbitrary", "arbitrary"), vmem_limit_bytes=VMEM_LIMIT),
        name="moba",
    )(qb, kb, vbt, kmean)


def _out_ffn_kernel(x_ref, oa_ref, ob_ref, ga_ref, gb_ref, mod_ref, gffn_ref,
                    wba_ref, wbb_ref, wo_ref, wgu_ref, wd_ref, o_ref, act_ref):
    merged = (ga_ref[0].astype(F32) * _dot(oa_ref[0], wba_ref[...])
              + gb_ref[0].astype(F32) * _dot(ob_ref[0], wbb_ref[...]))
    x1 = x_ref[0] + mod_ref[0, 2:3, :] * _dot(merged.astype(BF16), wo_ref[...])
    y = x1 * lax.rsqrt(jnp.mean(x1 * x1, axis=-1, keepdims=True) + EPS) * gffn_ref[...]
    h = (y * (1.0 + mod_ref[0, 4:5, :]) + mod_ref[0, 3:4, :]).astype(BF16)
    for c0 in range(0, D_FF, FF_CHUNK):
        g = _dot(h, wgu_ref[:, c0:c0 + FF_CHUNK])
        u = _dot(h, wgu_ref[:, D_FF + c0:D_FF + c0 + FF_CHUNK])
        act_ref[:, c0:c0 + FF_CHUNK] = (g * jax.nn.sigmoid(g) * u).astype(BF16)
    o_ref[0] = x1 + mod_ref[0, 5:6, :] * _dot(act_ref[...], wd_ref[...])


def _out_ffn_call(x, oa, ob, ga, gb, mod3, gffn, wba, wbb, wo, wgu, wd):
    bsz, seq, d = x.shape
    tm = TM_FFN
    const = lambda b, i: (0, 0)
    tok = lambda b, i: (b, i, 0)
    resident = lambda a: pl.BlockSpec(a.shape, const, pipeline_mode=pl.Buffered(1))
    return pl.pallas_call(
        _out_ffn_kernel,
        grid=(bsz, seq // tm),
        in_specs=[
            pl.BlockSpec((1, tm, d), tok),
            pl.BlockSpec((1, tm, BR_W), tok),
            pl.BlockSpec((1, tm, BR_W), tok),
            pl.BlockSpec((1, tm, d), tok),
            pl.BlockSpec((1, tm, d), tok),
            pl.BlockSpec((1, N_MOD, d), lambda b, i: (b, 0, 0)),
            pl.BlockSpec((1, d), const),
            resident(wba), resident(wbb), resident(wo), resident(wgu), resident(wd),
        ],
        out_specs=pl.BlockSpec((1, tm, d), tok),
        out_shape=jax.ShapeDtypeStruct((bsz, seq, d), F32),
        scratch_shapes=[pltpu.VMEM((tm, D_FF), BF16)],
        compiler_params=pltpu.CompilerParams(
            dimension_semantics=("arbitrary", "arbitrary"), vmem_limit_bytes=VMEM_LIMIT),
        name="out_ffn",
    )(x, oa, ob, ga, gb, mod3, gffn, wba, wbb, wo, wgu, wd)


def _rope_tables(seq, dim):
    inv = ROPE_THETA ** (-jnp.arange(0, dim, 2, dtype=F32) / dim)
    ang = jnp.arange(seq, dtype=F32)[:, None] * inv[None, :]
    cos, sin = jnp.cos(ang), jnp.sin(ang)
    reps = LANES // dim
    return (jnp.tile(jnp.concatenate([cos, cos], axis=-1), (1, reps)),
            jnp.tile(jnp.concatenate([-sin, sin], axis=-1), (1, reps)))


def _layer(x, mod, g_mix, g_ffn, w_in, g_qa, g_ka, g_qb, g_kb, w_br_a, w_br_b, w_out, w_gu, w_down):
    bsz, seq, d = x.shape
    mod3 = mod.reshape(bsz, N_MOD, d)

    o = 0
    cols = {}
    for name, width in (("qa", BR_W), ("ka", BR_W), ("va", BR_W), ("qi", IDX_HEADS * IDX_DIM),
                        ("ki", IDX_DIM), ("wi", IDX_HEADS), ("qb", BR_W), ("kb", BR_W), ("vb", BR_W),
                        ("ga", D_MODEL), ("gb", D_MODEL)):
        cols[name] = w_in[:, o:o + width]
        o += width
    wqk = jnp.concatenate([cols["qa"], cols["ka"], cols["qb"], cols["kb"]], axis=1).astype(BF16)
    wvt = jnp.concatenate([cols["va"], cols["vb"]], axis=1).T.astype(BF16)
    wg = jnp.concatenate([cols["ga"], cols["gb"]], axis=1).astype(BF16)
    pad = IDX_OUT_W - IDX_HEADS * IDX_DIM - IDX_DIM - IDX_HEADS
    wi = jnp.concatenate([cols["qi"], cols["ki"], cols["wi"], jnp.zeros((d, pad), F32)], axis=1).astype(BF16)
    gains = jnp.stack([g_qa, g_ka, g_qb, g_kb])
    cosh, sinh = _rope_tables(seq, HEAD_DIM)
    cosi, sini = _rope_tables(seq, IDX_DIM)

    qa, ka, qb, kb, vat, vbt, ga, gb, a_mat, kc, kw, kmean = _proj_call(
        x, mod3, g_mix.reshape(1, d), wqk, wvt, wg, wi, gains, cosh, sinh, cosi, sini)

    oa = _dsa_call(a_mat, kc, kw, qa, ka, vat)
    ob = _moba_call(qb, kb, vbt, kmean.reshape(bsz, seq // MOBA_BLOCK, BR_W))

    return _out_ffn_call(x, oa, ob, ga, gb, mod3, g_ffn.reshape(1, d),
                         w_br_a.astype(BF16), w_br_b.astype(BF16), w_out.astype(BF16),
                         w_gu.astype(BF16), w_down.astype(BF16))


def kernel(x, c, w_mod, b_mod, g_mix_norm, g_ffn_norm, w_in, g_q_dsa, g_k_dsa, g_q_moba, g_k_moba,
           w_br_dsa, w_br_moba, w_out, w_gate_up, w_down):
    assert x.shape[1] % TM_PROJ == 0 and x.shape[2] == D_MODEL and w_gate_up.shape[2] == 2 * D_FF
    for l in range(w_mod.shape[0]):
        mod = _mod_call(c, w_mod[l], b_mod[l])
        x = _layer(x, mod, g_mix_norm[l], g_ffn_norm[l], w_in[l], g_q_dsa[l], g_k_dsa[l], g_q_moba[l],
                   g_k_moba[l], w_br_dsa[l], w_br_moba[l], w_out[l], w_gate_up[l], w_down[l])
    return x
```

```python
import functools

import jax
import jax.numpy as jnp
from jax import lax
from jax.experimental import pallas as pl
from jax.experimental.pallas import tpu as pltpu

F32 = jnp.float32
BF16 = jnp.bfloat16
I32 = jnp.int32
I16 = jnp.int16

D_MODEL = 1024
HEAD_DIM = 128
N_HEADS = 4
BR_W = N_HEADS * HEAD_DIM
IDX_HEADS = 8
IDX_DIM = 64
DSA_TOPK_MAX = 256
MOBA_BLOCK = 256
MOBA_TOPK = 3
D_FF = 2816
ROPE_THETA = 10000.0
EPS = 1e-6
NEG = -1e30
LOG2_E = 1.4426950408889634
M_INIT = -1e29
N_MOD = 6
IDX_OUT_W = 640
IDX_K = 256
I16_MIN = -(2 ** 15)
BF16_KEY_NEG_INF = (0xFF80 ^ 0x7FFF) - 0x10000
BF16_KEY_POS_INF = 0x7F80
BISECT_STEPS = 17

LANES = 128
SUBLANES = 8
PACK_ROWS = 16
VT_ROWS = HEAD_DIM + PACK_ROWS
TQ = 256
TM_PROJ = 512
PROJ_ROWS = 256
TM_FFN = 512
FF_CHUNK = 256
SEL_ROWS = 32
KEEP = 3
KEEP_ALL = 8
SCORE_GROUPS = (4, 2, 1)
VMEM_LIMIT = 56 * 1024 * 1024

_NT = (((1,), (1,)), ((), ()))
_HEAD_SLICES = tuple(slice(h * HEAD_DIM, (h + 1) * HEAD_DIM) for h in range(N_HEADS))


def _split_bf16(x):
    hi = x.astype(BF16)
    lo = (x - hi.astype(F32)).astype(BF16)
    return hi, lo


def _dot(a, b):
    return jnp.dot(a, b, preferred_element_type=F32)


def _dot_nt(a, b):
    return lax.dot_general(a, b, _NT, preferred_element_type=F32)


def _mod_kernel(c_ref, w_ref, b_ref, o_ref):
    c = c_ref[...]
    a_hi, a_lo = _split_bf16(c * jax.nn.sigmoid(c))
    w_hi, w_lo = _split_bf16(w_ref[...])
    o_ref[...] = _dot(a_hi, w_hi) + _dot(a_lo, w_hi) + _dot(a_hi, w_lo) + b_ref[...]


def _mod_call(c, w_mod, b_mod):
    bsz, d = c.shape
    n = w_mod.shape[1]
    tn = 2048
    return pl.pallas_call(
        _mod_kernel,
        grid=(n // tn,),
        in_specs=[
            pl.BlockSpec((bsz, d), lambda i: (0, 0)),
            pl.BlockSpec((d, tn), lambda i: (0, i)),
            pl.BlockSpec((1, tn), lambda i: (0, i)),
        ],
        out_specs=pl.BlockSpec((bsz, tn), lambda i: (0, i)),
        out_shape=jax.ShapeDtypeStruct((bsz, n), F32),
        compiler_params=pltpu.CompilerParams(vmem_limit_bytes=VMEM_LIMIT),
        name="mod",
    )(c, w_mod, b_mod.reshape(1, n))


def _rope_partner_64(x):
    lane = lax.broadcasted_iota(I32, x.shape, 1)
    return jnp.where((lane % IDX_DIM) < IDX_DIM // 2,
                     pltpu.roll(x, LANES - IDX_DIM // 2, 1), pltpu.roll(x, IDX_DIM // 2, 1))


def _proj_kernel(x_ref, mod_ref, gmix_ref, wqk_ref, wvt_ref, wg_ref, wi_ref, gains_ref,
                 cosh_ref, sinh_ref, cosi_ref, sini_ref,
                 qa_ref, ka_ref, qb_ref, kb_ref, vat_ref, vbt_ref, ga_ref, gb_ref, a_ref, kc_ref, kw_ref, kmean_ref):
    for blk in range(x_ref.shape[1] // PROJ_ROWS):
        _proj_rows(blk, x_ref, mod_ref, gmix_ref, wqk_ref, wvt_ref, wg_ref, wi_ref, gains_ref,
                   cosh_ref, sinh_ref, cosi_ref, sini_ref,
                   qa_ref, ka_ref, qb_ref, kb_ref, vat_ref, vbt_ref, ga_ref, gb_ref, a_ref, kc_ref, kw_ref, kmean_ref)


def _proj_rows(blk, x_ref, mod_ref, gmix_ref, wqk_ref, wvt_ref, wg_ref, wi_ref, gains_ref,
               cosh_ref, sinh_ref, cosi_ref, sini_ref,
               qa_ref, ka_ref, qb_ref, kb_ref, vat_ref, vbt_ref, ga_ref, gb_ref, a_ref, kc_ref, kw_ref, kmean_ref):
    rows = slice(blk * PROJ_ROWS, (blk + 1) * PROJ_ROWS)
    x = x_ref[0, rows, :]
    y = x * lax.rsqrt(jnp.mean(x * x, axis=-1, keepdims=True) + EPS) * gmix_ref[...]
    h = y * (1.0 + mod_ref[0, 1:2, :]) + mod_ref[0, 0:1, :]
    h_hi = h.astype(BF16)
    cosh, sinh = cosh_ref[rows, :], sinh_ref[rows, :]
    scale = HEAD_DIM ** -0.5 * LOG2_E

    for gi, out_ref in enumerate((qa_ref, ka_ref, qb_ref, kb_ref)):
        p = _dot(h_hi, wqk_ref[:, gi * BR_W:(gi + 1) * BR_W])
        for hh in range(N_HEADS):
            ph = p[:, hh * HEAD_DIM:(hh + 1) * HEAD_DIM]
            yh = ph * lax.rsqrt(jnp.mean(ph * ph, axis=-1, keepdims=True) + EPS) * gains_ref[gi:gi + 1, :]
            r = yh * cosh + pltpu.roll(yh, HEAD_DIM // 2, 1) * sinh
            if gi == 3:
                kmean_ref[0, blk, :, hh * HEAD_DIM:(hh + 1) * HEAD_DIM] = jnp.mean(r, axis=0, keepdims=True)
            if gi % 2 == 0:
                r = r * scale
            out_ref[0, rows, hh * HEAD_DIM:(hh + 1) * HEAD_DIM] = r.astype(BF16)

    vt = _dot_nt(wvt_ref[...], h_hi)
    ones = jnp.ones((VT_ROWS - HEAD_DIM, TQ), BF16)
    for gi, out_ref in enumerate((vat_ref, vbt_ref)):
        for hh in range(N_HEADS):
            src = gi * BR_W + hh * HEAD_DIM
            out_ref[0, blk, hh * VT_ROWS:hh * VT_ROWS + HEAD_DIM, :] = vt[src:src + HEAD_DIM, :].astype(BF16)
            out_ref[0, blk, hh * VT_ROWS + HEAD_DIM:(hh + 1) * VT_ROWS, :] = ones

    for gi, out_ref in enumerate((ga_ref, gb_ref)):
        g = _dot(h_hi, wg_ref[:, gi * D_MODEL:(gi + 1) * D_MODEL])
        out_ref[0, rows, :] = jax.nn.sigmoid(g).astype(BF16)

    pi = _dot(h_hi, wi_ref[...])
    cosi, sini = cosi_ref[rows, :], sini_ref[rows, :]
    first = lax.broadcasted_iota(I32, (PROJ_ROWS, LANES), 1) < IDX_DIM
    half = LANES // 2

    def hi_lo(v):
        hi = v.astype(BF16).astype(F32)
        return hi, v - hi

    n_q = IDX_HEADS * IDX_DIM // LANES
    for g4 in range(n_q):
        xg = pi[:, g4 * LANES:(g4 + 1) * LANES]
        hi, lo = hi_lo(xg * cosi + _rope_partner_64(xg) * sini)
        hi_r, lo_r = pltpu.roll(hi, half, 1), pltpu.roll(lo, half, 1)
        for head, parts in ((2 * g4, (jnp.where(first, hi, lo_r), jnp.where(first, hi, 0.0))),
                            (2 * g4 + 1, (jnp.where(first, hi_r, lo), jnp.where(first, hi_r, 0.0)))):
            for pi_, part in enumerate(parts):
                a_ref[0, rows, head * IDX_K + pi_ * LANES:head * IDX_K + (pi_ + 1) * LANES] = part.astype(BF16)
    xg = pi[:, n_q * LANES:(n_q + 1) * LANES]
    roped = xg * cosi + _rope_partner_64(xg) * sini
    hi, lo = hi_lo(roped)
    kc_ref[0, rows, :LANES] = jnp.where(first, hi, pltpu.roll(hi, half, 1)).astype(BF16)
    kc_ref[0, rows, LANES:] = jnp.where(first, lo, 0.0).astype(BF16)
    w_scale = (IDX_HEADS ** -0.5) * (IDX_DIM ** -0.5)
    kw_ref[0, rows, :] = jnp.where(first, roped, xg * w_scale)


def _proj_call(x, mod3, gmix, wqk, wvt, wg, wi, gains, cosh, sinh, cosi, sini):
    bsz, seq, d = x.shape
    tm = TM_PROJ
    nb = seq // MOBA_BLOCK
    const = lambda b, i: (0, 0)
    tok = lambda b, i: (b, i, 0)
    tab = lambda b, i: (i, 0)
    resident = lambda a: pl.BlockSpec(a.shape, const, pipeline_mode=pl.Buffered(1))
    out_shape = (
        [jax.ShapeDtypeStruct((bsz, seq, BR_W), BF16)] * 4
        + [jax.ShapeDtypeStruct((bsz, seq // TQ, N_HEADS * VT_ROWS, TQ), BF16)] * 2
        + [jax.ShapeDtypeStruct((bsz, seq, D_MODEL), BF16)] * 2
        + [jax.ShapeDtypeStruct((bsz, seq, IDX_HEADS * IDX_K), BF16),
           jax.ShapeDtypeStruct((bsz, seq, IDX_K), BF16),
           jax.ShapeDtypeStruct((bsz, seq, LANES), F32),
           jax.ShapeDtypeStruct((bsz, nb, 1, BR_W), F32)]
    )
    out_specs = (
        [pl.BlockSpec((1, tm, BR_W), tok)] * 4
        + [pl.BlockSpec((1, tm // TQ, N_HEADS * VT_ROWS, TQ), lambda b, i: (b, i, 0, 0))] * 2
        + [pl.BlockSpec((1, tm, D_MODEL), tok)] * 2
        + [pl.BlockSpec((1, tm, IDX_HEADS * IDX_K), tok),
           pl.BlockSpec((1, tm, IDX_K), tok),
           pl.BlockSpec((1, tm, LANES), tok),
           pl.BlockSpec((1, tm // MOBA_BLOCK, 1, BR_W), lambda b, i: (b, i, 0, 0))]
    )
    return pl.pallas_call(
        _proj_kernel,
        grid=(bsz, seq // tm),
        in_specs=[
            pl.BlockSpec((1, tm, d), tok),
            pl.BlockSpec((1, N_MOD, d), lambda b, i: (b, 0, 0)),
            pl.BlockSpec((1, d), const),
            resident(wqk), resident(wvt), resident(wg), resident(wi),
            pl.BlockSpec(gains.shape, const),
            pl.BlockSpec((tm, LANES), tab),
            pl.BlockSpec((tm, LANES), tab),
            pl.BlockSpec((tm, LANES), tab),
            pl.BlockSpec((tm, LANES), tab),
        ],
        out_specs=out_specs,
        out_shape=out_shape,
        compiler_params=pltpu.CompilerParams(
            dimension_semantics=("arbitrary", "arbitrary"), vmem_limit_bytes=VMEM_LIMIT),
        name="proj",
    )(x, mod3, gmix, wqk, wvt, wg, wi, gains, cosh, sinh, cosi, sini)


def _tile3(x):
    return x.reshape(x.shape[0] // SUBLANES, SUBLANES, x.shape[1])


def _allreduce_sublanes(x, op):
    for shift in (4, 2, 1):
        x = op(x, pltpu.roll(x, shift, 0))
    return x


def _softmax_tile(s3, vt, m_old, acc_ref, hh):
    rows = slice(hh * VT_ROWS, (hh + 1) * VT_ROWS)
    m_new = jnp.maximum(m_old, _allreduce_sublanes(jnp.max(s3, axis=0), jnp.maximum))
    alpha = jnp.exp2(m_old - m_new)
    p = jnp.exp2(s3 - m_new[None]).reshape(s3.shape[0] * SUBLANES, TQ).astype(BF16)
    acc_ref[rows, :] = (_tile3(acc_ref[rows, :]) * alpha[None]).reshape(VT_ROWS, TQ) + _dot(vt, p)
    return m_new


def _qk_tile(k_ref, q_ref, t, s_ref):
    k0 = pl.multiple_of(t * TQ, TQ)
    for hh, hs in enumerate(_HEAD_SLICES):
        s_ref[hh] = _dot_nt(k_ref[0, pl.ds(k0, TQ), hs], q_ref[0, :, hs])


def _edge_mask(j, t_true):
    shape = (TQ // SUBLANES, SUBLANES, TQ)
    row = lax.broadcasted_iota(I32, shape, 0) * SUBLANES + lax.broadcasted_iota(I32, shape, 1)
    return row - lax.broadcasted_iota(I32, shape, 2) <= (j - t_true) * TQ


def _sweep_tiles(j, s_refs, qk, consume, state):
    s0, s1 = s_refs
    last_pair = j // 2
    qk(0, s0)

    def body(p, st):
        t0 = 2 * p
        qk(t0 + 1, s1)
        st = consume(t0, t0, s0, st, False)
        qk(t0 + 2, s0)
        return consume(t0 + 1, t0 + 1, s1, st, False)

    state = lax.fori_loop(0, last_pair // 2, lambda q, st: body(2 * q + 1, body(2 * q, st)), state)
    state = lax.fori_loop(2 * (last_pair // 2), last_pair, body, state)
    ta = 2 * last_pair

    @pl.when(ta == j)
    def _():
        consume(j, j, s0, state, True)

    @pl.when(ta != j)
    def _():
        qk(j, s1)
        consume(j, j, s1, consume(ta, ta, s0, state, False), True)


def _init_state(acc_ref):
    acc_ref[...] = jnp.zeros(acc_ref.shape, F32)
    return (jnp.full((SUBLANES, TQ), M_INIT, F32),) * N_HEADS


def _write_heads(o_ref, acc_ref):
    for hh, hs in enumerate(_HEAD_SLICES):
        base = hh * VT_ROWS
        inv = 1.0 / acc_ref[base + HEAD_DIM:base + HEAD_DIM + SUBLANES, :]
        o_t = (_tile3(acc_ref[base:base + HEAD_DIM, :]) * inv[None]).reshape(HEAD_DIM, TQ)
        o_ref[0, :, hs] = o_t.T.astype(BF16)


def _f32_from_key(key):
    bits = jnp.where(key < 0, key ^ jnp.int32(0x7FFFFFFF), key)
    return lax.bitcast_convert_type(bits, F32)


def _key32_of_key16(key16):
    return (key16 << 16) + jnp.where(key16 < 0, 0xFFFF, 0)


def _bf16_from_key16(key16):
    return _f32_from_key(_key32_of_key16(key16))


def _dsa_kernel(a_ref, kc_ref, wk_ref, qa_ref, ka_ref, vat_ref, o_ref,
                scores_ref, hb_ref, red_ref, fold_ref, thr_ref, cge_ref, s0_ref, s1_ref, acc_ref, *, topk, seq_bits):
    j = pl.program_id(1)
    n_tiles = j + 1
    w_rows = wk_ref[0].T[IDX_DIM:IDX_DIM + IDX_HEADS, :]
    w8 = [jnp.broadcast_to(w_rows[hh:hh + 1, :], (SUBLANES, TQ)) for hh in range(IDX_HEADS)]

    def score_tile(c, maybe_diagonal):
        k0 = pl.multiple_of(c * TQ, TQ)
        kc = kc_ref[0, pl.ds(k0, TQ), :]
        acc = jnp.zeros((TQ // SUBLANES, SUBLANES, TQ), F32)
        for hh in range(IDX_HEADS):
            lg = _dot_nt(kc, a_ref[0, :, hh * IDX_K:(hh + 1) * IDX_K])
            acc = acc + w8[hh][None] * jnp.maximum(_tile3(lg), 0.0)
        if maybe_diagonal:
            acc = jnp.where(_edge_mask(j, c), acc, NEG)
        acc = acc.reshape(TQ, TQ)
        scores_ref[pl.ds(k0, TQ), :] = acc
        hb_ref[pl.ds(k0, TQ), :] = acc.astype(BF16)

    def score_run(start, n_trips, width):
        def trip(g, carry):
            for u in range(width):
                score_tile(start + width * g + u, u == width - 1)
            return carry

        lax.fori_loop(0, n_trips, trip, 0)
        return start + width * n_trips

    done = 0
    for width in SCORE_GROUPS:
        done = score_run(done, (n_tiles - done) // width, width)

    def load_packed(c, u):
        r0 = pl.multiple_of(c * TQ + u * SEL_ROWS, SEL_ROWS)
        return hb_ref[pl.ds(r0, SEL_ROWS), :].reshape(SEL_ROWS // PACK_ROWS, PACK_ROWS, TQ)

    def load_scores(c, u):
        r0 = pl.multiple_of(c * TQ + u * SEL_ROWS, SEL_ROWS)
        return _tile3(scores_ref[pl.ds(r0, SEL_ROWS), :]), r0

    def sweep_count(tile, zero):
        cnt = lax.fori_loop(0, n_tiles // 2, lambda p, cnt: tile(2 * p + 1, tile(2 * p, cnt)), zero)
        return lax.fori_loop(2 * (n_tiles // 2), n_tiles, tile, cnt)

    one16, zero16 = jnp.int16(1), jnp.int16(0)

    def count_bf16(test):
        t = jnp.broadcast_to(test, (PACK_ROWS, TQ)).astype(BF16)[None]

        def tile(c, cnt):
            for u in range(TQ // SEL_ROWS):
                cnt = cnt + jnp.where(load_packed(c, u) >= t, one16, zero16)
            return cnt

        cnt = sweep_count(tile, jnp.zeros((SEL_ROWS // PACK_ROWS, PACK_ROWS, TQ), I16))
        return jnp.sum(jnp.sum(cnt.astype(F32), axis=0), axis=0, keepdims=True)

    def count_f32(pred):
        def tile(c, cnt):
            for u in range(TQ // SEL_ROWS):
                sc, r0 = load_scores(c, u)
                cnt = cnt + jnp.where(pred(sc, r0), 1.0, 0.0)
            return cnt

        cnt = sweep_count(tile, jnp.zeros((SEL_ROWS // SUBLANES, SUBLANES, TQ), F32))
        return jnp.sum(jnp.sum(cnt, axis=0), axis=0, keepdims=True)

    kf = float(topk)

    c0 = count_bf16(jnp.zeros((1, TQ), F32))
    start = jnp.where(c0 >= kf, 0, I16_MIN).astype(I32)

    def bit_step(i, val):
        test = val | (jnp.int32(1) << (14 - i))
        return jnp.where(count_bf16(_bf16_from_key16(test)) >= kf, test, val)

    hb_k = lax.fori_loop(0, 15, bit_step, start)

    below = _key32_of_key16(jnp.maximum(hb_k - 1, BF16_KEY_NEG_INF))
    lo = below + ((_key32_of_key16(hb_k) - below) >> 1) - 1
    hi = _key32_of_key16(jnp.minimum(hb_k + 1, BF16_KEY_POS_INF))

    def bisect(count_ge):
        def step(i, carry):
            lo, hi, cge = carry
            mid = lo + ((hi - lo) >> 1)
            c = count_ge(_f32_from_key(mid)[None])
            ok = c >= kf
            return jnp.where(ok, mid, lo), jnp.where(ok, hi, mid), jnp.where(ok, c, cge)

        lo_k, _, cge = lax.fori_loop(0, BISECT_STEPS, step, (lo, hi, jnp.full((1, TQ), kf, F32)))
        thr_ref[...] = jnp.broadcast_to(_f32_from_key(lo_k), (SUBLANES, TQ))
        cge_ref[...] = jnp.broadcast_to(cge, (SUBLANES, TQ))

    lo_f, hi_f = _f32_from_key(lo), _f32_from_key(hi)
    ninf = jnp.full((SUBLANES, TQ), -jnp.inf, F32)

    def reduce_tile(c, carry):
        above, spill = carry
        k0 = pl.multiple_of(c * TQ, TQ)
        x3 = _tile3(scores_ref[pl.ds(k0, TQ), :])
        top = [ninf] * KEEP
        for g in range(TQ // SUBLANES):
            x = x3[g]
            ge_hi = x >= hi_f
            above = above + jnp.where(ge_hi, 1.0, 0.0)
            t = jnp.where((x >= lo_f) & ~ge_hi, x, -jnp.inf)
            for r in range(KEEP):
                top[r], t = jnp.maximum(top[r], t), jnp.minimum(top[r], t)
            spill = jnp.maximum(spill, t)
        for r in range(KEEP):
            red_ref[c, r] = top[r]
        return above, spill

    above, spill = lax.fori_loop(0, n_tiles, reduce_tile, (jnp.zeros((SUBLANES, TQ), F32), ninf))
    above = jnp.sum(above, axis=0, keepdims=True)
    reduced_ok = jnp.max(spill) == -jnp.inf

    @pl.when(reduced_ok)
    def _():
        def fold_tile(c, carry):
            top, spill2 = carry
            for r in range(KEEP):
                t = red_ref[c, r]
                nxt = []
                for q in range(KEEP_ALL):
                    nxt.append(jnp.maximum(top[q], t))
                    t = jnp.minimum(top[q], t)
                top = tuple(nxt)
                spill2 = jnp.maximum(spill2, t)
            return top, spill2

        top, spill2 = lax.fori_loop(0, n_tiles, fold_tile, ((ninf,) * KEEP_ALL, ninf))
        for q in range(KEEP_ALL):
            fold_ref[q] = top[q]
        folded_ok = jnp.max(spill2) == -jnp.inf

        @pl.when(folded_ok)
        def _():
            def count_ge(t3):
                cnt = jnp.zeros((SUBLANES, TQ), F32)
                for q in range(KEEP_ALL):
                    cnt = cnt + jnp.where(fold_ref[q] >= t3[0], 1.0, 0.0)
                return above + jnp.sum(cnt, axis=0, keepdims=True)
            bisect(count_ge)

        @pl.when(jnp.logical_not(folded_ok))
        def _():
            def count_ge(t3):
                def tile(c, cnt):
                    for r in range(KEEP):
                        cnt = cnt + jnp.where(red_ref[c, r] >= t3[0], 1.0, 0.0)
                    return cnt
                cnt = sweep_count(tile, jnp.zeros((SUBLANES, TQ), F32))
                return above + jnp.sum(cnt, axis=0, keepdims=True)
            bisect(count_ge)

    @pl.when(jnp.logical_not(reduced_ok))
    def _():
        bisect(lambda t3: count_f32(lambda sc, r0: sc >= t3))

    thr = thr_ref[0:1, :]
    cge = cge_ref[0:1, :]

    @pl.when(jnp.max(jnp.abs(cge - kf)) > 0.0)
    def _():
        thr3 = thr[None]
        need = kf - count_f32(lambda sc, r0: sc > thr3)
        sub = (lax.broadcasted_iota(I32, (SEL_ROWS // SUBLANES, SUBLANES, TQ), 0) * SUBLANES
               + lax.broadcasted_iota(I32, (SEL_ROWS // SUBLANES, SUBLANES, TQ), 1))

        def cut_step(i, cut):
            test = cut | (jnp.int32(1) << (seq_bits - 1 - i))
            f = count_f32(lambda sc, r0: (sc == thr3) & (r0 + sub < test[None]))
            return jnp.where(f < need, test, cut)

        cut = lax.fori_loop(0, seq_bits, cut_step, jnp.zeros((1, TQ), I32))

        def demote(c, carry):
            r0 = pl.multiple_of(c * SEL_ROWS, SEL_ROWS)
            sc = _tile3(scores_ref[pl.ds(r0, SEL_ROWS), :])
            scores_ref[pl.ds(r0, SEL_ROWS), :] = jnp.where(
                (sc == thr3) & (r0 + sub > cut[None]), NEG, sc).reshape(SEL_ROWS, TQ)
            return carry

        lax.fori_loop(0, n_tiles * (TQ // SEL_ROWS), demote, 0)

    def consume(t, t_true, s_ref, state, edge):
        k0 = pl.multiple_of(t * TQ, TQ)
        sel = _tile3(scores_ref[pl.ds(k0, TQ), :]) >= thr[None]
        if edge:
            sel = sel & _edge_mask(j, t_true)
        bias = jnp.where(sel, 0.0, NEG)
        return tuple(
            _softmax_tile(_tile3(s_ref[hh]) + bias, vat_ref[0, t, hh * VT_ROWS:(hh + 1) * VT_ROWS, :],
                          state[hh], acc_ref, hh)
            for hh in range(N_HEADS))

    qk = functools.partial(_qk_tile, ka_ref, qa_ref)
    _sweep_tiles(j, (s0_ref, s1_ref), qk, consume, _init_state(acc_ref))
    _write_heads(o_ref, acc_ref)


def _dsa_call(a_mat, kc, idx_out, qa, ka, vat):
    bsz, seq, _ = qa.shape
    topk = min(DSA_TOPK_MAX, seq // 4)
    seq_bits = (seq - 1).bit_length()
    kern = functools.partial(_dsa_kernel, topk=topk, seq_bits=seq_bits)
    return pl.pallas_call(
        kern,
        grid=(bsz, seq // TQ),
        in_specs=[
            pl.BlockSpec((1, TQ, IDX_HEADS * IDX_K), lambda b, j: (b, j, 0)),
            pl.BlockSpec((1, seq, IDX_K), lambda b, j: (b, 0, 0)),
            pl.BlockSpec((1, TQ, LANES), lambda b, j: (b, j, 0)),
            pl.BlockSpec((1, TQ, BR_W), lambda b, j: (b, j, 0)),
            pl.BlockSpec((1, seq, BR_W), lambda b, j: (b, 0, 0)),
            pl.BlockSpec((1, seq // TQ, N_HEADS * VT_ROWS, TQ), lambda b, j: (b, 0, 0, 0)),
        ],
        out_specs=pl.BlockSpec((1, TQ, BR_W), lambda b, j: (b, j, 0)),
        out_shape=jax.ShapeDtypeStruct((bsz, seq, BR_W), BF16),
        scratch_shapes=[
            pltpu.VMEM((seq, TQ), F32),
            pltpu.VMEM((seq, TQ), BF16),
            pltpu.VMEM((seq // TQ, KEEP, SUBLANES, TQ), F32),
            pltpu.VMEM((KEEP_ALL, SUBLANES, TQ), F32),
            pltpu.VMEM((SUBLANES, TQ), F32),
            pltpu.VMEM((SUBLANES, TQ), F32),
            pltpu.VMEM((N_HEADS, TQ, TQ), F32),
            pltpu.VMEM((N_HEADS, TQ, TQ), F32),
            pltpu.VMEM((N_HEADS * VT_ROWS, TQ), F32),
        ],
        compiler_params=pltpu.CompilerParams(
            dimension_semantics=("arbitrary", "arbitrary"), vmem_limit_bytes=VMEM_LIMIT),
        name="dsa",
    )(a_mat, kc, idx_out, qa, ka, vat)


def _moba_kernel(qb_ref, kb_ref, vbt_ref, kmean_ref, o_ref, bias_ref, s0_ref, s1_ref, acc_ref, *, n_sel):
    j = pl.program_id(1)
    nb = kmean_ref.shape[1]
    blk = lax.broadcasted_iota(I32, (nb, TQ), 0).astype(F32)
    jf = j.astype(F32)

    for hh in range(N_HEADS):
        hs = slice(hh * HEAD_DIM, (hh + 1) * HEAD_DIM)
        km_hi, km_lo = _split_bf16(kmean_ref[0, :, hs])
        q = qb_ref[0, :, hs]
        gate = jnp.where(blk < jf, _dot_nt(km_hi, q) + _dot_nt(km_lo, q), NEG)
        chosen = jnp.zeros((nb, TQ), F32)
        for _ in range(n_sel):
            best = jnp.max(gate, axis=0, keepdims=True)
            first = jnp.min(jnp.where(gate == best, blk, float(nb)), axis=0, keepdims=True)
            hit = blk == first
            chosen = jnp.where(hit, 1.0, chosen)
            gate = jnp.where(hit, -jnp.inf, gate)
        bias = jnp.where((chosen > 0.0) & (blk < jf), 0.0, NEG)
        for n in range(nb):
            bias_ref[hh, n] = jnp.broadcast_to(bias[n:n + 1, :], (SUBLANES, TQ))

    def consume(t, t_true, s_ref, state, edge):
        new = []
        for hh in range(N_HEADS):
            if edge:
                s3 = jnp.where(_edge_mask(j, t_true), _tile3(s_ref[hh]), NEG)
            else:
                s3 = _tile3(s_ref[hh]) + bias_ref[hh, t][None]
            new.append(_softmax_tile(s3, vbt_ref[0, t, hh * VT_ROWS:(hh + 1) * VT_ROWS, :], state[hh], acc_ref, hh))
        return tuple(new)

    qk = functools.partial(_qk_tile, kb_ref, qb_ref)
    _sweep_tiles(j, (s0_ref, s1_ref), qk, consume, _init_state(acc_ref))
    _write_heads(o_ref, acc_ref)


def _moba_call(qb, kb, vbt, kmean):
    bsz, seq, _ = qb.shape
    nb = seq // MOBA_BLOCK
    kern = functools.partial(_moba_kernel, n_sel=min(MOBA_TOPK, nb - 1))
    return pl.pallas_call(
        kern,
        grid=(bsz, seq // TQ),
        in_specs=[
            pl.BlockSpec((1, TQ, BR_W), lambda b, j: (b, j, 0)),
            pl.BlockSpec((1, seq, BR_W), lambda b, j: (b, 0, 0)),
            pl.BlockSpec((1, seq // TQ, N_HEADS * VT_ROWS, TQ), lambda b, j: (b, 0, 0, 0)),
            pl.BlockSpec((1, nb, BR_W), lambda b, j: (b, 0, 0)),
        ],
        out_specs=pl.BlockSpec((1, TQ, BR_W), lambda b, j: (b, j, 0)),
        out_shape=jax.ShapeDtypeStruct((bsz, seq, BR_W), BF16),
        scratch_shapes=[
            pltpu.VMEM((N_HEADS, nb, SUBLANES, TQ), F32),
            pltpu.VMEM((N_HEADS, TQ, TQ), F32),
            pltpu.VMEM((N_HEADS, TQ, TQ), F32),
            pltpu.VMEM((N_HEADS * VT_ROWS, TQ), F32),
        ],
        compiler_params=pltpu.CompilerParams(
            dimension_semantics=("arbitrary", "arbitrary"), vmem_limit_bytes=VMEM_LIMIT),
        name="moba",
    )(qb, kb, vbt, kmean)


def _out_ffn_kernel(x_ref, oa_ref, ob_ref, ga_ref, gb_ref, mod_ref, gffn_ref,
                    wba_ref, wbb_ref, wo_ref, wgu_ref, wd_ref, o_ref, act_ref):
    merged = (ga_ref[0].astype(F32) * _dot(oa_ref[0], wba_ref[...])
              + gb_ref[0].astype(F32) * _dot(ob_ref[0], wbb_ref[...]))
    x1 = x_ref[0] + mod_ref[0, 2:3, :] * _dot(merged.astype(BF16), wo_ref[...])
    y = x1 * lax.rsqrt(jnp.mean(x1 * x1, axis=-1, keepdims=True) + EPS) * gffn_ref[...]
    h = (y * (1.0 + mod_ref[0, 4:5, :]) + mod_ref[0, 3:4, :]).astype(BF16)
    for c0 in range(0, D_FF, FF_CHUNK):
        g = _dot(h, wgu_ref[:, c0:c0 + FF_CHUNK])
        u = _dot(h, wgu_ref[:, D_FF + c0:D_FF + c0 + FF_CHUNK])
        act_ref[:, c0:c0 + FF_CHUNK] = (g * jax.nn.sigmoid(g) * u).astype(BF16)
    o_ref[0] = x1 + mod_ref[0, 5:6, :] * _dot(act_ref[...], wd_ref[...])


def _out_ffn_call(x, oa, ob, ga, gb, mod3, gffn, wba, wbb, wo, wgu, wd):
    bsz, seq, d = x.shape
    tm = TM_FFN
    const = lambda b, i: (0, 0)
    tok = lambda b, i: (b, i, 0)
    resident = lambda a: pl.BlockSpec(a.shape, const, pipeline_mode=pl.Buffered(1))
    return pl.pallas_call(
        _out_ffn_kernel,
        grid=(bsz, seq // tm),
        in_specs=[
            pl.BlockSpec((1, tm, d), tok),
            pl.BlockSpec((1, tm, BR_W), tok),
            pl.BlockSpec((1, tm, BR_W), tok),
            pl.BlockSpec((1, tm, d), tok),
            pl.BlockSpec((1, tm, d), tok),
            pl.BlockSpec((1, N_MOD, d), lambda b, i: (b, 0, 0)),
            pl.BlockSpec((1, d), const),
            resident(wba), resident(wbb), resident(wo), resident(wgu), resident(wd),
        ],
        out_specs=pl.BlockSpec((1, tm, d), tok),
        out_shape=jax.ShapeDtypeStruct((bsz, seq, d), F32),
        scratch_shapes=[pltpu.VMEM((tm, D_FF), BF16)],
        compiler_params=pltpu.CompilerParams(
            dimension_semantics=("arbitrary", "arbitrary"), vmem_limit_bytes=VMEM_LIMIT),
        name="out_ffn",
    )(x, oa, ob, ga, gb, mod3, gffn, wba, wbb, wo, wgu, wd)


def _rope_tables(seq, dim):
    inv = ROPE_THETA ** (-jnp.arange(0, dim, 2, dtype=F32) / dim)
    ang = jnp.arange(seq, dtype=F32)[:, None] * inv[None, :]
    cos, sin = jnp.cos(ang), jnp.sin(ang)
    reps = LANES // dim
    return (jnp.tile(jnp.concatenate([cos, cos], axis=-1), (1, reps)),
            jnp.tile(jnp.concatenate([-sin, sin], axis=-1), (1, reps)))


def _layer(x, mod, g_mix, g_ffn, w_in, g_qa, g_ka, g_qb, g_kb, w_br_a, w_br_b, w_out, w_gu, w_down):
    bsz, seq, d = x.shape
    mod3 = mod.reshape(bsz, N_MOD, d)

    o = 0
    cols = {}
    for name, width in (("qa", BR_W), ("ka", BR_W), ("va", BR_W), ("qi", IDX_HEADS * IDX_DIM),
                        ("ki", IDX_DIM), ("wi", IDX_HEADS), ("qb", BR_W), ("kb", BR_W), ("vb", BR_W),
                        ("ga", D_MODEL), ("gb", D_MODEL)):
        cols[name] = w_in[:, o:o + width]
        o += width
    wqk = jnp.concatenate([cols["qa"], cols["ka"], cols["qb"], cols["kb"]], axis=1).astype(BF16)
    wvt = jnp.concatenate([cols["va"], cols["vb"]], axis=1).T.astype(BF16)
    wg = jnp.concatenate([cols["ga"], cols["gb"]], axis=1).astype(BF16)
    pad = IDX_OUT_W - IDX_HEADS * IDX_DIM - IDX_DIM - IDX_HEADS
    wi = jnp.concatenate([cols["qi"], cols["ki"], cols["wi"], jnp.zeros((d, pad), F32)], axis=1).astype(BF16)
    gains = jnp.stack([g_qa, g_ka, g_qb, g_kb])
    cosh, sinh = _rope_tables(seq, HEAD_DIM)
    cosi, sini = _rope_tables(seq, IDX_DIM)

    qa, ka, qb, kb, vat, vbt, ga, gb, a_mat, kc, kw, kmean = _proj_call(
        x, mod3, g_mix.reshape(1, d), wqk, wvt, wg, wi, gains, cosh, sinh, cosi, sini)

    oa = _dsa_call(a_mat, kc, kw, qa, ka, vat)
    ob = _moba_call(qb, kb, vbt, kmean.reshape(bsz, seq // MOBA_BLOCK, BR_W))

    return _out_ffn_call(x, oa, ob, ga, gb, mod3, g_ffn.reshape(1, d),
                         w_br_a.astype(BF16), w_br_b.astype(BF16), w_out.astype(BF16),
                         w_gu.astype(BF16), w_down.astype(BF16))


def kernel(x, c, w_mod, b_mod, g_mix_norm, g_ffn_norm, w_in, g_q_dsa, g_k_dsa, g_q_moba, g_k_moba,
           w_br_dsa, w_br_moba, w_out, w_gate_up, w_down):
    assert x.shape[1] % TM_PROJ == 0 and x.shape[2] == D_MODEL and w_gate_up.shape[2] == 2 * D_FF
    for l in range(w_mod.shape[0]):
        mod = _mod_call(c, w_mod[l], b_mod[l])
        x = _layer(x, mod, g_mix_norm[l], g_ffn_norm[l], w_in[l], g_q_dsa[l], g_k_dsa[l], g_q_moba[l],
                   g_k_moba[l], w_br_dsa[l], w_br_moba[l], w_out[l], w_gate_up[l], w_down[l])
    return x
```

```python
import functools

import jax
import jax.numpy as jnp
from jax import lax
from jax.experimental import pallas as pl
from jax.experimental.pallas import tpu as pltpu

F32 = jnp.float32
BF16 = jnp.bfloat16
I32 = jnp.int32
I16 = jnp.int16

D_MODEL = 1024
HEAD_DIM = 128
N_HEADS = 4
BR_W = N_HEADS * HEAD_DIM
IDX_HEADS = 8
IDX_DIM = 64
DSA_TOPK_MAX = 256
MOBA_BLOCK = 256
MOBA_TOPK = 3
D_FF = 2816
ROPE_THETA = 10000.0
EPS = 1e-6
NEG = -1e30
LOG2_E = 1.4426950408889634
M_INIT = -1e29
N_MOD = 6
IDX_OUT_W = 640
IDX_K = 256
I16_MIN = -(2 ** 15)
BF16_KEY_NEG_INF = (0xFF80 ^ 0x7FFF) - 0x10000
BF16_KEY_POS_INF = 0x7F80
BISECT_STEPS = 17

LANES = 128
SUBLANES = 8
PACK_ROWS = 16
VT_ROWS = HEAD_DIM + PACK_ROWS
TQ = 256
TM_PROJ = 512
PROJ_ROWS = 256
TM_FFN = 512
FF_CHUNK = 256
SEL_ROWS = 32
KEEP = 3
KEEP_ALL = 8
SCORE_GROUPS = (8, 4, 2, 1)
VMEM_LIMIT = 56 * 1024 * 1024

_NT = (((1,), (1,)), ((), ()))
_HEAD_SLICES = tuple(slice(h * HEAD_DIM, (h + 1) * HEAD_DIM) for h in range(N_HEADS))


def _split_bf16(x):
    hi = x.astype(BF16)
    lo = (x - hi.astype(F32)).astype(BF16)
    return hi, lo


def _dot(a, b):
    return jnp.dot(a, b, preferred_element_type=F32)


def _dot_nt(a, b):
    return lax.dot_general(a, b, _NT, preferred_element_type=F32)


def _mod_kernel(c_ref, w_ref, b_ref, o_ref):
    c = c_ref[...]
    a_hi, a_lo = _split_bf16(c * jax.nn.sigmoid(c))
    w_hi, w_lo = _split_bf16(w_ref[...])
    o_ref[...] = _dot(a_hi, w_hi) + _dot(a_lo, w_hi) + _dot(a_hi, w_lo) + b_ref[...]


def _mod_call(c, w_mod, b_mod):
    bsz, d = c.shape
    n = w_mod.shape[1]
    tn = 2048
    return pl.pallas_call(
        _mod_kernel,
        grid=(n // tn,),
        in_specs=[
            pl.BlockSpec((bsz, d), lambda i: (0, 0)),
            pl.BlockSpec((d, tn), lambda i: (0, i)),
            pl.BlockSpec((1, tn), lambda i: (0, i)),
        ],
        out_specs=pl.BlockSpec((bsz, tn), lambda i: (0, i)),
        out_shape=jax.ShapeDtypeStruct((bsz, n), F32),
        compiler_params=pltpu.CompilerParams(vmem_limit_bytes=VMEM_LIMIT),
        name="mod",
    )(c, w_mod, b_mod.reshape(1, n))


def _rope_partner_64(x):
    lane = lax.broadcasted_iota(I32, x.shape, 1)
    return jnp.where((lane % IDX_DIM) < IDX_DIM // 2,
                     pltpu.roll(x, LANES - IDX_DIM // 2, 1), pltpu.roll(x, IDX_DIM // 2, 1))


def _proj_kernel(x_ref, mod_ref, gmix_ref, wqk_ref, wvt_ref, wg_ref, wi_ref, gains_ref,
                 cosh_ref, sinh_ref, cosi_ref, sini_ref,
                 qa_ref, ka_ref, qb_ref, kb_ref, vat_ref, vbt_ref, ga_ref, gb_ref, a_ref, kc_ref, kw_ref, kmean_ref):
    for blk in range(x_ref.shape[1] // PROJ_ROWS):
        _proj_rows(blk, x_ref, mod_ref, gmix_ref, wqk_ref, wvt_ref, wg_ref, wi_ref, gains_ref,
                   cosh_ref, sinh_ref, cosi_ref, sini_ref,
                   qa_ref, ka_ref, qb_ref, kb_ref, vat_ref, vbt_ref, ga_ref, gb_ref, a_ref, kc_ref, kw_ref, kmean_ref)


def _proj_rows(blk, x_ref, mod_ref, gmix_ref, wqk_ref, wvt_ref, wg_ref, wi_ref, gains_ref,
               cosh_ref, sinh_ref, cosi_ref, sini_ref,
               qa_ref, ka_ref, qb_ref, kb_ref, vat_ref, vbt_ref, ga_ref, gb_ref, a_ref, kc_ref, kw_ref, kmean_ref):
    rows = slice(blk * PROJ_ROWS, (blk + 1) * PROJ_ROWS)
    x = x_ref[0, rows, :]
    y = x * lax.rsqrt(jnp.mean(x * x, axis=-1, keepdims=True) + EPS) * gmix_ref[...]
    h = y * (1.0 + mod_ref[0, 1:2, :]) + mod_ref[0, 0:1, :]
    h_hi = h.astype(BF16)
    cosh, sinh = cosh_ref[rows, :], sinh_ref[rows, :]
    scale = HEAD_DIM ** -0.5 * LOG2_E

    for gi, out_ref in enumerate((qa_ref, ka_ref, qb_ref, kb_ref)):
        p = _dot(h_hi, wqk_ref[:, gi * BR_W:(gi + 1) * BR_W])
        for hh in range(N_HEADS):
            ph = p[:, hh * HEAD_DIM:(hh + 1) * HEAD_DIM]
            yh = ph * lax.rsqrt(jnp.mean(ph * ph, axis=-1, keepdims=True) + EPS) * gains_ref[gi:gi + 1, :]
            r = yh * cosh + pltpu.roll(yh, HEAD_DIM // 2, 1) * sinh
            if gi == 3:
                kmean_ref[0, blk, :, hh * HEAD_DIM:(hh + 1) * HEAD_DIM] = jnp.mean(r, axis=0, keepdims=True)
            if gi % 2 == 0:
                r = r * scale
            out_ref[0, rows, hh * HEAD_DIM:(hh + 1) * HEAD_DIM] = r.astype(BF16)

    vt = _dot_nt(wvt_ref[...], h_hi)
    ones = jnp.ones((VT_ROWS - HEAD_DIM, TQ), BF16)
    for gi, out_ref in enumerate((vat_ref, vbt_ref)):
        for hh in range(N_HEADS):
            src = gi * BR_W + hh * HEAD_DIM
            out_ref[0, blk, hh * VT_ROWS:hh * VT_ROWS + HEAD_DIM, :] = vt[src:src + HEAD_DIM, :].astype(BF16)
            out_ref[0, blk, hh * VT_ROWS + HEAD_DIM:(hh + 1) * VT_ROWS, :] = ones

    for gi, out_ref in enumerate((ga_ref, gb_ref)):
        g = _dot(h_hi, wg_ref[:, gi * D_MODEL:(gi + 1) * D_MODEL])
        out_ref[0, rows, :] = jax.nn.sigmoid(g).astype(BF16)

    pi = _dot(h_hi, wi_ref[...])
    cosi, sini = cosi_ref[rows, :], sini_ref[rows, :]
    first = lax.broadcasted_iota(I32, (PROJ_ROWS, LANES), 1) < IDX_DIM
    half = LANES // 2

    def hi_lo(v):
        hi = v.astype(BF16).astype(F32)
        return hi, v - hi

    n_q = IDX_HEADS * IDX_DIM // LANES
    for g4 in range(n_q):
        xg = pi[:, g4 * LANES:(g4 + 1) * LANES]
        hi, lo = hi_lo(xg * cosi + _rope_partner_64(xg) * sini)
        hi_r, lo_r = pltpu.roll(hi, half, 1), pltpu.roll(lo, half, 1)
        for head, parts in ((2 * g4, (jnp.where(first, hi, lo_r), jnp.where(first, hi, 0.0))),
                            (2 * g4 + 1, (jnp.where(first, hi_r, lo), jnp.where(first, hi_r, 0.0)))):
            for pi_, part in enumerate(parts):
                a_ref[0, rows, head * IDX_K + pi_ * LANES:head * IDX_K + (pi_ + 1) * LANES] = part.astype(BF16)
    xg = pi[:, n_q * LANES:(n_q + 1) * LANES]
    roped = xg * cosi + _rope_partner_64(xg) * sini
    hi, lo = hi_lo(roped)
    kc_ref[0, rows, :LANES] = jnp.where(first, hi, pltpu.roll(hi, half, 1)).astype(BF16)
    kc_ref[0, rows, LANES:] = jnp.where(first, lo, 0.0).astype(BF16)
    w_scale = (IDX_HEADS ** -0.5) * (IDX_DIM ** -0.5)
    kw_ref[0, rows, :] = jnp.where(first, roped, xg * w_scale)


def _proj_call(x, mod3, gmix, wqk, wvt, wg, wi, gains, cosh, sinh, cosi, sini):
    bsz, seq, d = x.shape
    tm = TM_PROJ
    nb = seq // MOBA_BLOCK
    const = lambda b, i: (0, 0)
    tok = lambda b, i: (b, i, 0)
    tab = lambda b, i: (i, 0)
    resident = lambda a: pl.BlockSpec(a.shape, const, pipeline_mode=pl.Buffered(1))
    out_shape = (
        [jax.ShapeDtypeStruct((bsz, seq, BR_W), BF16)] * 4
        + [jax.ShapeDtypeStruct((bsz, seq // TQ, N_HEADS * VT_ROWS, TQ), BF16)] * 2
        + [jax.ShapeDtypeStruct((bsz, seq, D_MODEL), BF16)] * 2
        + [jax.ShapeDtypeStruct((bsz, seq, IDX_HEADS * IDX_K), BF16),
           jax.ShapeDtypeStruct((bsz, seq, IDX_K), BF16),
           jax.ShapeDtypeStruct((bsz, seq, LANES), F32),
           jax.ShapeDtypeStruct((bsz, nb, 1, BR_W), F32)]
    )
    out_specs = (
        [pl.BlockSpec((1, tm, BR_W), tok)] * 4
        + [pl.BlockSpec((1, tm // TQ, N_HEADS * VT_ROWS, TQ), lambda b, i: (b, i, 0, 0))] * 2
        + [pl.BlockSpec((1, tm, D_MODEL), tok)] * 2
        + [pl.BlockSpec((1, tm, IDX_HEADS * IDX_K), tok),
           pl.BlockSpec((1, tm, IDX_K), tok),
           pl.BlockSpec((1, tm, LANES), tok),
           pl.BlockSpec((1, tm // MOBA_BLOCK, 1, BR_W), lambda b, i: (b, i, 0, 0))]
    )
    return pl.pallas_call(
        _proj_kernel,
        grid=(bsz, seq // tm),
        in_specs=[
            pl.BlockSpec((1, tm, d), tok),
            pl.BlockSpec((1, N_MOD, d), lambda b, i: (b, 0, 0)),
            pl.BlockSpec((1, d), const),
            resident(wqk), resident(wvt), resident(wg), resident(wi),
            pl.BlockSpec(gains.shape, const),
            pl.BlockSpec((tm, LANES), tab),
            pl.BlockSpec((tm, LANES), tab),
            pl.BlockSpec((tm, LANES), tab),
            pl.BlockSpec((tm, LANES), tab),
        ],
        out_specs=out_specs,
        out_shape=out_shape,
        compiler_params=pltpu.CompilerParams(
            dimension_semantics=("arbitrary", "arbitrary"), vmem_limit_bytes=VMEM_LIMIT),
        name="proj",
    )(x, mod3, gmix, wqk, wvt, wg, wi, gains, cosh, sinh, cosi, sini)


def _tile3(x):
    return x.reshape(x.shape[0] // SUBLANES, SUBLANES, x.shape[1])


def _allreduce_sublanes(x, op):
    for shift in (4, 2, 1):
        x = op(x, pltpu.roll(x, shift, 0))
    return x


def _softmax_tile(s3, vt, m_old, acc_ref, hh):
    rows = slice(hh * VT_ROWS, (hh + 1) * VT_ROWS)
    m_new = jnp.maximum(m_old, _allreduce_sublanes(jnp.max(s3, axis=0), jnp.maximum))
    alpha = jnp.exp2(m_old - m_new)
    p = jnp.exp2(s3 - m_new[None]).reshape(s3.shape[0] * SUBLANES, TQ).astype(BF16)
    acc_ref[rows, :] = (_tile3(acc_ref[rows, :]) * alpha[None]).reshape(VT_ROWS, TQ) + _dot(vt, p)
    return m_new


def _qk_tile(k_ref, q_ref, t, s_ref):
    k0 = pl.multiple_of(t * TQ, TQ)
    for hh, hs in enumerate(_HEAD_SLICES):
        s_ref[hh] = _dot_nt(k_ref[0, pl.ds(k0, TQ), hs], q_ref[0, :, hs])


def _edge_mask(j, t_true):
    shape = (TQ // SUBLANES, SUBLANES, TQ)
    row = lax.broadcasted_iota(I32, shape, 0) * SUBLANES + lax.broadcasted_iota(I32, shape, 1)
    return row - lax.broadcasted_iota(I32, shape, 2) <= (j - t_true) * TQ


def _sweep_tiles(j, s_refs, qk, consume, state):
    s0, s1 = s_refs
    last_pair = j // 2
    qk(0, s0)

    def body(p, st):
        t0 = 2 * p
        qk(t0 + 1, s1)
        st = consume(t0, t0, s0, st, False)
        qk(t0 + 2, s0)
        return consume(t0 + 1, t0 + 1, s1, st, False)

    state = lax.fori_loop(0, last_pair // 2, lambda q, st: body(2 * q + 1, body(2 * q, st)), state)
    state = lax.fori_loop(2 * (last_pair // 2), last_pair, body, state)
    ta = 2 * last_pair

    @pl.when(ta == j)
    def _():
        consume(j, j, s0, state, True)

    @pl.when(ta != j)
    def _():
        qk(j, s1)
        consume(j, j, s1, consume(ta, ta, s0, state, False), True)


def _init_state(acc_ref):
    acc_ref[...] = jnp.zeros(acc_ref.shape, F32)
    return (jnp.full((SUBLANES, TQ), M_INIT, F32),) * N_HEADS


def _write_heads(o_ref, acc_ref):
    for hh, hs in enumerate(_HEAD_SLICES):
        base = hh * VT_ROWS
        inv = 1.0 / acc_ref[base + HEAD_DIM:base + HEAD_DIM + SUBLANES, :]
        o_t = (_tile3(acc_ref[base:base + HEAD_DIM, :]) * inv[None]).reshape(HEAD_DIM, TQ)
        o_ref[0, :, hs] = o_t.T.astype(BF16)


def _f32_from_key(key):
    bits = jnp.where(key < 0, key ^ jnp.int32(0x7FFFFFFF), key)
    return lax.bitcast_convert_type(bits, F32)


def _key32_of_key16(key16):
    return (key16 << 16) + jnp.where(key16 < 0, 0xFFFF, 0)


def _bf16_from_key16(key16):
    return _f32_from_key(_key32_of_key16(key16))


def _dsa_kernel(a_ref, kc_ref, wk_ref, qa_ref, ka_ref, vat_ref, o_ref,
                scores_ref, hb_ref, red_ref, fold_ref, thr_ref, cge_ref, s0_ref, s1_ref, acc_ref, *, topk, seq_bits):
    j = pl.program_id(1)
    n_tiles = j + 1
    w_rows = wk_ref[0].T[IDX_DIM:IDX_DIM + IDX_HEADS, :]
    w8 = [jnp.broadcast_to(w_rows[hh:hh + 1, :], (SUBLANES, TQ)) for hh in range(IDX_HEADS)]

    def score_tile(c, maybe_diagonal):
        k0 = pl.multiple_of(c * TQ, TQ)
        kc = kc_ref[0, pl.ds(k0, TQ), :]
        acc = jnp.zeros((TQ // SUBLANES, SUBLANES, TQ), F32)
        for hh in range(IDX_HEADS):
            lg = _dot_nt(kc, a_ref[0, :, hh * IDX_K:(hh + 1) * IDX_K])
            acc = acc + w8[hh][None] * jnp.maximum(_tile3(lg), 0.0)
        if maybe_diagonal:
            acc = jnp.where(_edge_mask(j, c), acc, NEG)
        acc = acc.reshape(TQ, TQ)
        scores_ref[pl.ds(k0, TQ), :] = acc
        hb_ref[pl.ds(k0, TQ), :] = acc.astype(BF16)

    def score_run(start, n_trips, width):
        def trip(g, carry):
            for u in range(width):
                score_tile(start + width * g + u, u == width - 1)
            return carry

        lax.fori_loop(0, n_trips, trip, 0)
        return start + width * n_trips

    done = 0
    for width in SCORE_GROUPS:
        done = score_run(done, (n_tiles - done) // width, width)

    def load_packed(c, u):
        r0 = pl.multiple_of(c * TQ + u * SEL_ROWS, SEL_ROWS)
        return hb_ref[pl.ds(r0, SEL_ROWS), :].reshape(SEL_ROWS // PACK_ROWS, PACK_ROWS, TQ)

    def load_scores(c, u):
        r0 = pl.multiple_of(c * TQ + u * SEL_ROWS, SEL_ROWS)
        return _tile3(scores_ref[pl.ds(r0, SEL_ROWS), :]), r0

    def sweep_count(tile, zero):
        cnt = lax.fori_loop(0, n_tiles // 2, lambda p, cnt: tile(2 * p + 1, tile(2 * p, cnt)), zero)
        return lax.fori_loop(2 * (n_tiles // 2), n_tiles, tile, cnt)

    one16, zero16 = jnp.int16(1), jnp.int16(0)

    def count_bf16(test):
        t = jnp.broadcast_to(test, (PACK_ROWS, TQ)).astype(BF16)[None]

        def tile(c, cnt):
            for u in range(TQ // SEL_ROWS):
                cnt = cnt + jnp.where(load_packed(c, u) >= t, one16, zero16)
            return cnt

        cnt = sweep_count(tile, jnp.zeros((SEL_ROWS // PACK_ROWS, PACK_ROWS, TQ), I16))
        return jnp.sum(jnp.sum(cnt.astype(F32), axis=0), axis=0, keepdims=True)

    def count_f32(pred):
        def tile(c, cnt):
            for u in range(TQ // SEL_ROWS):
                sc, r0 = load_scores(c, u)
                cnt = cnt + jnp.where(pred(sc, r0), 1.0, 0.0)
            return cnt

        cnt = sweep_count(tile, jnp.zeros((SEL_ROWS // SUBLANES, SUBLANES, TQ), F32))
        return jnp.sum(jnp.sum(cnt, axis=0), axis=0, keepdims=True)

    kf = float(topk)

    c0 = count_bf16(jnp.zeros((1, TQ), F32))
    start = jnp.where(c0 >= kf, 0, I16_MIN).astype(I32)

    def bit_step(i, val):
        test = val | (jnp.int32(1) << (14 - i))
        return jnp.where(count_bf16(_bf16_from_key16(test)) >= kf, test, val)

    hb_k = lax.fori_loop(0, 15, bit_step, start)

    below = _key32_of_key16(jnp.maximum(hb_k - 1, BF16_KEY_NEG_INF))
    lo = below + ((_key32_of_key16(hb_k) - below) >> 1) - 1
    hi = _key32_of_key16(jnp.minimum(hb_k + 1, BF16_KEY_POS_INF))

    def bisect(count_ge):
        def step(i, carry):
            lo, hi, cge = carry
            mid = lo + ((hi - lo) >> 1)
            c = count_ge(_f32_from_key(mid)[None])
            ok = c >= kf
            return jnp.where(ok, mid, lo), jnp.where(ok, hi, mid), jnp.where(ok, c, cge)

        lo_k, _, cge = lax.fori_loop(0, BISECT_STEPS, step, (lo, hi, jnp.full((1, TQ), kf, F32)))
        thr_ref[...] = jnp.broadcast_to(_f32_from_key(lo_k), (SUBLANES, TQ))
        cge_ref[...] = jnp.broadcast_to(cge, (SUBLANES, TQ))

    lo_f, hi_f = _f32_from_key(lo), _f32_from_key(hi)
    ninf = jnp.full((SUBLANES, TQ), -jnp.inf, F32)

    def reduce_tile(c, carry):
        above, spill = carry
        k0 = pl.multiple_of(c * TQ, TQ)
        x3 = _tile3(scores_ref[pl.ds(k0, TQ), :])
        top = [ninf] * KEEP
        for g in range(TQ // SUBLANES):
            x = x3[g]
            ge_hi = x >= hi_f
            above = above + jnp.where(ge_hi, 1.0, 0.0)
            t = jnp.where((x >= lo_f) & ~ge_hi, x, -jnp.inf)
            for r in range(KEEP):
                top[r], t = jnp.maximum(top[r], t), jnp.minimum(top[r], t)
            spill = jnp.maximum(spill, t)
        for r in range(KEEP):
            red_ref[c, r] = top[r]
        return above, spill

    above, spill = lax.fori_loop(0, n_tiles, reduce_tile, (jnp.zeros((SUBLANES, TQ), F32), ninf))
    above = jnp.sum(above, axis=0, keepdims=True)
    reduced_ok = jnp.max(spill) == -jnp.inf

    @pl.when(reduced_ok)
    def _():
        def fold_tile(c, carry):
            top, spill2 = carry
            for r in range(KEEP):
                t = red_ref[c, r]
                nxt = []
                for q in range(KEEP_ALL):
                    nxt.append(jnp.maximum(top[q], t))
                    t = jnp.minimum(top[q], t)
                top = tuple(nxt)
                spill2 = jnp.maximum(spill2, t)
            return top, spill2

        top, spill2 = lax.fori_loop(0, n_tiles, fold_tile, ((ninf,) * KEEP_ALL, ninf))
        for q in range(KEEP_ALL):
            fold_ref[q] = top[q]
        folded_ok = jnp.max(spill2) == -jnp.inf

        @pl.when(folded_ok)
        def _():
            def count_ge(t3):
                cnt = jnp.zeros((SUBLANES, TQ), F32)
                for q in range(KEEP_ALL):
                    cnt = cnt + jnp.where(fold_ref[q] >= t3[0], 1.0, 0.0)
                return above + jnp.sum(cnt, axis=0, keepdims=True)
            bisect(count_ge)

        @pl.when(jnp.logical_not(folded_ok))
        def _():
            def count_ge(t3):
                def tile(c, cnt):
                    for r in range(KEEP):
                        cnt = cnt + jnp.where(red_ref[c, r] >= t3[0], 1.0, 0.0)
                    return cnt
                cnt = sweep_count(tile, jnp.zeros((SUBLANES, TQ), F32))
                return above + jnp.sum(cnt, axis=0, keepdims=True)
            bisect(count_ge)

    @pl.when(jnp.logical_not(reduced_ok))
    def _():
        bisect(lambda t3: count_f32(lambda sc, r0: sc >= t3))

    thr = thr_ref[0:1, :]
    cge = cge_ref[0:1, :]

    @pl.when(jnp.max(jnp.abs(cge - kf)) > 0.0)
    def _():
        thr3 = thr[None]
        need = kf - count_f32(lambda sc, r0: sc > thr3)
        sub = (lax.broadcasted_iota(I32, (SEL_ROWS // SUBLANES, SUBLANES, TQ), 0) * SUBLANES
               + lax.broadcasted_iota(I32, (SEL_ROWS // SUBLANES, SUBLANES, TQ), 1))

        def cut_step(i, cut):
            test = cut | (jnp.int32(1) << (seq_bits - 1 - i))
            f = count_f32(lambda sc, r0: (sc == thr3) & (r0 + sub < test[None]))
            return jnp.where(f < need, test, cut)

        cut = lax.fori_loop(0, seq_bits, cut_step, jnp.zeros((1, TQ), I32))

        def demote(c, carry):
            r0 = pl.multiple_of(c * SEL_ROWS, SEL_ROWS)
            sc = _tile3(scores_ref[pl.ds(r0, SEL_ROWS), :])
            scores_ref[pl.ds(r0, SEL_ROWS), :] = jnp.where(
                (sc == thr3) & (r0 + sub > cut[None]), NEG, sc).reshape(SEL_ROWS, TQ)
            return carry

        lax.fori_loop(0, n_tiles * (TQ // SEL_ROWS), demote, 0)

    def consume(t, t_true, s_ref, state, edge):
        k0 = pl.multiple_of(t * TQ, TQ)
        sel = _tile3(scores_ref[pl.ds(k0, TQ), :]) >= thr[None]
        if edge:
            sel = sel & _edge_mask(j, t_true)
        bias = jnp.where(sel, 0.0, NEG)
        return tuple(
            _softmax_tile(_tile3(s_ref[hh]) + bias, vat_ref[0, t, hh * VT_ROWS:(hh + 1) * VT_ROWS, :],
                          state[hh], acc_ref, hh)
            for hh in range(N_HEADS))

    qk = functools.partial(_qk_tile, ka_ref, qa_ref)
    _sweep_tiles(j, (s0_ref, s1_ref), qk, consume, _init_state(acc_ref))
    _write_heads(o_ref, acc_ref)


def _dsa_call(a_mat, kc, idx_out, qa, ka, vat):
    bsz, seq, _ = qa.shape
    topk = min(DSA_TOPK_MAX, seq // 4)
    seq_bits = (seq - 1).bit_length()
    kern = functools.partial(_dsa_kernel, topk=topk, seq_bits=seq_bits)
    return pl.pallas_call(
        kern,
        grid=(bsz, seq // TQ),
        in_specs=[
            pl.BlockSpec((1, TQ, IDX_HEADS * IDX_K), lambda b, j: (b, j, 0)),
            pl.BlockSpec((1, seq, IDX_K), lambda b, j: (b, 0, 0)),
            pl.BlockSpec((1, TQ, LANES), lambda b, j: (b, j, 0)),
            pl.BlockSpec((1, TQ, BR_W), lambda b, j: (b, j, 0)),
            pl.BlockSpec((1, seq, BR_W), lambda b, j: (b, 0, 0)),
            pl.BlockSpec((1, seq // TQ, N_HEADS * VT_ROWS, TQ), lambda b, j: (b, 0, 0, 0)),
        ],
        out_specs=pl.BlockSpec((1, TQ, BR_W), lambda b, j: (b, j, 0)),
        out_shape=jax.ShapeDtypeStruct((bsz, seq, BR_W), BF16),
        scratch_shapes=[
            pltpu.VMEM((seq, TQ), F32),
            pltpu.VMEM((seq, TQ), BF16),
            pltpu.VMEM((seq // TQ, KEEP, SUBLANES, TQ), F32),
            pltpu.VMEM((KEEP_ALL, SUBLANES, TQ), F32),
            pltpu.VMEM((SUBLANES, TQ), F32),
            pltpu.VMEM((SUBLANES, TQ), F32),
            pltpu.VMEM((N_HEADS, TQ, TQ), F32),
            pltpu.VMEM((N_HEADS, TQ, TQ), F32),
            pltpu.VMEM((N_HEADS * VT_ROWS, TQ), F32),
        ],
        compiler_params=pltpu.CompilerParams(
            dimension_semantics=("arbitrary", "arbitrary"), vmem_limit_bytes=VMEM_LIMIT),
        name="dsa",
    )(a_mat, kc, idx_out, qa, ka, vat)


def _moba_kernel(qb_ref, kb_ref, vbt_ref, kmean_ref, o_ref, bias_ref, s0_ref, s1_ref, acc_ref, *, n_sel):
    j = pl.program_id(1)
    nb = kmean_ref.shape[1]
    blk = lax.broadcasted_iota(I32, (nb, TQ), 0).astype(F32)
    jf = j.astype(F32)

    for hh in range(N_HEADS):
        hs = slice(hh * HEAD_DIM, (hh + 1) * HEAD_DIM)
        km_hi, km_lo = _split_bf16(kmean_ref[0, :, hs])
        q = qb_ref[0, :, hs]
        gate = jnp.where(blk < jf, _dot_nt(km_hi, q) + _dot_nt(km_lo, q), NEG)
        chosen = jnp.zeros((nb, TQ), F32)
        for _ in range(n_sel):
            best = jnp.max(gate, axis=0, keepdims=True)
            first = jnp.min(jnp.where(gate == best, blk, float(nb)), axis=0, keepdims=True)
            hit = blk == first
            chosen = jnp.where(hit, 1.0, chosen)
            gate = jnp.where(hit, -jnp.inf, gate)
        bias = jnp.where((chosen > 0.0) & (blk < jf), 0.0, NEG)
        for n in range(nb):
            bias_ref[hh, n] = jnp.broadcast_to(bias[n:n + 1, :], (SUBLANES, TQ))

    def consume(t, t_true, s_ref, state, edge):
        new = []
        for hh in range(N_HEADS):
            if edge:
                s3 = jnp.where(_edge_mask(j, t_true), _tile3(s_ref[hh]), NEG)
            else:
                s3 = _tile3(s_ref[hh]) + bias_ref[hh, t][None]
            new.append(_softmax_tile(s3, vbt_ref[0, t, hh * VT_ROWS:(hh + 1) * VT_ROWS, :], state[hh], acc_ref, hh))
        return tuple(new)

    qk = functools.partial(_qk_tile, kb_ref, qb_ref)
    _sweep_tiles(j, (s0_ref, s1_ref), qk, consume, _init_state(acc_ref))
    _write_heads(o_ref, acc_ref)


def _moba_call(qb, kb, vbt, kmean):
    bsz, seq, _ = qb.shape
    nb = seq // MOBA_BLOCK
    kern = functools.partial(_moba_kernel, n_sel=min(MOBA_TOPK, nb - 1))
    return pl.pallas_call(
        kern,
        grid=(bsz, seq // TQ),
        in_specs=[
            pl.BlockSpec((1, TQ, BR_W), lambda b, j: (b, j, 0)),
            pl.BlockSpec((1, seq, BR_W), lambda b, j: (b, 0, 0)),
            pl.BlockSpec((1, seq // TQ, N_HEADS * VT_ROWS, TQ), lambda b, j: (b, 0, 0, 0)),
            pl.BlockSpec((1, nb, BR_W), lambda b, j: (b, 0, 0)),
        ],
        out_specs=pl.BlockSpec((1, TQ, BR_W), lambda b, j: (b, j, 0)),
        out_shape=jax.ShapeDtypeStruct((bsz, seq, BR_W), BF16),
        scratch_shapes=[
            pltpu.VMEM((N_HEADS, nb, SUBLANES, TQ), F32),
            pltpu.VMEM((N_HEADS, TQ, TQ), F32),
            pltpu.VMEM((N_HEADS, TQ, TQ), F32),
            pltpu.VMEM((N_HEADS * VT_ROWS, TQ), F32),
        ],
        compiler_params=pltpu.CompilerParams(
            dimension_semantics=("arbitrary", "arbitrary"), vmem_limit_bytes=VMEM_LIMIT),
        name="moba",
    )(qb, kb, vbt, kmean)


def _out_ffn_kernel(x_ref, oa_ref, ob_ref, ga_ref, gb_ref, mod_ref, gffn_ref,
                    wba_ref, wbb_ref, wo_ref, wgu_ref, wd_ref, o_ref, act_ref):
    merged = (ga_ref[0].astype(F32) * _dot(oa_ref[0], wba_ref[...])
              + gb_ref[0].astype(F32) * _dot(ob_ref[0], wbb_ref[...]))
    x1 = x_ref[0] + mod_ref[0, 2:3, :] * _dot(merged.astype(BF16), wo_ref[...])
    y = x1 * lax.rsqrt(jnp.mean(x1 * x1, axis=-1, keepdims=True) + EPS) * gffn_ref[...]
    h = (y * (1.0 + mod_ref[0, 4:5, :]) + mod_ref[0, 3:4, :]).astype(BF16)
    for c0 in range(0, D_FF, FF_CHUNK):
        g = _dot(h, wgu_ref[:, c0:c0 + FF_CHUNK])
        u = _dot(h, wgu_ref[:, D_FF + c0:D_FF + c0 + FF_CHUNK])
        act_ref[:, c0:c0 + FF_CHUNK] = (g * jax.nn.sigmoid(g) * u).astype(BF16)
    o_ref[0] = x1 + mod_ref[0, 5:6, :] * _dot(act_ref[...], wd_ref[...])


def _out_ffn_call(x, oa, ob, ga, gb, mod3, gffn, wba, wbb, wo, wgu, wd):
    bsz, seq, d = x.shape
    tm = TM_FFN
    const = lambda b, i: (0, 0)
    tok = lambda b, i: (b, i, 0)
    resident = lambda a: pl.BlockSpec(a.shape, const, pipeline_mode=pl.Buffered(1))
    return pl.pallas_call(
        _out_ffn_kernel,
        grid=(bsz, seq // tm),
        in_specs=[
            pl.BlockSpec((1, tm, d), tok),
            pl.BlockSpec((1, tm, BR_W), tok),
            pl.BlockSpec((1, tm, BR_W), tok),
            pl.BlockSpec((1, tm, d), tok),
            pl.BlockSpec((1, tm, d), tok),
            pl.BlockSpec((1, N_MOD, d), lambda b, i: (b, 0, 0)),
            pl.BlockSpec((1, d), const),
            resident(wba), resident(wbb), resident(wo), resident(wgu), resident(wd),
        ],
        out_specs=pl.BlockSpec((1, tm, d), tok),
        out_shape=jax.ShapeDtypeStruct((bsz, seq, d), F32),
        scratch_shapes=[pltpu.VMEM((tm, D_FF), BF16)],
        compiler_params=pltpu.CompilerParams(
            dimension_semantics=("arbitrary", "arbitrary"), vmem_limit_bytes=VMEM_LIMIT),
        name="out_ffn",
    )(x, oa, ob, ga, gb, mod3, gffn, wba, wbb, wo, wgu, wd)


def _rope_tables(seq, dim):
    inv = ROPE_THETA ** (-jnp.arange(0, dim, 2, dtype=F32) / dim)
    ang = jnp.arange(seq, dtype=F32)[:, None] * inv[None, :]
    cos, sin = jnp.cos(ang), jnp.sin(ang)
    reps = LANES // dim
    return (jnp.tile(jnp.concatenate([cos, cos], axis=-1), (1, reps)),
            jnp.tile(jnp.concatenate([-sin, sin], axis=-1), (1, reps)))


def _layer(x, mod, g_mix, g_ffn, w_in, g_qa, g_ka, g_qb, g_kb, w_br_a, w_br_b, w_out, w_gu, w_down):
    bsz, seq, d = x.shape
    mod3 = mod.reshape(bsz, N_MOD, d)

    o = 0
    cols = {}
    for name, width in (("qa", BR_W), ("ka", BR_W), ("va", BR_W), ("qi", IDX_HEADS * IDX_DIM),
                        ("ki", IDX_DIM), ("wi", IDX_HEADS), ("qb", BR_W), ("kb", BR_W), ("vb", BR_W),
                        ("ga", D_MODEL), ("gb", D_MODEL)):
        cols[name] = w_in[:, o:o + width]
        o += width
    wqk = jnp.concatenate([cols["qa"], cols["ka"], cols["qb"], cols["kb"]], axis=1).astype(BF16)
    wvt = jnp.concatenate([cols["va"], cols["vb"]], axis=1).T.astype(BF16)
    wg = jnp.concatenate([cols["ga"], cols["gb"]], axis=1).astype(BF16)
    pad = IDX_OUT_W - IDX_HEADS * IDX_DIM - IDX_DIM - IDX_HEADS
    wi = jnp.concatenate([cols["qi"], cols["ki"], cols["wi"], jnp.zeros((d, pad), F32)], axis=1).astype(BF16)
    gains = jnp.stack([g_qa, g_ka, g_qb, g_kb])
    cosh, sinh = _rope_tables(seq, HEAD_DIM)
    cosi, sini = _rope_tables(seq, IDX_DIM)

    qa, ka, qb, kb, vat, vbt, ga, gb, a_mat, kc, kw, kmean = _proj_call(
        x, mod3, g_mix.reshape(1, d), wqk, wvt, wg, wi, gains, cosh, sinh, cosi, sini)

    oa = _dsa_call(a_mat, kc, kw, qa, ka, vat)
    ob = _moba_call(qb, kb, vbt, kmean.reshape(bsz, seq // MOBA_BLOCK, BR_W))

    return _out_ffn_call(x, oa, ob, ga, gb, mod3, g_ffn.reshape(1, d),
                         w_br_a.astype(BF16), w_br_b.astype(BF16), w_out.astype(BF16),
                         w_gu.astype(BF16), w_down.astype(BF16))


def kernel(x, c, w_mod, b_mod, g_mix_norm, g_ffn_norm, w_in, g_q_dsa, g_k_dsa, g_q_moba, g_k_moba,
           w_br_dsa, w_br_moba, w_out, w_gate_up, w_down):
    assert x.shape[1] % TM_PROJ == 0 and x.shape[2] == D_MODEL and w_gate_up.shape[2] == 2 * D_FF
    for l in range(w_mod.shape[0]):
        mod = _mod_call(c, w_mod[l], b_mod[l])
        x = _layer(x, mod, g_mix_norm[l], g_ffn_norm[l], w_in[l], g_q_dsa[l], g_k_dsa[l], g_q_moba[l],
                   g_k_moba[l], w_br_dsa[l], w_br_moba[l], w_out[l], w_gate_up[l], w_down[l])
    return x
```

```python
import functools

import jax
import jax.numpy as jnp
from jax import lax
from jax.experimental import pallas as pl
from jax.experimental.pallas import tpu as pltpu

F32 = jnp.float32
BF16 = jnp.bfloat16
I32 = jnp.int32
I16 = jnp.int16

D_MODEL = 1024
HEAD_DIM = 128
N_HEADS = 4
BR_W = N_HEADS * HEAD_DIM
IDX_HEADS = 8
IDX_DIM = 64
DSA_TOPK_MAX = 256
MOBA_BLOCK = 256
MOBA_TOPK = 3
D_FF = 2816
ROPE_THETA = 10000.0
EPS = 1e-6
NEG = -1e30
LOG2_E = 1.4426950408889634
M_INIT = -1e29
N_MOD = 6
IDX_OUT_W = 640
IDX_K = 256
I16_MIN = -(2 ** 15)
BF16_KEY_NEG_INF = (0xFF80 ^ 0x7FFF) - 0x10000
BF16_KEY_POS_INF = 0x7F80
BISECT_STEPS = 17

LANES = 128
SUBLANES = 8
PACK_ROWS = 16
VT_ROWS = HEAD_DIM + PACK_ROWS
TQ = 256
TM_PROJ = 512
PROJ_ROWS = 256
TM_FFN = 512
FF_CHUNK = 256
SEL_ROWS = 32
KEEP = 3
KEEP_ALL = 8
SCORE_GROUPS = (4, 2, 1)
VMEM_LIMIT = 56 * 1024 * 1024

_NT = (((1,), (1,)), ((), ()))
_HEAD_SLICES = tuple(slice(h * HEAD_DIM, (h + 1) * HEAD_DIM) for h in range(N_HEADS))


def _split_bf16(x):
    hi = x.astype(BF16)
    lo = (x - hi.astype(F32)).astype(BF16)
    return hi, lo


def _dot(a, b):
    return jnp.dot(a, b, preferred_element_type=F32)


def _dot_nt(a, b):
    return lax.dot_general(a, b, _NT, preferred_element_type=F32)


def _mod_kernel(c_ref, w_ref, b_ref, o_ref):
    c = c_ref[...]
    a_hi, a_lo = _split_bf16(c * jax.nn.sigmoid(c))
    w_hi, w_lo = _split_bf16(w_ref[...])
    o_ref[...] = _dot(a_hi, w_hi) + _dot(a_lo, w_hi) + _dot(a_hi, w_lo) + b_ref[...]


def _mod_call(c, w_mod, b_mod):
    bsz, d = c.shape
    n = w_mod.shape[1]
    tn = 2048
    return pl.pallas_call(
        _mod_kernel,
        grid=(n // tn,),
        in_specs=[
            pl.BlockSpec((bsz, d), lambda i: (0, 0)),
            pl.BlockSpec((d, tn), lambda i: (0, i)),
            pl.BlockSpec((1, tn), lambda i: (0, i)),
        ],
        out_specs=pl.BlockSpec((bsz, tn), lambda i: (0, i)),
        out_shape=jax.ShapeDtypeStruct((bsz, n), F32),
        compiler_params=pltpu.CompilerParams(vmem_limit_bytes=VMEM_LIMIT),
        name="mod",
    )(c, w_mod, b_mod.reshape(1, n))


def _rope_partner_64(x):
    lane = lax.broadcasted_iota(I32, x.shape, 1)
    return jnp.where((lane % IDX_DIM) < IDX_DIM // 2,
                     pltpu.roll(x, LANES - IDX_DIM // 2, 1), pltpu.roll(x, IDX_DIM // 2, 1))


def _proj_kernel(x_ref, mod_ref, gmix_ref, wqk_ref, wvt_ref, wg_ref, wi_ref, gains_ref,
                 cosh_ref, sinh_ref, cosi_ref, sini_ref,
                 qa_ref, ka_ref, qb_ref, kb_ref, vat_ref, vbt_ref, ga_ref, gb_ref, a_ref, kc_ref, kw_ref, kmean_ref):
    for blk in range(x_ref.shape[1] // PROJ_ROWS):
        _proj_rows(blk, x_ref, mod_ref, gmix_ref, wqk_ref, wvt_ref, wg_ref, wi_ref, gains_ref,
                   cosh_ref, sinh_ref, cosi_ref, sini_ref,
                   qa_ref, ka_ref, qb_ref, kb_ref, vat_ref, vbt_ref, ga_ref, gb_ref, a_ref, kc_ref, kw_ref, kmean_ref)


def _proj_rows(blk, x_ref, mod_ref, gmix_ref, wqk_ref, wvt_ref, wg_ref, wi_ref, gains_ref,
               cosh_ref, sinh_ref, cosi_ref, sini_ref,
               qa_ref, ka_ref, qb_ref, kb_ref, vat_ref, vbt_ref, ga_ref, gb_ref, a_ref, kc_ref, kw_ref, kmean_ref):
    rows = slice(blk * PROJ_ROWS, (blk + 1) * PROJ_ROWS)
    x = x_ref[0, rows, :]
    y = x * lax.rsqrt(jnp.mean(x * x, axis=-1, keepdims=True) + EPS) * gmix_ref[...]
    h = y * (1.0 + mod_ref[0, 1:2, :]) + mod_ref[0, 0:1, :]
    h_hi = h.astype(BF16)
    cosh, sinh = cosh_ref[rows, :], sinh_ref[rows, :]
    scale = HEAD_DIM ** -0.5 * LOG2_E

    for gi, out_ref in enumerate((qa_ref, ka_ref, qb_ref, kb_ref)):
        p = _dot(h_hi, wqk_ref[:, gi * BR_W:(gi + 1) * BR_W])
        for hh in range(N_HEADS):
            ph = p[:, hh * HEAD_DIM:(hh + 1) * HEAD_DIM]
            yh = ph * lax.rsqrt(jnp.mean(ph * ph, axis=-1, keepdims=True) + EPS) * gains_ref[gi:gi + 1, :]
            r = yh * cosh + pltpu.roll(yh, HEAD_DIM // 2, 1) * sinh
            if gi == 3:
                kmean_ref[0, blk, :, hh * HEAD_DIM:(hh + 1) * HEAD_DIM] = jnp.mean(r, axis=0, keepdims=True)
            if gi % 2 == 0:
                r = r * scale
            out_ref[0, rows, hh * HEAD_DIM:(hh + 1) * HEAD_DIM] = r.astype(BF16)

    vt = _dot_nt(wvt_ref[...], h_hi)
    ones = jnp.ones((VT_ROWS - HEAD_DIM, TQ), BF16)
    for gi, out_ref in enumerate((vat_ref, vbt_ref)):
        for hh in range(N_HEADS):
            src = gi * BR_W + hh * HEAD_DIM
            out_ref[0, blk, hh * VT_ROWS:hh * VT_ROWS + HEAD_DIM, :] = vt[src:src + HEAD_DIM, :].astype(BF16)
            out_ref[0, blk, hh * VT_ROWS + HEAD_DIM:(hh + 1) * VT_ROWS, :] = ones

    for gi, out_ref in enumerate((ga_ref, gb_ref)):
        g = _dot(h_hi, wg_ref[:, gi * D_MODEL:(gi + 1) * D_MODEL])
        out_ref[0, rows, :] = jax.nn.sigmoid(g).astype(BF16)

    pi = _dot(h_hi, wi_ref[...])
    cosi, sini = cosi_ref[rows, :], sini_ref[rows, :]
    first = lax.broadcasted_iota(I32, (PROJ_ROWS, LANES), 1) < IDX_DIM
    half = LANES // 2

    def hi_lo(v):
        hi = v.astype(BF16).astype(F32)
        return hi, v - hi

    n_q = IDX_HEADS * IDX_DIM // LANES
    for g4 in range(n_q):
        xg = pi[:, g4 * LANES:(g4 + 1) * LANES]
        hi, lo = hi_lo(xg * cosi + _rope_partner_64(xg) * sini)
        hi_r, lo_r = pltpu.roll(hi, half, 1), pltpu.roll(lo, half, 1)
        for head, parts in ((2 * g4, (jnp.where(first, hi, lo_r), jnp.where(first, hi, 0.0))),
                            (2 * g4 + 1, (jnp.where(first, hi_r, lo), jnp.where(first, hi_r, 0.0)))):
            for pi_, part in enumerate(parts):
                a_ref[0, rows, head * IDX_K + pi_ * LANES:head * IDX_K + (pi_ + 1) * LANES] = part.astype(BF16)
    xg = pi[:, n_q * LANES:(n_q + 1) * LANES]
    roped = xg * cosi + _rope_partner_64(xg) * sini
    hi, lo = hi_lo(roped)
    kc_ref[0, rows, :LANES] = jnp.where(first, hi, pltpu.roll(hi, half, 1)).astype(BF16)
    kc_ref[0, rows, LANES:] = jnp.where(first, lo, 0.0).astype(BF16)
    w_scale = (IDX_HEADS ** -0.5) * (IDX_DIM ** -0.5)
    kw_ref[0, rows, :] = jnp.where(first, roped, xg * w_scale)


def _proj_call(x, mod3, gmix, wqk, wvt, wg, wi, gains, cosh, sinh, cosi, sini):
    bsz, seq, d = x.shape
    tm = TM_PROJ
    nb = seq // MOBA_BLOCK
    const = lambda b, i: (0, 0)
    tok = lambda b, i: (b, i, 0)
    tab = lambda b, i: (i, 0)
    resident = lambda a: pl.BlockSpec(a.shape, const, pipeline_mode=pl.Buffered(1))
    out_shape = (
        [jax.ShapeDtypeStruct((bsz, seq, BR_W), BF16)] * 4
        + [jax.ShapeDtypeStruct((bsz, seq // TQ, N_HEADS * VT_ROWS, TQ), BF16)] * 2
        + [jax.ShapeDtypeStruct((bsz, seq, D_MODEL), BF16)] * 2
        + [jax.ShapeDtypeStruct((bsz, seq, IDX_HEADS * IDX_K), BF16),
           jax.ShapeDtypeStruct((bsz, seq, IDX_K), BF16),
           jax.ShapeDtypeStruct((bsz, seq, LANES), F32),
           jax.ShapeDtypeStruct((bsz, nb, 1, BR_W), F32)]
    )
    out_specs = (
        [pl.BlockSpec((1, tm, BR_W), tok)] * 4
        + [pl.BlockSpec((1, tm // TQ, N_HEADS * VT_ROWS, TQ), lambda b, i: (b, i, 0, 0))] * 2
        + [pl.BlockSpec((1, tm, D_MODEL), tok)] * 2
        + [pl.BlockSpec((1, tm, IDX_HEADS * IDX_K), tok),
           pl.BlockSpec((1, tm, IDX_K), tok),
           pl.BlockSpec((1, tm, LANES), tok),
           pl.BlockSpec((1, tm // MOBA_BLOCK, 1, BR_W), lambda b, i: (b, i, 0, 0))]
    )
    return pl.pallas_call(
        _proj_kernel,
        grid=(bsz, seq // tm),
        in_specs=[
            pl.BlockSpec((1, tm, d), tok),
            pl.BlockSpec((1, N_MOD, d), lambda b, i: (b, 0, 0)),
            pl.BlockSpec((1, d), const),
            resident(wqk), resident(wvt), resident(wg), resident(wi),
            pl.BlockSpec(gains.shape, const),
            pl.BlockSpec((tm, LANES), tab),
            pl.BlockSpec((tm, LANES), tab),
            pl.BlockSpec((tm, LANES), tab),
            pl.BlockSpec((tm, LANES), tab),
        ],
        out_specs=out_specs,
        out_shape=out_shape,
        compiler_params=pltpu.CompilerParams(
            dimension_semantics=("arbitrary", "arbitrary"), vmem_limit_bytes=VMEM_LIMIT),
        name="proj",
    )(x, mod3, gmix, wqk, wvt, wg, wi, gains, cosh, sinh, cosi, sini)


def _tile3(x):
    return x.reshape(x.shape[0] // SUBLANES, SUBLANES, x.shape[1])


def _allreduce_sublanes(x, op):
    for shift in (4, 2, 1):
        x = op(x, pltpu.roll(x, shift, 0))
    return x


def _softmax_tile(s3, vt, m_old, acc_ref, hh):
    rows = slice(hh * VT_ROWS, (hh + 1) * VT_ROWS)
    m_new = jnp.maximum(m_old, _allreduce_sublanes(jnp.max(s3, axis=0), jnp.maximum))
    alpha = jnp.exp2(m_old - m_new)
    p = jnp.exp2(s3 - m_new[None]).reshape(s3.shape[0] * SUBLANES, TQ).astype(BF16)
    acc_ref[rows, :] = (_tile3(acc_ref[rows, :]) * alpha[None]).reshape(VT_ROWS, TQ) + _dot(vt, p)
    return m_new


def _qk_tile(k_ref, q_ref, t, s_ref):
    k0 = pl.multiple_of(t * TQ, TQ)
    for hh, hs in enumerate(_HEAD_SLICES):
        s_ref[hh] = _dot_nt(k_ref[0, pl.ds(k0, TQ), hs], q_ref[0, :, hs])


def _edge_mask(j, t_true):
    shape = (TQ // SUBLANES, SUBLANES, TQ)
    row = lax.broadcasted_iota(I32, shape, 0) * SUBLANES + lax.broadcasted_iota(I32, shape, 1)
    return row - lax.broadcasted_iota(I32, shape, 2) <= (j - t_true) * TQ


def _sweep_tiles(j, s_refs, qk, consume, state):
    s0, s1 = s_refs
    last_pair = j // 2
    qk(0, s0)

    def body(p, st):
        t0 = 2 * p
        qk(t0 + 1, s1)
        st = consume(t0, t0, s0, st, False)
        qk(t0 + 2, s0)
        return consume(t0 + 1, t0 + 1, s1, st, False)

    state = lax.fori_loop(0, last_pair // 2, lambda q, st: body(2 * q + 1, body(2 * q, st)), state)
    state = lax.fori_loop(2 * (last_pair // 2), last_pair, body, state)
    ta = 2 * last_pair

    @pl.when(ta == j)
    def _():
        consume(j, j, s0, state, True)

    @pl.when(ta != j)
    def _():
        qk(j, s1)
        consume(j, j, s1, consume(ta, ta, s0, state, False), True)


def _init_state(acc_ref):
    acc_ref[...] = jnp.zeros(acc_ref.shape, F32)
    return (jnp.full((SUBLANES, TQ), M_INIT, F32),) * N_HEADS


def _write_heads(o_ref, acc_ref):
    for hh, hs in enumerate(_HEAD_SLICES):
        base = hh * VT_ROWS
        inv = 1.0 / acc_ref[base + HEAD_DIM:base + HEAD_DIM + SUBLANES, :]
        o_t = (_tile3(acc_ref[base:base + HEAD_DIM, :]) * inv[None]).reshape(HEAD_DIM, TQ)
        o_ref[0, :, hs] = o_t.T.astype(BF16)


def _f32_from_key(key):
    bits = jnp.where(key < 0, key ^ jnp.int32(0x7FFFFFFF), key)
    return lax.bitcast_convert_type(bits, F32)


def _key32_of_key16(key16):
    return (key16 << 16) + jnp.where(key16 < 0, 0xFFFF, 0)


def _bf16_from_key16(key16):
    return _f32_from_key(_key32_of_key16(key16))


def _dsa_kernel(a_ref, kc_ref, wk_ref, qa_ref, ka_ref, vat_ref, o_ref,
                scores_ref, hb_ref, red_ref, fold_ref, thr_ref, cge_ref, s0_ref, s1_ref, acc_ref, *, topk, seq_bits):
    j = pl.program_id(1)
    n_tiles = j + 1
    w_rows = wk_ref[0].T[IDX_DIM:IDX_DIM + IDX_HEADS, :]
    w8 = [jnp.broadcast_to(w_rows[hh:hh + 1, :], (SUBLANES, TQ)) for hh in range(IDX_HEADS)]

    def score_tile(c, maybe_diagonal):
        k0 = pl.multiple_of(c * TQ, TQ)
        kc = kc_ref[0, pl.ds(k0, TQ), :]
        acc = jnp.zeros((TQ // SUBLANES, SUBLANES, TQ), F32)
        for hh in range(IDX_HEADS):
            lg = _dot_nt(kc, a_ref[0, :, hh * IDX_K:(hh + 1) * IDX_K])
            acc = acc + w8[hh][None] * jnp.maximum(_tile3(lg), 0.0)
        if maybe_diagonal:
            acc = jnp.where(_edge_mask(j, c), acc, NEG)
        acc = acc.reshape(TQ, TQ)
        scores_ref[pl.ds(k0, TQ), :] = acc
        hb_ref[pl.ds(k0, TQ), :] = acc.astype(BF16)

    def score_run(start, n_trips, width):
        def trip(g, carry):
            for u in range(width):
                score_tile(start + width * g + u, u == width - 1)
            return carry

        lax.fori_loop(0, n_trips, trip, 0)
        return start + width * n_trips

    done = 0
    for width in SCORE_GROUPS:
        done = score_run(done, (n_tiles - done) // width, width)

    def load_packed(c, u):
        r0 = pl.multiple_of(c * TQ + u * SEL_ROWS, SEL_ROWS)
        return hb_ref[pl.ds(r0, SEL_ROWS), :].reshape(SEL_ROWS // PACK_ROWS, PACK_ROWS, TQ)

    def load_scores(c, u):
        r0 = pl.multiple_of(c * TQ + u * SEL_ROWS, SEL_ROWS)
        return _tile3(scores_ref[pl.ds(r0, SEL_ROWS), :]), r0

    def sweep_count(tile, zero):
        cnt = lax.fori_loop(0, n_tiles // 2, lambda p, cnt: tile(2 * p + 1, tile(2 * p, cnt)), zero)
        return lax.fori_loop(2 * (n_tiles // 2), n_tiles, tile, cnt)

    one16, zero16 = jnp.int16(1), jnp.int16(0)

    def count_bf16(test):
        t = jnp.broadcast_to(test, (PACK_ROWS, TQ)).astype(BF16)[None]

        def tile(c, cnt):
            for u in range(TQ // SEL_ROWS):
                cnt = cnt + jnp.where(load_packed(c, u) >= t, one16, zero16)
            return cnt

        cnt = sweep_count(tile, jnp.zeros((SEL_ROWS // PACK_ROWS, PACK_ROWS, TQ), I16))
        return jnp.sum(jnp.sum(cnt.astype(F32), axis=0), axis=0, keepdims=True)

    def count_f32(pred):
        def tile(c, cnt):
            for u in range(TQ // SEL_ROWS):
                sc, r0 = load_scores(c, u)
                cnt = cnt + jnp.where(pred(sc, r0), 1.0, 0.0)
            return cnt

        cnt = sweep_count(tile, jnp.zeros((SEL_ROWS // SUBLANES, SUBLANES, TQ), F32))
        return jnp.sum(jnp.sum(cnt, axis=0), axis=0, keepdims=True)

    kf = float(topk)

    c0 = count_bf16(jnp.zeros((1, TQ), F32))
    start = jnp.where(c0 >= kf, 0, I16_MIN).astype(I32)

    def bit_step(i, val):
        test = val | (jnp.int32(1) << (14 - i))
        return jnp.where(count_bf16(_bf16_from_key16(test)) >= kf, test, val)

    hb_k = lax.fori_loop(0, 15, bit_step, start)

    below = _key32_of_key16(jnp.maximum(hb_k - 1, BF16_KEY_NEG_INF))
    lo = below + ((_key32_of_key16(hb_k) - below) >> 1) - 1
    hi = _key32_of_key16(jnp.minimum(hb_k + 1, BF16_KEY_POS_INF))

    def bisect(count_ge):
        def step(i, carry):
            lo, hi, cge = carry
            mid = lo + ((hi - lo) >> 1)
            c = count_ge(_f32_from_key(mid)[None])
            ok = c >= kf
            return jnp.where(ok, mid, lo), jnp.where(ok, hi, mid), jnp.where(ok, c, cge)

        lo_k, _, cge = lax.fori_loop(0, BISECT_STEPS, step, (lo, hi, jnp.full((1, TQ), kf, F32)))
        thr_ref[...] = jnp.broadcast_to(_f32_from_key(lo_k), (SUBLANES, TQ))
        cge_ref[...] = jnp.broadcast_to(cge, (SUBLANES, TQ))

    lo_f, hi_f = _f32_from_key(lo), _f32_from_key(hi)
    ninf = jnp.full((SUBLANES, TQ), -jnp.inf, F32)

    def reduce_tile(c, carry):
        above, spill = carry
        k0 = pl.multiple_of(c * TQ, TQ)
        x3 = _tile3(scores_ref[pl.ds(k0, TQ), :])
        top = [ninf] * KEEP
        for g in range(TQ // SUBLANES):
            x = x3[g]
            ge_hi = x >= hi_f
            above = above + jnp.where(ge_hi, 1.0, 0.0)
            t = jnp.where((x >= lo_f) & ~ge_hi, x, -jnp.inf)
            for r in range(KEEP):
                top[r], t = jnp.maximum(top[r], t), jnp.minimum(top[r], t)
            spill = jnp.maximum(spill, t)
        for r in range(KEEP):
            red_ref[c, r] = top[r]
        return above, spill

    above, spill = lax.fori_loop(0, n_tiles, reduce_tile, (jnp.zeros((SUBLANES, TQ), F32), ninf))
    above = jnp.sum(above, axis=0, keepdims=True)
    reduced_ok = jnp.max(spill) == -jnp.inf

    @pl.when(reduced_ok)
    def _():
        def fold_tile(c, carry):
            top, spill2 = carry
            for r in range(KEEP):
                t = red_ref[c, r]
                nxt = []
                for q in range(KEEP_ALL):
                    nxt.append(jnp.maximum(top[q], t))
                    t = jnp.minimum(top[q], t)
                top = tuple(nxt)
                spill2 = jnp.maximum(spill2, t)
            return top, spill2

        top, spill2 = lax.fori_loop(0, n_tiles, fold_tile, ((ninf,) * KEEP_ALL, ninf))
        for q in range(KEEP_ALL):
            fold_ref[q] = top[q]
        folded_ok = jnp.max(spill2) == -jnp.inf

        @pl.when(folded_ok)
        def _():
            def count_ge(t3):
                cnt = jnp.zeros((SUBLANES, TQ), F32)
                for q in range(KEEP_ALL):
                    cnt = cnt + jnp.where(fold_ref[q] >= t3[0], 1.0, 0.0)
                return above + jnp.sum(cnt, axis=0, keepdims=True)
            bisect(count_ge)

        @pl.when(jnp.logical_not(folded_ok))
        def _():
            def count_ge(t3):
                def tile(c, cnt):
                    for r in range(KEEP):
                        cnt = cnt + jnp.where(red_ref[c, r] >= t3[0], 1.0, 0.0)
                    return cnt
                cnt = sweep_count(tile, jnp.zeros((SUBLANES, TQ), F32))
                return above + jnp.sum(cnt, axis=0, keepdims=True)
            bisect(count_ge)

    @pl.when(jnp.logical_not(reduced_ok))
    def _():
        bisect(lambda t3: count_f32(lambda sc, r0: sc >= t3))

    thr = thr_ref[0:1, :]
    cge = cge_ref[0:1, :]

    @pl.when(jnp.max(jnp.abs(cge - kf)) > 0.0)
    def _():
        thr3 = thr[None]
        need = kf - count_f32(lambda sc, r0: sc > thr3)
        sub = (lax.broadcasted_iota(I32, (SEL_ROWS // SUBLANES, SUBLANES, TQ), 0) * SUBLANES
               + lax.broadcasted_iota(I32, (SEL_ROWS // SUBLANES, SUBLANES, TQ), 1))

        def cut_step(i, cut):
            test = cut | (jnp.int32(1) << (seq_bits - 1 - i))
            f = count_f32(lambda sc, r0: (sc == thr3) & (r0 + sub < test[None]))
            return jnp.where(f < need, test, cut)

        cut = lax.fori_loop(0, seq_bits, cut_step, jnp.zeros((1, TQ), I32))

        def demote(c, carry):
            r0 = pl.multiple_of(c * SEL_ROWS, SEL_ROWS)
            sc = _tile3(scores_ref[pl.ds(r0, SEL_ROWS), :])
            scores_ref[pl.ds(r0, SEL_ROWS), :] = jnp.where(
                (sc == thr3) & (r0 + sub > cut[None]), NEG, sc).reshape(SEL_ROWS, TQ)
            return carry

        lax.fori_loop(0, n_tiles * (TQ // SEL_ROWS), demote, 0)

    def consume(t, t_true, s_ref, state, edge):
        k0 = pl.multiple_of(t * TQ, TQ)
        sel = _tile3(scores_ref[pl.ds(k0, TQ), :]) >= thr[None]
        if edge:
            sel = sel & _edge_mask(j, t_true)
        bias = jnp.where(sel, 0.0, NEG)
        return tuple(
            _softmax_tile(_tile3(s_ref[hh]) + bias, vat_ref[0, t, hh * VT_ROWS:(hh + 1) * VT_ROWS, :],
                          state[hh], acc_ref, hh)
            for hh in range(N_HEADS))

    qk = functools.partial(_qk_tile, ka_ref, qa_ref)
    _sweep_tiles(j, (s0_ref, s1_ref), qk, consume, _init_state(acc_ref))
    _write_heads(o_ref, acc_ref)


def _dsa_call(a_mat, kc, idx_out, qa, ka, vat):
    bsz, seq, _ = qa.shape
    topk = min(DSA_TOPK_MAX, seq // 4)
    seq_bits = (seq - 1).bit_length()
    kern = functools.partial(_dsa_kernel, topk=topk, seq_bits=seq_bits)
    return pl.pallas_call(
        kern,
        grid=(bsz, seq // TQ),
        in_specs=[
            pl.BlockSpec((1, TQ, IDX_HEADS * IDX_K), lambda b, j: (b, j, 0)),
            pl.BlockSpec((1, seq, IDX_K), lambda b, j: (b, 0, 0)),
            pl.BlockSpec((1, TQ, LANES), lambda b, j: (b, j, 0)),
            pl.BlockSpec((1, TQ, BR_W), lambda b, j: (b, j, 0)),
            pl.BlockSpec((1, seq, BR_W), lambda b, j: (b, 0, 0)),
            pl.BlockSpec((1, seq // TQ, N_HEADS * VT_ROWS, TQ), lambda b, j: (b, 0, 0, 0)),
        ],
        out_specs=pl.BlockSpec((1, TQ, BR_W), lambda b, j: (b, j, 0)),
        out_shape=jax.ShapeDtypeStruct((bsz, seq, BR_W), BF16),
        scratch_shapes=[
            pltpu.VMEM((seq, TQ), F32),
            pltpu.VMEM((seq, TQ), BF16),
            pltpu.VMEM((seq // TQ, KEEP, SUBLANES, TQ), F32),
            pltpu.VMEM((KEEP_ALL, SUBLANES, TQ), F32),
            pltpu.VMEM((SUBLANES, TQ), F32),
            pltpu.VMEM((SUBLANES, TQ), F32),
            pltpu.VMEM((N_HEADS, TQ, TQ), F32),
            pltpu.VMEM((N_HEADS, TQ, TQ), F32),
            pltpu.VMEM((N_HEADS * VT_ROWS, TQ), F32),
        ],
        compiler_params=pltpu.CompilerParams(
            dimension_semantics=("arbitrary", "arbitrary"), vmem_limit_bytes=VMEM_LIMIT),
        name="dsa",
    )(a_mat, kc, idx_out, qa, ka, vat)


def _moba_kernel(qb_ref, kb_ref, vbt_ref, kmean_ref, o_ref, bias_ref, s0_ref, s1_ref, acc_ref, *, n_sel):
    j = pl.program_id(1)
    nb = kmean_ref.shape[1]
    blk = lax.broadcasted_iota(I32, (nb, TQ), 0).astype(F32)
    jf = j.astype(F32)

    for hh in range(N_HEADS):
        hs = slice(hh * HEAD_DIM, (hh + 1) * HEAD_DIM)
        km_hi, km_lo = _split_bf16(kmean_ref[0, :, hs])
        q = qb_ref[0, :, hs]
        gate = jnp.where(blk < jf, _dot_nt(jnp.concatenate([km_hi, km_lo], axis=1),
                                           jnp.concatenate([q, q], axis=1)), NEG)
        chosen = jnp.zeros((nb, TQ), F32)
        for _ in range(n_sel):
            best = jnp.max(gate, axis=0, keepdims=True)
            first = jnp.min(jnp.where(gate == best, blk, float(nb)), axis=0, keepdims=True)
            hit = blk == first
            chosen = jnp.where(hit, 1.0, chosen)
            gate = jnp.where(hit, -jnp.inf, gate)
        bias = jnp.where((chosen > 0.0) & (blk < jf), 0.0, NEG)
        for n in range(nb):
            bias_ref[hh, n] = jnp.broadcast_to(bias[n:n + 1, :], (SUBLANES, TQ))

    def consume(t, t_true, s_ref, state, edge):
        new = []
        for hh in range(N_HEADS):
            if edge:
                s3 = jnp.where(_edge_mask(j, t_true), _tile3(s_ref[hh]), NEG)
            else:
                s3 = _tile3(s_ref[hh]) + bias_ref[hh, t][None]
            new.append(_softmax_tile(s3, vbt_ref[0, t, hh * VT_ROWS:(hh + 1) * VT_ROWS, :], state[hh], acc_ref, hh))
        return tuple(new)

    qk = functools.partial(_qk_tile, kb_ref, qb_ref)
    _sweep_tiles(j, (s0_ref, s1_ref), qk, consume, _init_state(acc_ref))
    _write_heads(o_ref, acc_ref)


def _moba_call(qb, kb, vbt, kmean):
    bsz, seq, _ = qb.shape
    nb = seq // MOBA_BLOCK
    kern = functools.partial(_moba_kernel, n_sel=min(MOBA_TOPK, nb - 1))
    return pl.pallas_call(
        kern,
        grid=(bsz, seq // TQ),
        in_specs=[
            pl.BlockSpec((1, TQ, BR_W), lambda b, j: (b, j, 0)),
            pl.BlockSpec((1, seq, BR_W), lambda b, j: (b, 0, 0)),
            pl.BlockSpec((1, seq // TQ, N_HEADS * VT_ROWS, TQ), lambda b, j: (b, 0, 0, 0)),
            pl.BlockSpec((1, nb, BR_W), lambda b, j: (b, 0, 0)),
        ],
        out_specs=pl.BlockSpec((1, TQ, BR_W), lambda b, j: (b, j, 0)),
        out_shape=jax.ShapeDtypeStruct((bsz, seq, BR_W), BF16),
        scratch_shapes=[
            pltpu.VMEM((N_HEADS, nb, SUBLANES, TQ), F32),
            pltpu.VMEM((N_HEADS, TQ, TQ), F32),
            pltpu.VMEM((N_HEADS, TQ, TQ), F32),
            pltpu.VMEM((N_HEADS * VT_ROWS, TQ), F32),
        ],
        compiler_params=pltpu.CompilerParams(
            dimension_semantics=("arbitrary", "arbitrary"), vmem_limit_bytes=VMEM_LIMIT),
        name="moba",
    )(qb, kb, vbt, kmean)


def _out_ffn_kernel(x_ref, oa_ref, ob_ref, ga_ref, gb_ref, mod_ref, gffn_ref,
                    wba_ref, wbb_ref, wo_ref, wgu_ref, wd_ref, o_ref, act_ref):
    merged = (ga_ref[0].astype(F32) * _dot(oa_ref[0], wba_ref[...])
              + gb_ref[0].astype(F32) * _dot(ob_ref[0], wbb_ref[...]))
    x1 = x_ref[0] + mod_ref[0, 2:3, :] * _dot(merged.astype(BF16), wo_ref[...])
    y = x1 * lax.rsqrt(jnp.mean(x1 * x1, axis=-1, keepdims=True) + EPS) * gffn_ref[...]
    h = (y * (1.0 + mod_ref[0, 4:5, :]) + mod_ref[0, 3:4, :]).astype(BF16)
    for c0 in range(0, D_FF, FF_CHUNK):
        g = _dot(h, wgu_ref[:, c0:c0 + FF_CHUNK])
        u = _dot(h, wgu_ref[:, D_FF + c0:D_FF + c0 + FF_CHUNK])
        act_ref[:, c0:c0 + FF_CHUNK] = (g * jax.nn.sigmoid(g) * u).astype(BF16)
    o_ref[0] = x1 + mod_ref[0, 5:6, :] * _dot(act_ref[...], wd_ref[...])


def _out_ffn_call(x, oa, ob, ga, gb, mod3, gffn, wba, wbb, wo, wgu, wd):
    bsz, seq, d = x.shape
    tm = TM_FFN
    const = lambda b, i: (0, 0)
    tok = lambda b, i: (b, i, 0)
    resident = lambda a: pl.BlockSpec(a.shape, const, pipeline_mode=pl.Buffered(1))
    return pl.pallas_call(
        _out_ffn_kernel,
        grid=(bsz, seq // tm),
        in_specs=[
            pl.BlockSpec((1, tm, d), tok),
            pl.BlockSpec((1, tm, BR_W), tok),
            pl.BlockSpec((1, tm, BR_W), tok),
            pl.BlockSpec((1, tm, d), tok),
            pl.BlockSpec((1, tm, d), tok),
            pl.BlockSpec((1, N_MOD, d), lambda b, i: (b, 0, 0)),
            pl.BlockSpec((1, d), const),
            resident(wba), resident(wbb), resident(wo), resident(wgu), resident(wd),
        ],
        out_specs=pl.BlockSpec((1, tm, d), tok),
        out_shape=jax.ShapeDtypeStruct((bsz, seq, d), F32),
        scratch_shapes=[pltpu.VMEM((tm, D_FF), BF16)],
        compiler_params=pltpu.CompilerParams(
            dimension_semantics=("arbitrary", "arbitrary"), vmem_limit_bytes=VMEM_LIMIT),
        name="out_ffn",
    )(x, oa, ob, ga, gb, mod3, gffn, wba, wbb, wo, wgu, wd)


def _rope_tables(seq, dim):
    inv = ROPE_THETA ** (-jnp.arange(0, dim, 2, dtype=F32) / dim)
    ang = jnp.arange(seq, dtype=F32)[:, None] * inv[None, :]
    cos, sin = jnp.cos(ang), jnp.sin(ang)
    reps = LANES // dim
    return (jnp.tile(jnp.concatenate([cos, cos], axis=-1), (1, reps)),
            jnp.tile(jnp.concatenate([-sin, sin], axis=-1), (1, reps)))


def _layer(x, mod, g_mix, g_ffn, w_in, g_qa, g_ka, g_qb, g_kb, w_br_a, w_br_b, w_out, w_gu, w_down):
    bsz, seq, d = x.shape
    mod3 = mod.reshape(bsz, N_MOD, d)

    o = 0
    cols = {}
    for name, width in (("qa", BR_W), ("ka", BR_W), ("va", BR_W), ("qi", IDX_HEADS * IDX_DIM),
                        ("ki", IDX_DIM), ("wi", IDX_HEADS), ("qb", BR_W), ("kb", BR_W), ("vb", BR_W),
                        ("ga", D_MODEL), ("gb", D_MODEL)):
        cols[name] = w_in[:, o:o + width]
        o += width
    wqk = jnp.concatenate([cols["qa"], cols["ka"], cols["qb"], cols["kb"]], axis=1).astype(BF16)
    wvt = jnp.concatenate([cols["va"], cols["vb"]], axis=1).T.astype(BF16)
    wg = jnp.concatenate([cols["ga"], cols["gb"]], axis=1).astype(BF16)
    pad = IDX_OUT_W - IDX_HEADS * IDX_DIM - IDX_DIM - IDX_HEADS
    wi = jnp.concatenate([cols["qi"], cols["ki"], cols["wi"], jnp.zeros((d, pad), F32)], axis=1).astype(BF16)
    gains = jnp.stack([g_qa, g_ka, g_qb, g_kb])
    cosh, sinh = _rope_tables(seq, HEAD_DIM)
    cosi, sini = _rope_tables(seq, IDX_DIM)

    qa, ka, qb, kb, vat, vbt, ga, gb, a_mat, kc, kw, kmean = _proj_call(
        x, mod3, g_mix.reshape(1, d), wqk, wvt, wg, wi, gains, cosh, sinh, cosi, sini)

    oa = _dsa_call(a_mat, kc, kw, qa, ka, vat)
    ob = _moba_call(qb, kb, vbt, kmean.reshape(bsz, seq // MOBA_BLOCK, BR_W))

    return _out_ffn_call(x, oa, ob, ga, gb, mod3, g_ffn.reshape(1, d),
                         w_br_a.astype(BF16), w_br_b.astype(BF16), w_out.astype(BF16),
                         w_gu.astype(BF16), w_down.astype(BF16))


def kernel(x, c, w_mod, b_mod, g_mix_norm, g_ffn_norm, w_in, g_q_dsa, g_k_dsa, g_q_moba, g_k_moba,
           w_br_dsa, w_br_moba, w_out, w_gate_up, w_down):
    assert x.shape[1] % TM_PROJ == 0 and x.shape[2] == D_MODEL and w_gate_up.shape[2] == 2 * D_FF
    for l in range(w_mod.shape[0]):
        mod = _mod_call(c, w_mod[l], b_mod[l])
        x = _layer(x, mod, g_mix_norm[l], g_ffn_norm[l], w_in[l], g_q_dsa[l], g_k_dsa[l], g_q_moba[l],
                   g_k_moba[l], w_br_dsa[l], w_br_moba[l], w_out[l], w_gate_up[l], w_down[l])
    return x
```
